```python
import math
import jax
import jax.numpy as jnp
from jax import lax
import numpy as np

D_MODEL = 2048
BATCH = 2
SEQ = 4096
DEPTH = 4
DEC_BATCH = 8
DEC_SEQ = 4
PAST_LEN = 16384
PAGE_SIZE = 128

H_A = D_MODEL // 128
HEAD_A = 64
D_A = H_A * HEAD_A
DECAY_LORA = 64
A_LORA = 64
GATE_LORA = 160
LNX_EPS = 64e-5
H_B = D_MODEL // 256
N_KV = 2
HPG = H_B // N_KV
HEAD_B = 128
D_B = H_B * HEAD_B
L_CMP = 32
S_CMP = 16
L_SEL = 64
TOP_N = 16
WINDOW = 512
Q_BLOCK = 128
FORCE_BONUS = 1e4
NEG = -1e30
NUM_BUCKETS = 32
MAX_DIST = 128
D_FF = 4 * D_MODEL
EPS = 1e-6
RWKV_PROJ = 3 * D_A + DECAY_LORA + A_LORA + GATE_LORA
NSA_PROJ = D_B + 6 * N_KV * HEAD_B + 3 * H_B
D_IN = RWKV_PROJ + NSA_PROJ + 2 * D_MODEL

kernel_name = "rwkv7_nsa_gated_hybrid_step"


def rmsnorm(x, g):
    xf = x.astype(jnp.float32)
    y = xf * lax.rsqrt(jnp.mean(xf * xf, axis=-1, keepdims=True) + EPS)
    return (y * g.astype(jnp.float32)).astype(x.dtype)


def masked_softmax(s, mask):
    s = jnp.where(mask, s.astype(jnp.float32), NEG)
    m = jnp.max(s, axis=-1, keepdims=True)
    e = jnp.where(mask, jnp.exp(s - m), 0.0)
    return e / jnp.maximum(jnp.sum(e, axis=-1, keepdims=True), 1e-30)


def t5_bucket(dist):
    d = jnp.maximum(dist, 0)
    max_exact = NUM_BUCKETS // 2
    df = jnp.maximum(d, 1).astype(jnp.float32)
    large = max_exact + (jnp.log(df / max_exact) / math.log(MAX_DIST / max_exact)
                         * (NUM_BUCKETS - max_exact)).astype(jnp.int32)
    large = jnp.minimum(large, NUM_BUCKETS - 1)
    return jnp.where(d < max_exact, d, large)


def rwkv7_mix(pa, prev, s0, mu, w0, w_up, a0, a_up, g_up, k_k, k_a, r_k, lnx_w, lnx_b, w_oa):
    B, T, _ = pa.shape
    f32 = jnp.float32
    shifted = jnp.concatenate([prev[:, None].astype(pa.dtype), pa[:, :-1]], axis=1)
    xs = pa + (shifted - pa) * mu
    r, k, v, wd, ad, gd = jnp.split(
        xs, [D_A, 2 * D_A, 3 * D_A, 3 * D_A + DECAY_LORA, 3 * D_A + DECAY_LORA + A_LORA], axis=-1)
    w = -jax.nn.softplus(-(w0 + jnp.tanh(wd) @ w_up)) - 0.5
    decay = jnp.exp(-jnp.exp(w.astype(f32)))
    a = jax.nn.sigmoid(a0 + ad @ a_up)
    g = jax.nn.sigmoid(gd) @ g_up
    hs = lambda z: z.reshape(B, T, H_A, HEAD_A).astype(f32)
    kk = hs(k * k_k)
    kk = kk / jnp.maximum(jnp.sqrt(jnp.sum(kk * kk, axis=-1, keepdims=True)), 1e-12)
    k2 = k * (1 + (a - 1) * k_a)
    rh, kh, vh, ah, wh = hs(r), hs(k2), hs(v), hs(a), decay.reshape(B, T, H_A, HEAD_A)

    def step(S, inp):
        r_t, w_t, k_t, v_t, kk_t, a_t = inp
        sk = jnp.einsum('bhvk,bhk->bhv', S, kk_t)
        S = (S * w_t[:, :, None, :] - sk[..., None] * (kk_t * a_t)[:, :, None, :]
             + v_t[..., None] * k_t[:, :, None, :])
        return S, jnp.einsum('bhvk,bhk->bhv', S, r_t)

    seq = tuple(jnp.moveaxis(z, 1, 0) for z in (rh, wh, kh, vh, kk, ah))
    s_fin, out = lax.scan(step, s0.astype(f32), seq)
    out = jnp.moveaxis(out, 0, 1)
    mean = jnp.mean(out, axis=-1, keepdims=True)
    var = jnp.mean(jnp.square(out - mean), axis=-1, keepdims=True)
    xo = ((out - mean) * lax.rsqrt(var + LNX_EPS)).reshape(B, T, D_A) * lnx_w + lnx_b
    bonus = jnp.sum(rh * kh * r_k, axis=-1, keepdims=True) * vh
    xo = xo + bonus.reshape(B, T, D_A)
    y = (xo * g).astype(pa.dtype) @ w_oa
    return y, pa[:, -1], s_fin.astype(s0.dtype)


def compress(kv, pe, w1, w2):
    B, L = kv.shape[:2]
    n_chunks = -(-L // S_CMP)
    kv = jnp.pad(kv, ((0, 0), (0, n_chunks * S_CMP - L), (0, 0), (0, 0)))
    ch = kv.reshape(B, n_chunks, S_CMP, N_KV, HEAD_B)
    blk = jnp.concatenate([ch[:, :-1], ch[:, 1:]], axis=2) + pe[None, None, :, None, :]
    flat = blk.transpose(0, 1, 3, 2, 4).reshape(B, n_chunks - 1, N_KV, L_CMP * HEAD_B)
    return jax.nn.gelu(flat @ w1) @ w2


def nsa_mix(pb, past, win_buf, rel_tb, cmp_pe, cmp_w1, cmp_w2, w_ob):
    B, T, _ = pb.shape
    q, kvn, gl = jnp.split(pb, [D_B, D_B + 6 * N_KV * HEAD_B], axis=-1)
    q = q.reshape(B, T, N_KV, HPG, HEAD_B)
    kvn = kvn.reshape(B, T, 6, N_KV, HEAD_B)
    new_rows = kvn[:, :, :4]
    new_win = kvn[:, :, 4:]
    full = new_rows if past is None else jnp.concatenate([past.astype(new_rows.dtype), new_rows], axis=1)
    L = full.shape[1]
    p0 = L - T
    pos = p0 + jnp.arange(T)
    scale = HEAD_B ** -0.5
    qb = Q_BLOCK if T % Q_BLOCK == 0 else T
    nq = T // qb

    kc = compress(full[:, :, 0], cmp_pe[0], cmp_w1[0], cmp_w2[0])
    vc = compress(full[:, :, 1], cmp_pe[1], cmp_w1[1], cmp_w2[1])
    nc = kc.shape[1]
    c_end = S_CMP * jnp.arange(nc) + L_CMP - 1
    dist_c = pos[:, None] - c_end[None, :]
    bias_c = rel_tb[:, t5_bucket(dist_c)].transpose(0, 3, 1, 2)
    s_c = jnp.einsum('btghd,bngd->bghtn', q, kc).astype(jnp.float32) * scale + bias_c[None]
    p_c = masked_softmax(s_c, (dist_c >= 0)[None, None, None])
    o_c = jnp.einsum('bghtn,bngd->btghd', p_c.astype(vc.dtype), vc)

    ns = -(-L // L_SEL)
    ci = jnp.arange(nc)
    sj = jnp.arange(ns)
    overlap = (((S_CMP * ci) // L_SEL)[:, None] == sj[None]) | \
              (((S_CMP * ci + L_CMP - 1) // L_SEL)[:, None] == sj[None])
    imp = jnp.einsum('bghtn,ns->bgts', p_c, overlap.astype(jnp.float32))
    cur = pos // L_SEL
    valid_blk = sj[None, :] <= cur[:, None]
    forced = (sj[None, :] == 0) | (sj[None, :] == cur[:, None]) | (sj[None, :] == cur[:, None] - 1)
    score = jnp.where(valid_blk, imp + FORCE_BONUS * forced.astype(jnp.float32), NEG)
    n_top = min(TOP_N, ns)
    top_s, top_i = lax.top_k(score, n_top)
    top_ok = top_s > NEG / 2
    pad_sel = ((0, 0), (0, ns * L_SEL - L), (0, 0), (0, 0))
    ks = jnp.pad(full[:, :, 2], pad_sel).reshape(B, ns, L_SEL, N_KV, HEAD_B).transpose(0, 3, 1, 2, 4)
    vs = jnp.pad(full[:, :, 3], pad_sel).reshape(B, ns, L_SEL, N_KV, HEAD_B).transpose(0, 3, 1, 2, 4)
    take = jax.vmap(jax.vmap(lambda blocks, ix: blocks[ix]))
    take_bias = jax.vmap(lambda tb, bk: tb[bk], in_axes=(0, 1), out_axes=1)

    def sel_block(args):
        q_b, pos_b, i_b, ok_b = args
        nqb = pos_b.shape[0]
        kb = take(ks, i_b)
        vb = take(vs, i_b)
        tok = i_b[..., None] * L_SEL + jnp.arange(L_SEL)
        dist = pos_b[None, None, :, None, None] - tok
        mask = (ok_b[..., None] & (dist >= 0)).reshape(B, N_KV, 1, nqb, n_top * L_SEL)
        bias = jnp.moveaxis(take_bias(rel_tb, t5_bucket(dist)), -1, 2)
        s = jnp.einsum('bqghd,bgqkld->bghqkl', q_b, kb).astype(jnp.float32) * scale + bias
        p = masked_softmax(s.reshape(B, N_KV, HPG, nqb, n_top * L_SEL), mask)
        return jnp.einsum('bghqm,bgqmd->bqghd', p.astype(vb.dtype),
                          vb.reshape(B, N_KV, nqb, n_top * L_SEL, HEAD_B))

    q_blocks = jnp.moveaxis(q.reshape(B, nq, qb, N_KV, HPG, HEAD_B), 1, 0)
    i_blocks = jnp.moveaxis(top_i.reshape(B, N_KV, nq, qb, n_top), 2, 0)
    ok_blocks = jnp.moveaxis(top_ok.reshape(B, N_KV, nq, qb, n_top), 2, 0)
    o_s = lax.map(sel_block, (q_blocks, pos.reshape(nq, qb), i_blocks, ok_blocks))
    o_s = jnp.moveaxis(o_s, 0, 1).reshape(B, T, N_KV, HPG, HEAD_B)

    wctx = new_win if win_buf is None else jnp.concatenate([win_buf.astype(new_win.dtype), new_win], axis=1)
    wb_prev = wctx.shape[1] - T
    ctx = jnp.pad(wctx, ((0, 0), (WINDOW - wb_prev, 0), (0, 0), (0, 0), (0, 0)))
    span = WINDOW + qb
    idx = (jnp.arange(nq) * qb)[:, None] + jnp.arange(span)[None, :]
    kw = ctx[:, :, 0][:, idx]
    vw = ctx[:, :, 1][:, idx]
    kpos = p0 - WINDOW + idx
    dist_w = pos.reshape(nq, qb)[:, :, None] - kpos[:, None, :]
    mask_w = (dist_w >= 0) & (dist_w < WINDOW) & (kpos[:, None, :] >= 0)
    bias_w = rel_tb[:, t5_bucket(dist_w)].transpose(1, 0, 4, 2, 3)
    qw = q.reshape(B, nq, qb, N_KV, HPG, HEAD_B)
    s_w = jnp.einsum('bnqghd,bnsgd->bnghqs', qw, kw).astype(jnp.float32) * scale + bias_w[None]
    p_w = masked_softmax(s_w, mask_w[None, :, None, None])
    o_w = jnp.einsum('bnghqs,bnsgd->bnqghd', p_w.astype(vw.dtype), vw).reshape(B, T, N_KV, HPG, HEAD_B)
    n_keep = min(WINDOW, wctx.shape[1])
    new_buf = wctx[:, wctx.shape[1] - n_keep:]

    gates = jax.nn.sigmoid(gl.reshape(B, T, 3, N_KV, HPG))
    o = (gates[:, :, 0][..., None] * o_c + gates[:, :, 1][..., None] * o_s
         + gates[:, :, 2][..., None] * o_w)
    y = o.reshape(B, T, D_B) @ w_ob
    return y, new_rows, new_buf


def trunk(x, shift0, wkv0, cache_kv, cache_win, page_table, rel_tb, W):
    rows, wins, shifts, wkvs = [], [], [], []
    for l in range(DEPTH):
        g = W['norm_g'][l]
        h = rmsnorm(x, g[0])
        proj = h @ W['w_in'][l]
        pa, pb, pg = jnp.split(proj, [RWKV_PROJ, RWKV_PROJ + NSA_PROJ], axis=-1)
        ya, sh, st = rwkv7_mix(pa, shift0[l], wkv0[l], W['mu'][l], W['w0'][l], W['w_up'][l],
                               W['a0'][l], W['a_up'][l], W['g_up'][l], W['k_k'][l], W['k_a'][l],
                               W['r_k'][l], W['lnx_w'][l], W['lnx_b'][l], W['w_oa'][l])
        if cache_kv is None:
            past, win = None, None
        else:
            pages = cache_kv[l][page_table]
            past = pages.reshape(pages.shape[0], pages.shape[1] * pages.shape[2], *pages.shape[3:])
            win = cache_win[l]
        yb, nr, nw = nsa_mix(pb, past, win, rel_tb, W['cmp_pe'][l], W['cmp_w1'][l],
                             W['cmp_w2'][l], W['w_ob'][l])
        ga, gb = jnp.split(jax.nn.sigmoid(pg), 2, axis=-1)
        mix = (ga * ya + gb * yb) @ W['w_o'][l]
        x = x + rmsnorm(mix, g[1])
        h2 = rmsnorm(x, g[2])
        f = jnp.square(jax.nn.relu(h2 @ W['w_ff_up'][l])) @ W['w_ff_down'][l]
        x = x + rmsnorm(f, g[3])
        rows.append(nr)
        wins.append(nw)
        shifts.append(sh)
        wkvs.append(st)
    return x, jnp.stack(rows), jnp.stack(wins), jnp.stack(shifts), jnp.stack(wkvs)


def setup_inputs(seed: int = 0) -> dict:
    key = jax.random.key(seed)
    ks = jax.random.split(key, 32)
    n_pages = PAST_LEN // PAGE_SIZE
    n_used = DEC_BATCH * n_pages
    n_pool = n_used + max(1, n_used // 4)
    win_buf = min(WINDOW, PAST_LEN)
    nrm = lambda k, shape, s: jax.random.normal(k, shape, jnp.float32) * s
    page_table = jax.random.permutation(ks[0], n_pool)[:n_used].reshape(DEC_BATCH, n_pages).astype(jnp.int32)
    return {
        'x_prompt': nrm(ks[1], (BATCH, SEQ, D_MODEL), 1.0),
        'x_sample': nrm(ks[2], (DEC_BATCH, DEC_SEQ, D_MODEL), 1.0),
        'cache_kv': nrm(ks[3], (DEPTH, n_pool, PAGE_SIZE, 4, N_KV, HEAD_B), 1.0),
        'cache_win': nrm(ks[4], (DEPTH, DEC_BATCH, win_buf, 2, N_KV, HEAD_B), 1.0),
        'state_shift': nrm(ks[5], (DEPTH, DEC_BATCH, RWKV_PROJ), 1.0),
        'state_wkv': nrm(ks[6], (DEPTH, DEC_BATCH, H_A, HEAD_A, HEAD_A), 0.3),
        'page_table': page_table,
        'w_in': nrm(ks[7], (DEPTH, D_MODEL, D_IN), D_MODEL ** -0.5),
        'mu': jax.random.uniform(ks[8], (DEPTH, RWKV_PROJ), jnp.float32),
        'w0': jax.random.uniform(ks[9], (DEPTH, D_A), jnp.float32, -6.0, -1.0),
        'w_up': nrm(ks[10], (DEPTH, DECAY_LORA, D_A), 0.1 * DECAY_LORA ** -0.5),
        'a0': nrm(ks[11], (DEPTH, D_A), 0.1),
        'a_up': nrm(ks[12], (DEPTH, A_LORA, D_A), 0.5 * A_LORA ** -0.5),
        'g_up': nrm(ks[13], (DEPTH, GATE_LORA, D_A), GATE_LORA ** -0.5),
        'k_k': 1.0 + nrm(ks[14], (DEPTH, D_A), 0.05),
        'k_a': 1.0 + nrm(ks[15], (DEPTH, D_A), 0.05),
        'r_k': nrm(ks[16], (DEPTH, H_A, HEAD_A), 0.1),
        'lnx_w': 1.0 + nrm(ks[17], (DEPTH, D_A), 0.05),
        'lnx_b': nrm(ks[18], (DEPTH, D_A), 0.02),
        'w_oa': nrm(ks[19], (DEPTH, D_A, D_MODEL), D_A ** -0.5),
        'cmp_pe': nrm(ks[20], (DEPTH, 2, L_CMP, HEAD_B), 0.1),
        'cmp_w1': nrm(ks[21], (DEPTH, 2, L_CMP * HEAD_B, HEAD_B), (L_CMP * HEAD_B) ** -0.5),
        'cmp_w2': nrm(ks[22], (DEPTH, 2, HEAD_B, HEAD_B), HEAD_B ** -0.5),
        'w_ob': nrm(ks[23], (DEPTH, D_B, D_MODEL), D_B ** -0.5),
        'w_o': nrm(ks[24], (DEPTH, D_MODEL, D_MODEL), D_MODEL ** -0.5),
        'w_ff_up': nrm(ks[25], (DEPTH, D_MODEL, D_FF), D_MODEL ** -0.5),
        'w_ff_down': nrm(ks[26], (DEPTH, D_FF, D_MODEL), D_FF ** -0.5),
        'norm_g': 1.0 + nrm(ks[27], (DEPTH, 4, D_MODEL), 0.05),
        'rel_bias': nrm(ks[28], (NUM_BUCKETS, H_B), 0.5),
    }


def reference(x_prompt, x_sample, cache_kv, cache_win, state_shift, state_wkv, page_table,
              w_in, mu, w0, w_up, a0, a_up, g_up, k_k, k_a, r_k, lnx_w, lnx_b, w_oa,
              cmp_pe, cmp_w1, cmp_w2, w_ob, w_o, w_ff_up, w_ff_down, norm_g, rel_bias):
    W = {'w_in': w_in, 'mu': mu, 'w0': w0, 'w_up': w_up, 'a0': a0, 'a_up': a_up, 'g_up': g_up,
         'k_k': k_k, 'k_a': k_a, 'r_k': r_k, 'lnx_w': lnx_w, 'lnx_b': lnx_b, 'w_oa': w_oa,
         'cmp_pe': cmp_pe, 'cmp_w1': cmp_w1, 'cmp_w2': cmp_w2, 'w_ob': w_ob, 'w_o': w_o,
         'w_ff_up': w_ff_up, 'w_ff_down': w_ff_down, 'norm_g': norm_g}
    rel_tb = rel_bias.reshape(NUM_BUCKETS, N_KV, HPG).transpose(1, 0, 2)
    bp = x_prompt.shape[0]
    zeros_shift = jnp.zeros((DEPTH, bp, RWKV_PROJ), x_prompt.dtype)
    zeros_wkv = jnp.zeros((DEPTH, bp, H_A, HEAD_A, HEAD_A), state_wkv.dtype)
    y_prompt, kv_p, win_p, sh_p, wkv_p = trunk(x_prompt, zeros_shift, zeros_wkv, None, None, None, rel_tb, W)
    y_sample, kv_s, win_s, sh_s, wkv_s = trunk(x_sample, state_shift, state_wkv, cache_kv, cache_win,
                                               page_table, rel_tb, W)
    return (y_prompt, y_sample, kv_p, kv_s, win_p, win_s, sh_p, sh_s, wkv_p, wkv_s)
```

```python
import functools
import math

import numpy as np
import jax
import jax.numpy as jnp
from jax import lax
from jax.experimental import pallas as pl
from jax.experimental.pallas import tpu as pltpu

F32 = jnp.float32
BF16 = jnp.bfloat16

HEAD_A = 64
DECAY_LORA = 64
A_LORA = 64
GATE_LORA = 160
LNX_EPS = 64e-5
N_KV = 2
HEAD_B = 128
L_CMP = 32
S_CMP = 16
L_SEL = 64
TOP_N = 16
WINDOW = 512
FORCE_BONUS = 1e4
NEG = -1e30
NUM_BUCKETS = 32
MAX_DIST = 128
EPS = 1e-6
PAGE_SIZE = 128

LANE = 128
SUBLANE = 8
VMEM_LIMIT = 56 * 1024 * 1024
RWKV_CHUNK = 64
LORA_W = 384
FAR = LANE - 1


def _cp(n_axes):
    return pltpu.CompilerParams(dimension_semantics=("arbitrary",) * n_axes,
                                vmem_limit_bytes=VMEM_LIMIT)


def _round_up(x, m):
    return -(-x // m) * m


def _tile(n, pref, mult=SUBLANE):
    if n <= pref:
        return n
    for t in range(pref - pref % mult, 0, -mult):
        if n % t == 0:
            return t
    return n


def _dot(a, b):
    return jnp.dot(a, b, preferred_element_type=F32)


def _dot_nt(a, b):
    return lax.dot_general(a, b, (((1,), (1,)), ((), ())), preferred_element_type=F32)


def _dot_tn(a, b):
    return lax.dot_general(a, b, (((0,), (0,)), ((), ())), preferred_element_type=F32)


def _split(x):
    hi = x.astype(BF16)
    lo = (x - hi.astype(F32)).astype(BF16)
    return hi, lo


def _dot_x(a, b):
    hi, lo = _split(a)
    return _dot(hi, b) + _dot(lo, b)


def _dot3(a, b):
    ah, al = _split(a)
    bh, bl = _split(b)
    return _dot(ah, bh) + _dot(ah, bl) + _dot(al, bh)


def _dot3_nt(a, b):
    ah, al = _split(a)
    bh, bl = _split(b)
    return _dot_nt(ah, bh) + _dot_nt(ah, bl) + _dot_nt(al, bh)


def _dot3_tn(a, b):
    ah, al = _split(a)
    bh, bl = _split(b)
    return _dot_tn(ah, bh) + _dot_tn(ah, bl) + _dot_tn(al, bh)


def _sigmoid(x):
    return 1.0 / (1.0 + jnp.exp(-x))


def _rms(y, g):
    return y * lax.rsqrt(jnp.mean(y * y, axis=-1, keepdims=True) + EPS) * g


def _rmsnorm_kernel(x_ref, g_ref, o_ref):
    o_ref[...] = _rms(x_ref[...], g_ref[...]).astype(o_ref.dtype)


def _rmsnorm_cast(x, g):
    m, d = x.shape
    tm = _tile(m, 256)
    return pl.pallas_call(
        _rmsnorm_kernel, grid=(m // tm,),
        in_specs=[pl.BlockSpec((tm, d), lambda i: (i, 0)), pl.BlockSpec((1, d), lambda i: (0, 0))],
        out_specs=pl.BlockSpec((tm, d), lambda i: (i, 0)),
        out_shape=jax.ShapeDtypeStruct((m, d), BF16), compiler_params=_cp(1), name="rmsnorm_cast",
    )(x, g)


def _mm_kernel(a_ref, w_ref, o_ref, *, relu2):
    y = _dot(a_ref[...], w_ref[...])
    if relu2:
        y = jnp.square(jnp.maximum(y, 0.0))
    o_ref[...] = y.astype(o_ref.dtype)


def _matmul(a, w, out_dtype, relu2=False, name="matmul"):
    m, k = a.shape
    n = w.shape[1]
    tm = _tile(m, 512)
    tn = _tile(n, 512, LANE)
    return pl.pallas_call(
        functools.partial(_mm_kernel, relu2=relu2), grid=(m // tm, n // tn),
        in_specs=[pl.BlockSpec((tm, k), lambda i, j: (i, 0)), pl.BlockSpec((k, tn), lambda i, j: (0, j))],
        out_specs=pl.BlockSpec((tm, tn), lambda i, j: (i, j)),
        out_shape=jax.ShapeDtypeStruct((m, n), out_dtype), compiler_params=_cp(2), name=name,
    )(a, w)


def _ffn_down_kernel(h_ref, w_ref, x_ref, g_ref, o_ref, acc_ref):
    k = pl.program_id(1)

    @pl.when(k == 0)
    def _():
        acc_ref[...] = jnp.zeros_like(acc_ref)

    acc_ref[...] += _dot(h_ref[...], w_ref[...])

    @pl.when(k == pl.num_programs(1) - 1)
    def _():
        o_ref[...] = x_ref[...] + _rms(acc_ref[...], g_ref[...])


def _ffn_down(h, w, x, g):
    m, kdim = h.shape
    n = w.shape[1]
    tm = _tile(m, 512)
    tk = _tile(kdim, 512, LANE)
    return pl.pallas_call(
        _ffn_down_kernel, grid=(m // tm, kdim // tk),
        in_specs=[pl.BlockSpec((tm, tk), lambda i, k: (i, k)), pl.BlockSpec((tk, n), lambda i, k: (k, 0)),
                  pl.BlockSpec((tm, n), lambda i, k: (i, 0)), pl.BlockSpec((1, n), lambda i, k: (0, 0))],
        out_specs=pl.BlockSpec((tm, n), lambda i, k: (i, 0)),
        out_shape=jax.ShapeDtypeStruct((m, n), F32),
        scratch_shapes=[pltpu.VMEM((tm, n), F32)], compiler_params=_cp(2), name="ffn_down",
    )(h, w, x, g)


def _merge_kernel(ya_ref, yb_ref, pg_ref, x_ref, w_ref, g_ref, o_ref):
    d = ya_ref.shape[1]
    pg = pg_ref[...]
    mix = _sigmoid(pg[:, :d]) * ya_ref[...] + _sigmoid(pg[:, d:]) * yb_ref[...]
    y = _dot(mix.astype(BF16), w_ref[...])
    o_ref[...] = x_ref[...] + _rms(y, g_ref[...])


def _merge(ya, yb, pg, x, w_o, g):
    m, d = x.shape
    tm = _tile(m, 256)
    row = lambda c: pl.BlockSpec((tm, c), lambda i: (i, 0))
    return pl.pallas_call(
        _merge_kernel, grid=(m // tm,),
        in_specs=[row(d), row(d), row(2 * d), row(d),
                  pl.BlockSpec((d, d), lambda i: (0, 0)), pl.BlockSpec((1, d), lambda i: (0, 0))],
        out_specs=row(d), out_shape=jax.ShapeDtypeStruct((m, d), F32),
        compiler_params=_cp(1), name="merge",
    )(ya, yb, pg, x, w_o, g)


def _head_indicator(d_a):
    h = np.zeros((d_a, LANE), np.float32)
    h[np.arange(d_a), np.arange(d_a) // HEAD_A] = 1.0
    return jnp.asarray(h, BF16), jnp.asarray(h.T, BF16)


def _rwkv_prep_kernel(pa_ref, prev_ref, mu_ref, wl_ref, w0_ref, a0_ref, kk_ref, ka_ref, hd_ref, hdt_ref,
                      r_ref, lw_ref, k2_ref, v_ref, kkn_ref, a_ref, g_ref, carry_ref, *, d_a, t_real):
    t = pl.program_id(1)
    tt = pa_ref.shape[1]

    @pl.when(t == 0)
    def _():
        carry_ref[...] = prev_ref[0]

    x = pa_ref[0]
    row = lax.broadcasted_iota(jnp.int32, x.shape, 0)
    xprev = jnp.where(row == 0, carry_ref[...], pltpu.roll(x, 1, 0))
    carry_ref[...] = pa_ref[0, pl.ds(tt - 1, 1), :]
    xs = x + (xprev - x) * mu_ref[...]
    r = xs[:, :d_a]
    k = xs[:, d_a:2 * d_a]
    v = xs[:, 2 * d_a:3 * d_a]
    lo = xs[:, 3 * d_a:]
    lane = lax.broadcasted_iota(jnp.int32, lo.shape, 1)
    act = jnp.where(lane < DECAY_LORA, jnp.tanh(lo),
                    jnp.where(lane < DECAY_LORA + A_LORA, lo, _sigmoid(lo)))
    lin = _dot(act.astype(BF16), wl_ref[...])
    z = -(w0_ref[...] + lin[:, :d_a])
    w = -(jnp.maximum(z, 0.0) + jnp.log(1.0 + jnp.exp(-jnp.abs(z)))) - 0.5
    logw = -jnp.exp(w)
    a = _sigmoid(a0_ref[...] + lin[:, d_a:2 * d_a])
    g = lin[:, 2 * d_a:]
    kkr = k * kk_ref[...]
    ss = _dot_x(kkr * kkr, hd_ref[...])
    inv = 1.0 / jnp.maximum(jnp.sqrt(ss), 1e-12)
    kkn = kkr * _dot_x(inv, hdt_ref[...])
    k2 = k * (1.0 + (a - 1.0) * ka_ref[...])
    if t_real is not None:
        live = (t * tt + lax.broadcasted_iota(jnp.int32, r.shape, 0)) < t_real
        zero = jnp.zeros_like(r)
        r, logw, k2, v, kkn, a = (jnp.where(live, u, zero) for u in (r, logw, k2, v, kkn, a))
    r_ref[0] = r
    lw_ref[0] = logw
    k2_ref[0] = k2
    v_ref[0] = v
    kkn_ref[0] = kkn
    a_ref[0] = a
    g_ref[0] = g


def _rwkv_prep(pa, prev, mu, wl, w0, a0, k_k, k_a, t_real):
    b, t, p = pa.shape
    d_a = w0.shape[1]
    tt = _tile(t, 256)
    hd, hdt = _head_indicator(d_a)
    full = lambda a: pl.BlockSpec(a.shape, lambda i, j: (0,) * a.ndim)
    out = jax.ShapeDtypeStruct((b, t, d_a), F32)
    ospec = pl.BlockSpec((1, tt, d_a), lambda i, j: (i, j, 0))
    return pl.pallas_call(
        functools.partial(_rwkv_prep_kernel, d_a=d_a, t_real=t_real), grid=(b, t // tt),
        in_specs=[pl.BlockSpec((1, tt, p), lambda i, j: (i, j, 0)), pl.BlockSpec((1, 1, p), lambda i, j: (i, 0, 0)),
                  full(mu), full(wl), full(w0), full(a0), full(k_k), full(k_a), full(hd), full(hdt)],
        out_specs=[ospec] * 7, out_shape=[out] * 7,
        scratch_shapes=[pltpu.VMEM((1, p), F32)], compiler_params=_cp(2), name="rwkv_prep",
    )(pa, prev, mu, wl, w0, a0, k_k, k_a, hd, hdt)


def _rwkv_scan_kernel(r_ref, lw_ref, k2_ref, v_ref, kk_ref, a_ref, s0_ref, o_ref, sT_ref, s_ref, *, c):
    ci = pl.program_id(2)
    hk = HEAD_A

    @pl.when(ci == 0)
    def _():
        s_ref[...] = s0_ref[0, 0]

    r = r_ref[0]
    lw = lw_ref[0]
    k2 = k2_ref[0]
    v = v_ref[0]
    kk = kk_ref[0]
    a = a_ref[0]
    lane = lax.broadcasted_iota(jnp.int32, (c, LANE), 1)
    m0 = lane < hk
    rowi = lax.broadcasted_iota(jnp.int32, (c, c), 0)
    coli = lax.broadcasted_iota(jnp.int32, (c, c), 1)
    ltri = (coli <= rowi).astype(BF16)
    cum = _dot_x_rhs(ltri, lw)
    e_w = jnp.exp(cum)
    e_wm = jnp.exp(cum - lw)
    e_wi = jnp.exp(-cum)
    at = -kk * e_wm
    bt = kk * a * e_wi
    kt = k2 * e_wi
    rt = r * e_w
    w_c = jnp.exp(cum[c - 1:c, :])

    def bd(z):
        zero = jnp.zeros_like(z)
        return jnp.concatenate([jnp.where(m0, z, zero), jnp.where(m0, zero, z)], axis=0)

    prow = lax.broadcasted_iota(jnp.int32, (c, 2 * c), 0)
    pcol = lax.broadcasted_iota(jnp.int32, (c, 2 * c), 1)
    pcol = jnp.where(pcol >= c, pcol - c, pcol)
    strict = pcol < prow
    incl = pcol <= prow
    x2 = jnp.concatenate([at, rt], axis=0)
    xb = _dot3_nt(x2, bd(bt))
    xk = _dot3_nt(x2, bd(kt))
    zero_cc = jnp.zeros((c, 2 * c), F32)
    a_ab = jnp.where(strict, xb[:c], zero_cc)
    a_rb = jnp.where(incl, xb[c:], zero_cc)
    a_ak = jnp.where(strict, xk[:c], zero_cc)
    a_rk = jnp.where(incl, xk[c:], zero_cc)

    if 2 * c == LANE:
        bdc = bd
    else:
        mc = lax.broadcasted_iota(jnp.int32, (c, 2 * c), 1) < c

        def bdc(z):
            zero = jnp.zeros_like(z)
            return jnp.concatenate([jnp.where(mc, z, zero), jnp.where(mc, zero, z)], axis=0)

    eye = (pcol == prow).astype(F32)
    tmat = eye + a_ab
    npow = a_ab
    steps = 1
    while 2 * steps < c:
        npow = _dot3(npow, bdc(npow))
        tmat = tmat + _dot3(tmat, bdc(npow))
        steps *= 2
    ta = _dot3(tmat, bd(at))
    xv = _dot3(a_ak, bd(v))
    tx = _dot3(tmat, bd(xv))
    p_c = rt + _dot3(a_rb, bd(ta))
    q_c = _dot3(a_rb, bd(tx)) + _dot3(a_rk, bd(v))
    lr = lax.broadcasted_iota(jnp.int32, (LANE, LANE), 0)
    lc = lax.broadcasted_iota(jnp.int32, (LANE, LANE), 1)
    same = (lr < hk) == (lc < hk)
    eye_l = (lr == lc).astype(F32)
    zero_l = jnp.zeros((LANE, LANE), F32)
    m_c = (eye_l + jnp.where(same, _dot3_tn(ta, bt), zero_l)) * w_c
    n_c = jnp.where(same, _dot3_tn(tx, bt) + _dot3_tn(v, kt), zero_l) * w_c
    s = s_ref[...]
    o_ref[0] = _dot3_nt(p_c, s) + q_c
    s_new = _dot3(s, m_c) + n_c
    s_ref[...] = s_new
    sT_ref[0, 0] = s_new


def _dot_x_rhs(a, b):
    hi, lo = _split(b)
    return _dot(a, hi) + _dot(a, lo)


def _rwkv_scan(r, lw, k2, v, kk, a, s0, c):
    b, t, d_a = r.shape
    npair = d_a // LANE
    seq = pl.BlockSpec((1, c, LANE), lambda i, p, j: (i, j, p))
    st = pl.BlockSpec((1, 1, LANE, LANE), lambda i, p, j: (i, p, 0, 0))
    return pl.pallas_call(
        functools.partial(_rwkv_scan_kernel, c=c), grid=(b, npair, t // c),
        in_specs=[seq] * 6 + [st], out_specs=[seq, st],
        out_shape=[jax.ShapeDtypeStruct((b, t, d_a), F32), jax.ShapeDtypeStruct((b, npair, LANE, LANE), F32)],
        scratch_shapes=[pltpu.VMEM((LANE, LANE), F32)], compiler_params=_cp(3), name="rwkv_scan",
    )(r, lw, k2, v, kk, a, s0)


def _rwkv_post_kernel(o_ref, r_ref, k2_ref, v_ref, g_ref, lw_ref, lb_ref, rk_ref, hd_ref, hdt_ref, w_ref, y_ref):
    o = o_ref[...]
    hd = hd_ref[...]
    hdt = hdt_ref[...]
    inv_n = 1.0 / HEAD_A
    mean = _dot_x(_dot_x(o, hd) * inv_n, hdt)
    d = o - mean
    var = _dot_x(d * d, hd) * inv_n
    xo = d * _dot_x(lax.rsqrt(var + LNX_EPS), hdt) * lw_ref[...] + lb_ref[...]
    bonus = _dot_x(_dot_x(r_ref[...] * k2_ref[...] * rk_ref[...], hd), hdt) * v_ref[...]
    y_ref[...] = _dot(((xo + bonus) * g_ref[...]).astype(BF16), w_ref[...])


def _rwkv_post(o, r, k2, v, g, lnx_w, lnx_b, r_k, w_oa):
    m, d_a = o.shape
    n = w_oa.shape[1]
    tm = _tile(m, 256)
    hd, hdt = _head_indicator(d_a)
    row = pl.BlockSpec((tm, d_a), lambda i: (i, 0))
    full = lambda a: pl.BlockSpec(a.shape, lambda i: (0,) * a.ndim)
    return pl.pallas_call(
        _rwkv_post_kernel, grid=(m // tm,),
        in_specs=[row] * 5 + [full(lnx_w), full(lnx_b), full(r_k), full(hd), full(hdt), full(w_oa)],
        out_specs=pl.BlockSpec((tm, n), lambda i: (i, 0)), out_shape=jax.ShapeDtypeStruct((m, n), F32),
        compiler_params=_cp(1), name="rwkv_post",
    )(o, r, k2, v, g, lnx_w, lnx_b, r_k, hd, hdt, w_oa)


def _t5_bucket_table():
    d = np.arange(LANE)
    max_exact = NUM_BUCKETS // 2
    df = np.maximum(d, 1).astype(np.float32)
    large = max_exact + (np.log(df / np.float32(max_exact)) / np.float32(math.log(MAX_DIST / max_exact))
                         * np.float32(NUM_BUCKETS - max_exact)).astype(np.int32)
    large = np.minimum(large, NUM_BUCKETS - 1)
    tab = np.where(d < max_exact, d, large)
    assert tab[FAR] == NUM_BUCKETS - 1
    return tab


def _page_gather_kernel(pt_ref, x_ref, o_ref):
    for s in range(o_ref.shape[2]):
        o_ref[0, 0, s] = x_ref[0, 0, :, s * LANE:(s + 1) * LANE]


def _page_gather(cache, page_table):
    nl, _, page, w = cache.shape
    b, n_pages = page_table.shape
    ns = w // LANE
    grid_spec = pltpu.PrefetchScalarGridSpec(
        num_scalar_prefetch=1, grid=(nl, b, n_pages),
        in_specs=[pl.BlockSpec((1, 1, page, w), lambda l, i, p, pt: (l, pt[i, p], 0, 0))],
        out_specs=pl.BlockSpec((1, 1, ns, page, LANE), lambda l, i, p, pt: (l, i, 0, p, 0)))
    return pl.pallas_call(
        _page_gather_kernel, grid_spec=grid_spec,
        out_shape=jax.ShapeDtypeStruct((nl, b, ns, n_pages * page, LANE), cache.dtype),
        compiler_params=_cp(3), name="page_gather",
    )(page_table, cache)


def _gelu_tanh(x):
    return 0.5 * x * (1.0 + jnp.tanh(math.sqrt(2.0 / math.pi) * (x + 0.044715 * (x * x * x))))


def _compress_kernel(ch_ref, ex_ref, w1_ref, w2_ref, o_ref):
    rr = ch_ref.shape[1]
    w1 = w1_ref[0]
    y = _dot(ch_ref[0].astype(BF16), w1)
    e = _dot(ex_ref[0].astype(BF16), w1)
    ybot = y[:, LANE:]
    nxt = pltpu.roll(ybot, rr - 1, 0)
    row = lax.broadcasted_iota(jnp.int32, nxt.shape, 0)
    nxt = jnp.where(row == rr - 1, e[0:1, LANE:], nxt)
    pre = y[:, :LANE] + nxt + (e[1:2, :LANE] + e[2:3, LANE:])
    o_ref[0] = _dot(_gelu_tanh(pre).astype(BF16), w2_ref[0])


def _compress(ch, ex, w1cat, w2):
    nb, rr, cw = ch.shape
    c_of = lambda i: (i // N_KV) % 2
    return pl.pallas_call(
        _compress_kernel, grid=(nb,),
        in_specs=[pl.BlockSpec((1, rr, cw), lambda i: (i, 0, 0)), pl.BlockSpec((1, SUBLANE, cw), lambda i: (i, 0, 0)),
                  pl.BlockSpec((1, cw, 2 * LANE), lambda i: (c_of(i), 0, 0)),
                  pl.BlockSpec((1, LANE, LANE), lambda i: (c_of(i), 0, 0))],
        out_specs=pl.BlockSpec((1, rr, LANE), lambda i: (i, 0, 0)),
        out_shape=jax.ShapeDtypeStruct((nb, rr, LANE), F32), compiler_params=_cp(1), name="nsa_compress",
    )(ch, ex, w1cat, w2)


def _bias_gather(tab_row, dist):
    idx = jnp.clip(dist, 0, FAR)
    return jnp.take_along_axis(jnp.broadcast_to(tab_row, idx.shape), idx, axis=1)


def _cmp_attn_kernel(q_ref, kc_ref, vc_ref, ovt_ref, tab_ref, o_ref, selt_ref, sc_ref,
                     *, tq, q0, ns, n_top, lanes):
    g = pl.program_id(1)
    qt = pl.program_id(2)
    hpg = q_ref.shape[2] // HEAD_B
    rr = kc_ref.shape[1]
    scale = HEAD_B ** -0.5
    kc = kc_ref[0].astype(BF16)
    vc = vc_ref[0].astype(BF16)
    qbase = q0 + qt * tq
    qrow = qbase + lax.broadcasted_iota(jnp.int32, (tq, LANE), 0)
    nchunk = rr // LANE
    dists = []
    for cix in range(nchunk):
        c_end = S_CMP * (cix * LANE + lax.broadcasted_iota(jnp.int32, (tq, LANE), 1)) + (L_CMP - 1)
        dists.append(qrow - c_end)
    mask = jnp.concatenate(dists, axis=1) >= 0
    psum = jnp.zeros((tq, rr), F32)
    outs = []
    for h in range(hpg):
        tab_row = tab_ref[pl.ds(g * hpg + h, 1), :]
        bias = jnp.concatenate([_bias_gather(tab_row, d) for d in dists], axis=1)
        s = _dot_nt(q_ref[0, :, h * HEAD_B:(h + 1) * HEAD_B], kc) * scale + bias
        s = jnp.where(mask, s, NEG)
        m = jnp.max(s, axis=1, keepdims=True)
        e = jnp.where(mask, jnp.exp(s - m), 0.0)
        p = e / jnp.maximum(jnp.sum(e, axis=1, keepdims=True), 1e-30)
        outs.append(_dot(p.astype(BF16), vc))
        psum = psum + p
    o_ref[0] = jnp.concatenate(outs, axis=1)
    if tq < lanes:
        psum = jnp.concatenate([psum, jnp.zeros((lanes - tq, rr), F32)], axis=0)
    ph, plo = _split(psum)
    ovt = ovt_ref[...]
    imp = _dot_nt(ovt, ph) + _dot_nt(ovt, plo)
    nsr = ovt.shape[0]
    j = lax.broadcasted_iota(jnp.int32, (nsr, lanes), 0)
    cur = (qbase + lax.broadcasted_iota(jnp.int32, (nsr, lanes), 1)) // L_SEL
    valid = j <= cur
    forced = (j == 0) | (j == cur) | (j == cur - 1)
    score = jnp.where(valid, imp + jnp.where(forced, FORCE_BONUS, 0.0), NEG)
    sc_ref[...] = score

    def body(jp, rank):
        rowv = sc_ref[pl.ds(jp, 1), :]
        before = (rowv > score) | ((rowv == score) & (jp < j))
        return rank + jnp.where(before, 1.0, 0.0)

    rank = lax.fori_loop(0, ns, body, jnp.zeros((nsr, lanes), F32))
    selt_ref[0, 0] = jnp.where(valid & (rank < n_top), 1.0, 0.0)


def _cmp_attn(q, kcv, tab, *, t_keys, q0, tq):
    b, tqa, d_b = q.shape
    rr = kcv.shape[1]
    gw = d_b // N_KV
    ns = -(-t_keys // L_SEL)
    nsr = _round_up(ns, SUBLANE)
    lanes = max(tq, LANE)
    nqt = tqa // tq
    ci = np.arange(rr)
    sj = np.arange(nsr)
    ov = ((S_CMP * ci) // L_SEL)[None, :] == sj[:, None]
    ov |= ((S_CMP * ci + L_CMP - 1) // L_SEL)[None, :] == sj[:, None]
    ovt = jnp.asarray(ov.astype(np.float32), BF16)
    kern = functools.partial(_cmp_attn_kernel, tq=tq, q0=q0, ns=ns, n_top=min(TOP_N, ns), lanes=lanes)
    return pl.pallas_call(
        kern, grid=(b, N_KV, nqt),
        in_specs=[pl.BlockSpec((1, tq, gw), lambda i, g, t: (i, t, g)),
                  pl.BlockSpec((1, rr, LANE), lambda i, g, t: (i * 4 + g, 0, 0)),
                  pl.BlockSpec((1, rr, LANE), lambda i, g, t: (i * 4 + N_KV + g, 0, 0)),
                  pl.BlockSpec(ovt.shape, lambda i, g, t: (0, 0)),
                  pl.BlockSpec(tab.shape, lambda i, g, t: (0, 0))],
        out_specs=[pl.BlockSpec((1, tq, gw), lambda i, g, t: (i, t, g)),
                   pl.BlockSpec((1, 1, nsr, lanes), lambda i, g, t: (i, g, 0, t))],
        out_shape=[jax.ShapeDtypeStruct((b, tqa, d_b), F32),
                   jax.ShapeDtypeStruct((b, N_KV, nsr, nqt * lanes), F32)],
        scratch_shapes=[pltpu.VMEM((nsr, lanes), F32)], compiler_params=_cp(3), name="nsa_cmp_attn",
    )(q, kcv, kcv, ovt, tab)


def _flash_kernel(*refs, mode, band, tq, tk, nj, q0, kbase, n_tail, tail_pos0):
    refs = list(refs)
    q_ref, k_ref, v_ref = refs[:3]
    pos = 3
    if n_tail:
        kt_ref, vt_ref = refs[pos:pos + 2]
        pos += 2
    if mode == "sel":
        sel_ref = refs[pos]
        pos += 1
    tab_ref, o_ref, m_ref, l_ref, acc_ref = refs[pos:pos + 5]
    g = pl.program_id(1)
    qt = pl.program_id(2)
    j = pl.program_id(3)
    hpg = q_ref.shape[1] // HEAD_B
    scale = HEAD_B ** -0.5
    qbase = q0 + qt * tq

    @pl.when(j == 0)
    def _():
        m_ref[...] = jnp.full_like(m_ref, NEG)
        l_ref[...] = jnp.zeros_like(l_ref)
        acc_ref[...] = jnp.zeros_like(acc_ref)

    def update(k_r, v_r, kpos0, width, n_valid, near):
        qrow = qbase + lax.broadcasted_iota(jnp.int32, (tq, width), 0)
        kcol = kpos0 + lax.broadcasted_iota(jnp.int32, (tq, width), 1)
        dist = qrow - kcol
        mask = dist >= 0
        if n_valid is not None:
            mask &= lax.broadcasted_iota(jnp.int32, (tq, width), 1) < n_valid
        if mode == "win":
            mask &= dist < WINDOW
        else:
            nsp = sel_ref.shape[1]
            blk = (kpos0 + lax.broadcasted_iota(jnp.int32, (nsp, width), 1)) // L_SEL
            expand = (blk == lax.broadcasted_iota(jnp.int32, (nsp, width), 0)).astype(BF16)
            mask &= _dot(sel_ref[...].astype(BF16), expand) > 0.5
        kb = k_r[...].astype(BF16)
        vb = v_r[...].astype(BF16)
        for h in range(hpg):
            tab_row = tab_ref[pl.ds(g * hpg + h, 1), :]
            if near:
                bias = jnp.concatenate([_bias_gather(tab_row, dist[:, c * LANE:(c + 1) * LANE])
                                        for c in range(width // LANE)], axis=1)
            else:
                bias = tab_row[:, FAR:FAR + 1]
            s = _dot_nt(q_ref[:, h * HEAD_B:(h + 1) * HEAD_B], kb) * scale + bias
            s = jnp.where(mask, s, NEG)
            m_prev = m_ref[h][:, :1]
            m_new = jnp.maximum(m_prev, jnp.max(s, axis=1, keepdims=True))
            alpha = jnp.exp(m_prev - m_new)
            p = jnp.where(mask, jnp.exp(s - m_new), 0.0)
            l_ref[h] = alpha * l_ref[h] + jnp.sum(p, axis=1, keepdims=True)
            acc_ref[h] = alpha * acc_ref[h] + _dot(p.astype(BF16), vb)
            m_ref[h] = jnp.broadcast_to(m_new, m_ref.shape[1:])

    if band:
        kt_abs = qt - (nj - 1) + j
        active = kt_abs >= 0
    else:
        kt_abs = j
        active = kbase + j * tk <= qbase + tq - 1
    kpos0 = kbase + kt_abs * tk
    is_far = qbase - (kpos0 + tk - 1) >= FAR

    @pl.when(active & is_far)
    def _():
        update(k_ref, v_ref, kpos0, tk, None, False)

    @pl.when(active & jnp.logical_not(is_far))
    def _():
        update(k_ref, v_ref, kpos0, tk, None, True)

    @pl.when(j == nj - 1)
    def _():
        if n_tail:
            update(kt_ref, vt_ref, tail_pos0, kt_ref.shape[0], n_tail, True)
        outs = [acc_ref[h] / jnp.maximum(l_ref[h][:, :1], 1e-30) for h in range(hpg)]
        o_ref[...] = jnp.concatenate(outs, axis=1)


def _flash(q, k_arr, k_spec, v_arr, v_spec, tab, *, mode, band, tq, tk, nj, q0, kbase,
           tail=None, sel=None, name):
    b, tqa, d_b = q.shape
    gw = d_b // N_KV
    hpg = gw // HEAD_B
    nqt = tqa // tq
    args = [q, k_arr, v_arr]
    in_specs = [pl.BlockSpec((None, tq, gw), lambda i, g, t, j: (i, t, g)), k_spec, v_spec]
    n_tail, tail_pos0 = 0, 0
    if tail is not None:
        tk_arr, tk_spec, tv_arr, tv_spec, n_tail, tail_pos0 = tail
        args += [tk_arr, tv_arr]
        in_specs += [tk_spec, tv_spec]
    if mode == "sel":
        args.append(sel)
        in_specs.append(pl.BlockSpec((None, None, tq, sel.shape[3]), lambda i, g, t, j: (i, g, t, 0)))
    args.append(tab)
    in_specs.append(pl.BlockSpec(tab.shape, lambda i, g, t, j: (0, 0)))
    kern = functools.partial(_flash_kernel, mode=mode, band=band, tq=tq, tk=tk, nj=nj, q0=q0, kbase=kbase,
                             n_tail=n_tail, tail_pos0=tail_pos0)
    return pl.pallas_call(
        kern, grid=(b, N_KV, nqt, nj), in_specs=in_specs,
        out_specs=pl.BlockSpec((None, tq, gw), lambda i, g, t, j: (i, t, g)),
        out_shape=jax.ShapeDtypeStruct((b, tqa, d_b), F32),
        scratch_shapes=[pltpu.VMEM((hpg, tq, LANE), F32), pltpu.VMEM((hpg, tq, LANE), F32),
                        pltpu.VMEM((hpg, tq, HEAD_B), F32)],
        compiler_params=_cp(4), name=name,
    )(*args)


def _nsa_out_kernel(oc_ref, os_ref, ow_ref, gl_ref, e_ref, w_ref, y_ref):
    gate = _sigmoid(gl_ref[...])
    o = (_dot_x(gate, e_ref[0]) * oc_ref[...] + _dot_x(gate, e_ref[1]) * os_ref[...]
         + _dot_x(gate, e_ref[2]) * ow_ref[...])
    y_ref[...] = _dot(o.astype(BF16), w_ref[...])


def _nsa_out(o_c, o_s, o_w, gl, w_ob):
    m, d_b = o_c.shape
    n = w_ob.shape[1]
    h_b = d_b // HEAD_B
    tm = _tile(m, 256)
    e = np.zeros((3, LANE, d_b), np.float32)
    for br in range(3):
        for hh in range(h_b):
            e[br, br * h_b + hh, hh * HEAD_B:(hh + 1) * HEAD_B] = 1.0
    e = jnp.asarray(e, BF16)
    row = pl.BlockSpec((tm, d_b), lambda i: (i, 0))
    return pl.pallas_call(
        _nsa_out_kernel, grid=(m // tm,),
        in_specs=[row, row, row, pl.BlockSpec((tm, LANE), lambda i: (i, 0)),
                  pl.BlockSpec(e.shape, lambda i: (0, 0, 0)), pl.BlockSpec(w_ob.shape, lambda i: (0, 0))],
        out_specs=pl.BlockSpec((tm, n), lambda i: (i, 0)), out_shape=jax.ShapeDtypeStruct((m, n), F32),
        compiler_params=_cp(1), name="nsa_out",
    )(o_c, o_s, o_w, gl, e, w_ob)


def _nsa(q, rows, win, gl, tab, lw, *, t_real, past8, cwin):
    b, tqa, d_b = q.shape
    t = t_real
    ncmp = 2 * N_KV
    cw = S_CMP * LANE
    p_len = 0 if past8 is None else past8.shape[2]
    l_tot = p_len + t
    assert p_len % (S_CMP * SUBLANE) == 0 and (p_len == 0 or t <= S_CMP) and (p_len > 0 or t % LANE == 0)
    new_cmp = rows[:, :, :ncmp * LANE].reshape(b, t, ncmp, LANE).transpose(0, 2, 1, 3)
    if past8 is None:
        ch = new_cmp.reshape(b * ncmp, t // S_CMP, cw)
        ex0 = jnp.zeros((b * ncmp, 1, cw), F32)
    else:
        ch = past8[:, :ncmp].reshape(b * ncmp, p_len // S_CMP, cw)
        ex0 = jnp.pad(new_cmp.reshape(b * ncmp, 1, t * LANE), ((0, 0), (0, 0), (0, cw - t * LANE)))
    pe = jnp.repeat(lw["cmp_pe"].reshape(2, 1, 2, cw), N_KV, axis=1)
    pe = jnp.broadcast_to(pe[None], (b, 2, N_KV, 2, cw)).reshape(b * ncmp, 2, cw)
    ex = jnp.concatenate([ex0, pe, jnp.zeros((b * ncmp, SUBLANE - 3, cw), F32)], axis=1)
    kcv = _compress(ch, ex, lw["w1cat"], lw["w2"])
    if kcv.shape[1] % LANE:
        kcv = jnp.pad(kcv, ((0, 0), (0, _round_up(kcv.shape[1], LANE) - kcv.shape[1]), (0, 0)))
    tq = tqa if p_len else LANE
    o_c, selt = _cmp_attn(q, kcv, tab, t_keys=l_tot, q0=p_len, tq=tq)
    ns = -(-l_tot // L_SEL)
    nsp = _round_up(ns, LANE)
    sel = jnp.swapaxes(selt, 2, 3)[:, :, :tqa]
    sel = jnp.pad(sel, ((0, 0), (0, 0), (0, 0), (0, nsp - sel.shape[3])))
    if past8 is None:
        tk = LANE
        ks = lambda col: pl.BlockSpec((None, tk, LANE), lambda i, g, qt, j, col=col: (i, jnp.minimum(j, qt), col + g))
        o_s = _flash(q, rows, ks(2 * N_KV), rows, ks(3 * N_KV), tab, mode="sel", band=False, tq=tq, tk=tk,
                     nj=t // tk, q0=0, kbase=0, sel=sel, name="nsa_sel_attn")
        nb = WINDOW // tk + 1
        kw = lambda col: pl.BlockSpec((None, tk, LANE),
                                      lambda i, g, qt, j, col=col: (i, jnp.maximum(qt - (nb - 1) + j, 0), col + g))
        o_w = _flash(q, win, kw(0), win, kw(N_KV), tab, mode="win", band=True, tq=tq, tk=tk,
                     nj=nb, q0=0, kbase=0, name="nsa_win_attn")
    else:
        tk = 512
        rows_t = jnp.pad(rows, ((0, 0), (0, LANE - t), (0, 0)))
        win_t = jnp.pad(win, ((0, 0), (0, LANE - t), (0, 0)))
        tails = lambda col: pl.BlockSpec((None, LANE, LANE), lambda i, g, qt, j, col=col: (i, 0, col + g))
        ks = lambda s0: pl.BlockSpec((None, None, tk, LANE), lambda i, g, qt, j, s0=s0: (i, s0 + g, j, 0))
        o_s = _flash(q, past8, ks(2 * N_KV), past8, ks(3 * N_KV), tab, mode="sel", band=False, tq=tq, tk=tk,
                     nj=p_len // tk, q0=p_len, kbase=0, sel=sel,
                     tail=(rows_t, tails(2 * N_KV), rows_t, tails(3 * N_KV), t, p_len), name="nsa_sel_attn_paged")
        wb = cwin.shape[1]
        kw = lambda col: pl.BlockSpec((None, wb, LANE), lambda i, g, qt, j, col=col: (i, 0, col + g))
        o_w = _flash(q, cwin, kw(0), cwin, kw(N_KV), tab, mode="win", band=False, tq=tq, tk=wb,
                     nj=1, q0=p_len, kbase=p_len - wb,
                     tail=(win_t, tails(0), win_t, tails(N_KV), t, p_len), name="nsa_win_attn_cached")
    m = b * tqa
    return _nsa_out(o_c.reshape(m, d_b), o_s.reshape(m, d_b), o_w.reshape(m, d_b), gl, lw["w_ob"])


def _layer_weights(l, w_in, mu, w0, w_up, a0, a_up, g_up, k_k, k_a, r_k, lnx_w, lnx_b, w_oa,
                   cmp_pe, cmp_w1, cmp_w2, w_ob, w_o, w_ff_up, w_ff_down, norm_g):
    d_a = w_oa.shape[1]
    d_b = w_ob.shape[1]
    d = w_o.shape[1]
    rw = 3 * d_a + DECAY_LORA + A_LORA + GATE_LORA
    pw = 3 * d_a + LORA_W
    nrow = 4 * N_KV * HEAD_B
    nwin = 2 * N_KV * HEAD_B
    ngl = 3 * (d_b // HEAD_B)
    wi = w_in[l]
    o_q = rw
    o_rows = o_q + d_b
    o_win = o_rows + nrow
    o_gl = o_win + nwin
    o_pg = o_gl + ngl
    wl = jnp.zeros((LORA_W, 3 * d_a), F32)
    wl = wl.at[:DECAY_LORA, :d_a].set(w_up[l])
    wl = wl.at[DECAY_LORA:DECAY_LORA + A_LORA, d_a:2 * d_a].set(a_up[l])
    wl = wl.at[DECAY_LORA + A_LORA:DECAY_LORA + A_LORA + GATE_LORA, 2 * d_a:].set(g_up[l])
    half = S_CMP * HEAD_B
    return dict(
        w_pa=jnp.pad(wi[:, :rw], ((0, 0), (0, pw - rw))).astype(BF16),
        w_q=wi[:, o_q:o_rows].astype(BF16),
        w_rows=wi[:, o_rows:o_win].astype(BF16),
        w_win=wi[:, o_win:o_gl].astype(BF16),
        w_gl=jnp.pad(wi[:, o_gl:o_pg], ((0, 0), (0, LANE - ngl))).astype(BF16),
        w_pg=wi[:, o_pg:o_pg + 2 * d].astype(BF16),
        mu=jnp.pad(mu[l], (0, pw - rw))[None], wl=wl.astype(BF16),
        w0=w0[l][None], a0=a0[l][None], k_k=k_k[l][None], k_a=k_a[l][None],
        r_k=r_k[l].reshape(1, d_a), lnx_w=lnx_w[l][None], lnx_b=lnx_b[l][None],
        w_oa=w_oa[l].astype(BF16), w_ob=w_ob[l].astype(BF16), w_o=w_o[l].astype(BF16),
        w_up=w_ff_up[l].astype(BF16), w_down=w_ff_down[l].astype(BF16),
        cmp_pe=cmp_pe[l],
        w1cat=jnp.concatenate([cmp_w1[l][:, :half], cmp_w1[l][:, half:]], axis=2).astype(BF16),
        w2=cmp_w2[l].astype(BF16), g=norm_g[l], rw=rw,
    )


def _layer(x, lw, tab, shift0, s0, past8, cwin):
    b, t, d = x.shape
    m = b * t
    d_a = lw["w_oa"].shape[0]
    x2 = x.reshape(m, d)
    g = lw["g"]
    xn = _rmsnorm_cast(x2, g[0:1])
    pa = _matmul(xn, lw["w_pa"], F32, name="proj_rwkv")
    q = _matmul(xn, lw["w_q"], BF16, name="proj_q")
    rows = _matmul(xn, lw["w_rows"], F32, name="proj_rows")
    win = _matmul(xn, lw["w_win"], F32, name="proj_win")
    gl = _matmul(xn, lw["w_gl"], F32, name="proj_gl")
    pg = _matmul(xn, lw["w_pg"], F32, name="proj_pg")
    pw = pa.shape[1]
    d_b = q.shape[1]
    tp = t if t % RWKV_CHUNK == 0 else _round_up(t, SUBLANE)
    c = RWKV_CHUNK if t % RWKV_CHUNK == 0 else tp
    pa3 = pa.reshape(b, t, pw)
    pa_p = pa3 if tp == t else jnp.pad(pa3, ((0, 0), (0, tp - t), (0, 0)))
    prev = jnp.pad(shift0, ((0, 0), (0, pw - shift0.shape[1])))[:, None]
    r, lgw, k2, v, kk, a, gg = _rwkv_prep(pa_p, prev, lw["mu"], lw["wl"], lw["w0"], lw["a0"], lw["k_k"], lw["k_a"],
                                          None if tp == t else t)
    o, s_fin = _rwkv_scan(r, lgw, k2, v, kk, a, s0, c)
    flat = lambda u: u.reshape(b * tp, d_a)
    ya = _rwkv_post(flat(o), flat(r), flat(k2), flat(v), flat(gg), lw["lnx_w"], lw["lnx_b"], lw["r_k"], lw["w_oa"])
    if tp != t:
        ya = ya.reshape(b, tp, d)[:, :t].reshape(m, d)
    sh = pa3[:, t - 1, :lw["rw"]]
    tqa = t if t % LANE == 0 else _round_up(t, SUBLANE)
    q3 = q.reshape(b, t, d_b)
    gl_p = gl
    if tqa != t:
        q3 = jnp.pad(q3, ((0, 0), (0, tqa - t), (0, 0)))
        gl_p = jnp.pad(gl.reshape(b, t, LANE), ((0, 0), (0, tqa - t), (0, 0))).reshape(b * tqa, LANE)
    rows3 = rows.reshape(b, t, rows.shape[1])
    win3 = win.reshape(b, t, win.shape[1])
    yb = _nsa(q3, rows3, win3, gl_p, tab, lw, t_real=t, past8=past8, cwin=cwin)
    if tqa != t:
        yb = yb.reshape(b, tqa, d)[:, :t].reshape(m, d)
    x1 = _merge(ya, yb, pg, x2, lw["w_o"], g[1:2])
    h = _matmul(_rmsnorm_cast(x1, g[2:3]), lw["w_up"], BF16, relu2=True, name="ffn_up")
    x_out = _ffn_down(h, lw["w_down"], x1, g[3:4])
    wctx = win3 if cwin is None else jnp.concatenate([cwin, win3], axis=1)
    n_keep = min(WINDOW, wctx.shape[1])
    return x_out.reshape(b, t, d), rows3, wctx[:, wctx.shape[1] - n_keep:], sh, s_fin


def _pair_states(s):
    b, h, n, _ = s.shape
    s = s.reshape(b, h // 2, 2, n, n)
    z = jnp.zeros((b, h // 2, n, n), s.dtype)
    top = jnp.concatenate([s[:, :, 0], z], axis=3)
    bot = jnp.concatenate([z, s[:, :, 1]], axis=3)
    return jnp.concatenate([top, bot], axis=2)


def _unpair_states(s):
    n = HEAD_A
    b, hp = s.shape[:2]
    return jnp.stack([s[:, :, :n, :n], s[:, :, n:, n:]], axis=2).reshape(b, 2 * hp, n, n)


def _trunk(x, shift0, wkv0, past8, cache_win, tab, layers):
    rows, wins, shifts, wkvs = [], [], [], []
    for l, lw in enumerate(layers):
        x, nr, nw, sh, st = _layer(x, lw, tab, shift0[l], _pair_states(wkv0[l]),
                                   None if past8 is None else past8[l],
                                   None if cache_win is None else cache_win[l])
        b, t = nr.shape[:2]
        rows.append(nr.reshape(b, t, 4, N_KV, HEAD_B))
        wins.append(nw.reshape(b, nw.shape[1], 2, N_KV, HEAD_B))
        shifts.append(sh)
        wkvs.append(_unpair_states(st))
    return x, jnp.stack(rows), jnp.stack(wins), jnp.stack(shifts), jnp.stack(wkvs)


def kernel(x_prompt, x_sample, cache_kv, cache_win, state_shift, state_wkv, page_table, w_in, mu, w0, w_up, a0, a_up, g_up, k_k, k_a, r_k, lnx_w, lnx_b, w_oa, cmp_pe, cmp_w1, cmp_w2, w_ob, w_o, w_ff_up, w_ff_down, norm_g, rel_bias):
    depth = w_in.shape[0]
    layers = [_layer_weights(l, w_in, mu, w0, w_up, a0, a_up, g_up, k_k, k_a, r_k, lnx_w, lnx_b, w_oa,
                             cmp_pe, cmp_w1, cmp_w2, w_ob, w_o, w_ff_up, w_ff_down, norm_g) for l in range(depth)]
    tab = rel_bias[_t5_bucket_table()].T
    bp = x_prompt.shape[0]
    rw = state_shift.shape[2]
    zeros_shift = jnp.zeros((depth, bp, rw), x_prompt.dtype)
    zeros_wkv = jnp.zeros((depth, bp) + state_wkv.shape[2:], state_wkv.dtype)
    y_p, kv_p, win_p, sh_p, wkv_p = _trunk(x_prompt, zeros_shift, zeros_wkv, None, None, tab, layers)
    nl, n_pool, page = cache_kv.shape[:3]
    past8 = _page_gather(cache_kv.reshape(nl, n_pool, page, -1), page_table)
    cwin = cache_win.reshape(cache_win.shape[:3] + (-1,))
    y_s, kv_s, win_s, sh_s, wkv_s = _trunk(x_sample, state_shift, state_wkv, past8, cwin, tab, layers)
    return (y_p, y_s, kv_p, kv_s, win_p, win_s, sh_p, sh_s, wkv_p, wkv_s)
```

```python
import functools
import math

import numpy as np
import jax
import jax.numpy as jnp
from jax import lax
from jax.experimental import pallas as pl
from jax.experimental.pallas import tpu as pltpu

F32 = jnp.float32
BF16 = jnp.bfloat16

HEAD_A = 64
DECAY_LORA = 64
A_LORA = 64
GATE_LORA = 160
LNX_EPS = 64e-5
N_KV = 2
HEAD_B = 128
L_CMP = 32
S_CMP = 16
L_SEL = 64
TOP_N = 16
WINDOW = 512
FORCE_BONUS = 1e4
NEG = -1e30
NUM_BUCKETS = 32
MAX_DIST = 128
EPS = 1e-6
PAGE_SIZE = 128

LANE = 128
SUBLANE = 8
VMEM_LIMIT = 56 * 1024 * 1024
RWKV_CHUNK = 64
LORA_W = 384
FAR = LANE - 1


def _cp(n_axes):
    return pltpu.CompilerParams(dimension_semantics=("arbitrary",) * n_axes,
                                vmem_limit_bytes=VMEM_LIMIT)


def _round_up(x, m):
    return -(-x // m) * m


def _tile(n, pref, mult=SUBLANE):
    if n <= pref:
        return n
    for t in range(pref - pref % mult, 0, -mult):
        if n % t == 0:
            return t
    return n


def _dot(a, b):
    return jnp.dot(a, b, preferred_element_type=F32)


def _dot_nt(a, b):
    return lax.dot_general(a, b, (((1,), (1,)), ((), ())), preferred_element_type=F32)


def _dot_tn(a, b):
    return lax.dot_general(a, b, (((0,), (0,)), ((), ())), preferred_element_type=F32)


def _split(x):
    hi = x.astype(BF16)
    lo = (x - hi.astype(F32)).astype(BF16)
    return hi, lo


def _dot_x(a, b):
    hi, lo = _split(a)
    return _dot(hi, b) + _dot(lo, b)


def _dot3(a, b):
    ah, al = _split(a)
    bh, bl = _split(b)
    return _dot(ah, bh) + _dot(ah, bl) + _dot(al, bh)


def _dot3_nt(a, b):
    ah, al = _split(a)
    bh, bl = _split(b)
    return _dot_nt(ah, bh) + _dot_nt(ah, bl) + _dot_nt(al, bh)


def _dot3_tn(a, b):
    ah, al = _split(a)
    bh, bl = _split(b)
    return _dot_tn(ah, bh) + _dot_tn(ah, bl) + _dot_tn(al, bh)


def _sigmoid(x):
    return 1.0 / (1.0 + jnp.exp(-x))


def _rms(y, g):
    return y * lax.rsqrt(jnp.mean(y * y, axis=-1, keepdims=True) + EPS) * g


def _rmsnorm_kernel(x_ref, g_ref, o_ref):
    o_ref[...] = _rms(x_ref[...], g_ref[...]).astype(o_ref.dtype)


def _rmsnorm_cast(x, g):
    m, d = x.shape
    tm = _tile(m, 256)
    return pl.pallas_call(
        _rmsnorm_kernel, grid=(m // tm,),
        in_specs=[pl.BlockSpec((tm, d), lambda i: (i, 0)), pl.BlockSpec((1, d), lambda i: (0, 0))],
        out_specs=pl.BlockSpec((tm, d), lambda i: (i, 0)),
        out_shape=jax.ShapeDtypeStruct((m, d), BF16), compiler_params=_cp(1), name="rmsnorm_cast",
    )(x, g)


def _mm_kernel(a_ref, w_ref, o_ref, *, relu2):
    y = _dot(a_ref[...], w_ref[...])
    if relu2:
        y = jnp.square(jnp.maximum(y, 0.0))
    o_ref[...] = y.astype(o_ref.dtype)


def _matmul(a, w, out_dtype, relu2=False, name="matmul"):
    m, k = a.shape
    n = w.shape[1]
    tm = _tile(m, 1024)
    tn = _tile(n, 1280, LANE)
    return pl.pallas_call(
        functools.partial(_mm_kernel, relu2=relu2), grid=(m // tm, n // tn),
        in_specs=[pl.BlockSpec((tm, k), lambda i, j: (i, 0)), pl.BlockSpec((k, tn), lambda i, j: (0, j))],
        out_specs=pl.BlockSpec((tm, tn), lambda i, j: (i, j)),
        out_shape=jax.ShapeDtypeStruct((m, n), out_dtype), compiler_params=_cp(2), name=name,
    )(a, w)


def _ffn_down_kernel(h_ref, w_ref, x_ref, g_ref, o_ref, acc_ref):
    k = pl.program_id(1)

    @pl.when(k == 0)
    def _():
        acc_ref[...] = jnp.zeros_like(acc_ref)

    acc_ref[...] += _dot(h_ref[...], w_ref[...])

    @pl.when(k == pl.num_programs(1) - 1)
    def _():
        o_ref[...] = x_ref[...] + _rms(acc_ref[...], g_ref[...])


def _ffn_down(h, w, x, g):
    m, kdim = h.shape
    n = w.shape[1]
    tm = _tile(m, 512)
    tk = _tile(kdim, 1024, LANE)
    return pl.pallas_call(
        _ffn_down_kernel, grid=(m // tm, kdim // tk),
        in_specs=[pl.BlockSpec((tm, tk), lambda i, k: (i, k)), pl.BlockSpec((tk, n), lambda i, k: (k, 0)),
                  pl.BlockSpec((tm, n), lambda i, k: (i, 0)), pl.BlockSpec((1, n), lambda i, k: (0, 0))],
        out_specs=pl.BlockSpec((tm, n), lambda i, k: (i, 0)),
        out_shape=jax.ShapeDtypeStruct((m, n), F32),
        scratch_shapes=[pltpu.VMEM((tm, n), F32)], compiler_params=_cp(2), name="ffn_down",
    )(h, w, x, g)


def _merge_kernel(ya_ref, yb_ref, pg_ref, x_ref, w_ref, g_ref, o_ref):
    d = ya_ref.shape[1]
    pg = pg_ref[...]
    mix = _sigmoid(pg[:, :d]) * ya_ref[...] + _sigmoid(pg[:, d:]) * yb_ref[...]
    y = _dot(mix.astype(BF16), w_ref[...])
    o_ref[...] = x_ref[...] + _rms(y, g_ref[...])


def _merge(ya, yb, pg, x, w_o, g):
    m, d = x.shape
    tm = _tile(m, 256)
    row = lambda c: pl.BlockSpec((tm, c), lambda i: (i, 0))
    return pl.pallas_call(
        _merge_kernel, grid=(m // tm,),
        in_specs=[row(d), row(d), row(2 * d), row(d),
                  pl.BlockSpec((d, d), lambda i: (0, 0)), pl.BlockSpec((1, d), lambda i: (0, 0))],
        out_specs=row(d), out_shape=jax.ShapeDtypeStruct((m, d), F32),
        compiler_params=_cp(1), name="merge",
    )(ya, yb, pg, x, w_o, g)


def _head_indicator(d_a):
    h = np.zeros((d_a, LANE), np.float32)
    h[np.arange(d_a), np.arange(d_a) // HEAD_A] = 1.0
    return jnp.asarray(h, BF16), jnp.asarray(h.T, BF16)


def _rwkv_prep_kernel(pa_ref, prev_ref, mu_ref, wl_ref, w0_ref, a0_ref, kk_ref, ka_ref, hd_ref, hdt_ref,
                      r_ref, lw_ref, k2_ref, v_ref, kkn_ref, a_ref, g_ref, carry_ref, *, d_a, t_real):
    t = pl.program_id(1)
    tt = pa_ref.shape[1]

    @pl.when(t == 0)
    def _():
        carry_ref[...] = prev_ref[0]

    x = pa_ref[0]
    row = lax.broadcasted_iota(jnp.int32, x.shape, 0)
    xprev = jnp.where(row == 0, carry_ref[...], pltpu.roll(x, 1, 0))
    carry_ref[...] = pa_ref[0, pl.ds(tt - 1, 1), :]
    xs = x + (xprev - x) * mu_ref[...]
    r = xs[:, :d_a]
    k = xs[:, d_a:2 * d_a]
    v = xs[:, 2 * d_a:3 * d_a]
    lo = xs[:, 3 * d_a:]
    lane = lax.broadcasted_iota(jnp.int32, lo.shape, 1)
    act = jnp.where(lane < DECAY_LORA, jnp.tanh(lo),
                    jnp.where(lane < DECAY_LORA + A_LORA, lo, _sigmoid(lo)))
    lin = _dot(act.astype(BF16), wl_ref[...])
    z = -(w0_ref[...] + lin[:, :d_a])
    w = -(jnp.maximum(z, 0.0) + jnp.log(1.0 + jnp.exp(-jnp.abs(z)))) - 0.5
    logw = -jnp.exp(w)
    a = _sigmoid(a0_ref[...] + lin[:, d_a:2 * d_a])
    g = lin[:, 2 * d_a:]
    kkr = k * kk_ref[...]
    ss = _dot_x(kkr * kkr, hd_ref[...])
    inv = 1.0 / jnp.maximum(jnp.sqrt(ss), 1e-12)
    kkn = kkr * _dot_x(inv, hdt_ref[...])
    k2 = k * (1.0 + (a - 1.0) * ka_ref[...])
    if t_real is not None:
        live = (t * tt + lax.broadcasted_iota(jnp.int32, r.shape, 0)) < t_real
        zero = jnp.zeros_like(r)
        r, logw, k2, v, kkn, a = (jnp.where(live, u, zero) for u in (r, logw, k2, v, kkn, a))
    r_ref[0] = r
    lw_ref[0] = logw
    k2_ref[0] = k2
    v_ref[0] = v
    kkn_ref[0] = kkn
    a_ref[0] = a
    g_ref[0] = g


def _rwkv_prep(pa, prev, mu, wl, w0, a0, k_k, k_a, t_real):
    b, t, p = pa.shape
    d_a = w0.shape[1]
    tt = _tile(t, 256)
    hd, hdt = _head_indicator(d_a)
    full = lambda a: pl.BlockSpec(a.shape, lambda i, j: (0,) * a.ndim)
    out = jax.ShapeDtypeStruct((b, t, d_a), F32)
    ospec = pl.BlockSpec((1, tt, d_a), lambda i, j: (i, j, 0))
    return pl.pallas_call(
        functools.partial(_rwkv_prep_kernel, d_a=d_a, t_real=t_real), grid=(b, t // tt),
        in_specs=[pl.BlockSpec((1, tt, p), lambda i, j: (i, j, 0)), pl.BlockSpec((1, 1, p), lambda i, j: (i, 0, 0)),
                  full(mu), full(wl), full(w0), full(a0), full(k_k), full(k_a), full(hd), full(hdt)],
        out_specs=[ospec] * 7, out_shape=[out] * 7,
        scratch_shapes=[pltpu.VMEM((1, p), F32)], compiler_params=_cp(2), name="rwkv_prep",
    )(pa, prev, mu, wl, w0, a0, k_k, k_a, hd, hdt)


def _mmb(a, b, form="nn"):
    dot = {"nn": _dot, "nt": _dot_nt, "tn": _dot_tn}[form]
    return dot(a.astype(BF16), b.astype(BF16))


def _rwkv_scan_kernel(r_ref, lw_ref, k2_ref, v_ref, kk_ref, a_ref, s0_ref, o_ref, sT_ref, s_ref, *, c):
    @pl.when(pl.program_id(1) == 0)
    def _():
        s_ref[...] = s0_ref[0]

    hk = HEAD_A
    pairs = range(s_ref.shape[0])
    sls = [slice(p * LANE, (p + 1) * LANE) for p in pairs]
    lane = lax.broadcasted_iota(jnp.int32, (c, LANE), 1)
    m0 = lane < hk
    rowi = lax.broadcasted_iota(jnp.int32, (c, c), 0)
    coli = lax.broadcasted_iota(jnp.int32, (c, c), 1)
    ltri = (coli <= rowi).astype(BF16)
    lw = [lw_ref[0, :, sl] for sl in sls]
    cum = [_dot_x_rhs(ltri, x) for x in lw]
    e_wi = [jnp.exp(-x) for x in cum]
    at = [-kk_ref[0, :, sl] * jnp.exp(x - y) for sl, x, y in zip(sls, cum, lw)]
    bt = [kk_ref[0, :, sl] * a_ref[0, :, sl] * e for sl, e in zip(sls, e_wi)]
    kt = [k2_ref[0, :, sl] * e for sl, e in zip(sls, e_wi)]
    rt = [r_ref[0, :, sl] * jnp.exp(x) for sl, x in zip(sls, cum)]
    v = [v_ref[0, :, sl] for sl in sls]
    w_c = [jnp.exp(x[c - 1:c, :]) for x in cum]

    def bd(z):
        zero = jnp.zeros_like(z)
        return jnp.concatenate([jnp.where(m0, z, zero), jnp.where(m0, zero, z)], axis=0)

    prow = lax.broadcasted_iota(jnp.int32, (c, 2 * c), 0)
    pcol = lax.broadcasted_iota(jnp.int32, (c, 2 * c), 1)
    pcol = jnp.where(pcol >= c, pcol - c, pcol)
    strict = pcol < prow
    incl = pcol <= prow
    zero_cc = jnp.zeros((c, 2 * c), F32)
    x2 = [jnp.concatenate([x, y], axis=0) for x, y in zip(at, rt)]
    xb = [_mmb(x, bd(y), "nt") for x, y in zip(x2, bt)]
    xk = [_mmb(x, bd(y), "nt") for x, y in zip(x2, kt)]
    a_ab = [jnp.where(strict, x[:c], zero_cc) for x in xb]
    a_rb = [jnp.where(incl, x[c:], zero_cc) for x in xb]
    a_ak = [jnp.where(strict, x[:c], zero_cc) for x in xk]
    a_rk = [jnp.where(incl, x[c:], zero_cc) for x in xk]

    if 2 * c == LANE:
        bdc = bd
    else:
        mc = lax.broadcasted_iota(jnp.int32, (c, 2 * c), 1) < c

        def bdc(z):
            zero = jnp.zeros_like(z)
            return jnp.concatenate([jnp.where(mc, z, zero), jnp.where(mc, zero, z)], axis=0)

    eye = (pcol == prow).astype(F32)
    tmat = [eye + x for x in a_ab]
    npow = a_ab
    steps = 1
    while 2 * steps < c:
        npow = [_dot3(x, bdc(x)) for x in npow]
        tmat = [x + _dot3(x, bdc(y)) for x, y in zip(tmat, npow)]
        steps *= 2
    ta = [_mmb(x, bd(y)) for x, y in zip(tmat, at)]
    xv = [_mmb(x, bd(y)) for x, y in zip(a_ak, v)]
    tx = [_mmb(x, bd(y)) for x, y in zip(tmat, xv)]
    p_c = [x + _mmb(y, bd(z)) for x, y, z in zip(rt, a_rb, ta)]
    q_c = [_mmb(x, bd(y)) + _mmb(z, bd(u)) for x, y, z, u in zip(a_rb, tx, a_rk, v)]
    lr = lax.broadcasted_iota(jnp.int32, (LANE, LANE), 0)
    lc = lax.broadcasted_iota(jnp.int32, (LANE, LANE), 1)
    same = (lr < hk) == (lc < hk)
    eye_l = (lr == lc).astype(F32)
    zero_l = jnp.zeros((LANE, LANE), F32)
    m_c = [(eye_l + jnp.where(same, _mmb(x, y, "tn"), zero_l)) * w for x, y, w in zip(ta, bt, w_c)]
    n_c = [jnp.where(same, _mmb(x, y, "tn") + _mmb(z, u, "tn"), zero_l) * w
           for x, y, z, u, w in zip(tx, bt, v, kt, w_c)]
    s = [s_ref[p] for p in pairs]
    o = [_mmb(x, y, "nt") + z for x, y, z in zip(p_c, s, q_c)]
    s_new = [_mmb(x, y) + z for x, y, z in zip(s, m_c, n_c)]
    for p in pairs:
        o_ref[0, :, sls[p]] = o[p]
        s_ref[p] = s_new[p]
        sT_ref[0, p] = s_new[p]


def _dot_x_rhs(a, b):
    hi, lo = _split(b)
    return _dot(a, hi) + _dot(a, lo)


def _rwkv_scan(r, lw, k2, v, kk, a, s0, c):
    b, t, d_a = r.shape
    npair = d_a // LANE
    seq = pl.BlockSpec((1, c, d_a), lambda i, j: (i, j, 0))
    st = pl.BlockSpec((1, npair, LANE, LANE), lambda i, j: (i, 0, 0, 0))
    return pl.pallas_call(
        functools.partial(_rwkv_scan_kernel, c=c), grid=(b, t // c),
        in_specs=[seq] * 6 + [st], out_specs=[seq, st],
        out_shape=[jax.ShapeDtypeStruct((b, t, d_a), F32), jax.ShapeDtypeStruct((b, npair, LANE, LANE), F32)],
        scratch_shapes=[pltpu.VMEM((npair, LANE, LANE), F32)], compiler_params=_cp(2), name="rwkv_scan",
    )(r, lw, k2, v, kk, a, s0)


def _rwkv_post_kernel(o_ref, r_ref, k2_ref, v_ref, g_ref, lw_ref, lb_ref, rk_ref, hd_ref, hdt_ref, w_ref, y_ref):
    o = o_ref[...]
    hd = hd_ref[...]
    hdt = hdt_ref[...]
    inv_n = 1.0 / HEAD_A
    mean = _dot_x(_dot_x(o, hd) * inv_n, hdt)
    d = o - mean
    var = _dot_x(d * d, hd) * inv_n
    xo = d * _dot_x(lax.rsqrt(var + LNX_EPS), hdt) * lw_ref[...] + lb_ref[...]
    bonus = _dot_x(_dot_x(r_ref[...] * k2_ref[...] * rk_ref[...], hd), hdt) * v_ref[...]
    y_ref[...] = _dot(((xo + bonus) * g_ref[...]).astype(BF16), w_ref[...])


def _rwkv_post(o, r, k2, v, g, lnx_w, lnx_b, r_k, w_oa):
    m, d_a = o.shape
    n = w_oa.shape[1]
    tm = _tile(m, 256)
    hd, hdt = _head_indicator(d_a)
    row = pl.BlockSpec((tm, d_a), lambda i: (i, 0))
    full = lambda a: pl.BlockSpec(a.shape, lambda i: (0,) * a.ndim)
    return pl.pallas_call(
        _rwkv_post_kernel, grid=(m // tm,),
        in_specs=[row] * 5 + [full(lnx_w), full(lnx_b), full(r_k), full(hd), full(hdt), full(w_oa)],
        out_specs=pl.BlockSpec((tm, n), lambda i: (i, 0)), out_shape=jax.ShapeDtypeStruct((m, n), F32),
        compiler_params=_cp(1), name="rwkv_post",
    )(o, r, k2, v, g, lnx_w, lnx_b, r_k, hd, hdt, w_oa)


def _t5_bucket_table():
    d = np.arange(LANE)
    max_exact = NUM_BUCKETS // 2
    df = np.maximum(d, 1).astype(np.float32)
    large = max_exact + (np.log(df / np.float32(max_exact)) / np.float32(math.log(MAX_DIST / max_exact))
                         * np.float32(NUM_BUCKETS - max_exact)).astype(np.int32)
    large = np.minimum(large, NUM_BUCKETS - 1)
    tab = np.where(d < max_exact, d, large)
    assert tab[FAR] == NUM_BUCKETS - 1
    return tab


def _page_gather_kernel(pt_ref, *refs, ns, page):
    o_ref = refs[-1]
    for k, x_ref in enumerate(refs[:-1]):
        for s in range(ns):
            o_ref[0, 0, s, k * page:(k + 1) * page, :] = x_ref[0, 0, pl.ds(s, page, stride=ns), :]


def _page_gather(cache, page_table, ns, page):
    nl = cache.shape[0]
    b, n_pages = page_table.shape
    npp = next(k for k in (4, 2, 1) if n_pages % k == 0)
    in_spec = lambda k: pl.BlockSpec((1, 1, page * ns, LANE), lambda l, i, p, pt: (l, pt[i, p * npp + k], 0, 0))
    grid_spec = pltpu.PrefetchScalarGridSpec(
        num_scalar_prefetch=1, grid=(nl, b, n_pages // npp),
        in_specs=[in_spec(k) for k in range(npp)],
        out_specs=pl.BlockSpec((1, 1, ns, npp * page, LANE), lambda l, i, p, pt: (l, i, 0, p, 0)))
    return pl.pallas_call(
        functools.partial(_page_gather_kernel, ns=ns, page=page), grid_spec=grid_spec,
        out_shape=jax.ShapeDtypeStruct((nl, b, ns, n_pages * page, LANE), cache.dtype),
        compiler_params=_cp(3), name="page_gather",
    )(page_table, *([cache] * npp))


def _gelu_tanh(x):
    return 0.5 * x * (1.0 + jnp.tanh(math.sqrt(2.0 / math.pi) * (x + 0.044715 * (x * x * x))))


def _compress_kernel(x_ref, ex_ref, w1_ref, w2_ref, o_ref):
    rr = x_ref.shape[0] // S_CMP
    w1 = w1_ref[0]
    y = jnp.zeros((rr, 2 * LANE), F32)
    for j in range(S_CMP):
        xj = x_ref[pl.ds(j, rr, stride=S_CMP), :].astype(BF16)
        y = y + _dot(xj, w1[j * LANE:(j + 1) * LANE, :])
    e = _dot(ex_ref[0].astype(BF16), w1)
    ybot = y[:, LANE:]
    nxt = pltpu.roll(ybot, rr - 1, 0)
    row = lax.broadcasted_iota(jnp.int32, nxt.shape, 0)
    nxt = jnp.where(row == rr - 1, e[0:1, LANE:], nxt)
    pre = y[:, :LANE] + nxt + (e[1:2, :LANE] + e[2:3, LANE:])
    o_ref[0] = _dot(_gelu_tanh(pre).astype(BF16), w2_ref[0])


def _compress(x, x_spec, n_pos, ex, w1cat, w2):
    nb, _, cw = ex.shape
    rr = n_pos // S_CMP
    c_of = lambda i: (i // N_KV) % 2
    return pl.pallas_call(
        _compress_kernel, grid=(nb,),
        in_specs=[x_spec, pl.BlockSpec((1, SUBLANE, cw), lambda i: (i, 0, 0)),
                  pl.BlockSpec((1, cw, 2 * LANE), lambda i: (c_of(i), 0, 0)),
                  pl.BlockSpec((1, LANE, LANE), lambda i: (c_of(i), 0, 0))],
        out_specs=pl.BlockSpec((1, rr, LANE), lambda i: (i, 0, 0)),
        out_shape=jax.ShapeDtypeStruct((nb, rr, LANE), F32), compiler_params=_cp(1), name="nsa_compress",
    )(x, ex, w1cat, w2)


def _bias_gather(tab_row, dist):
    idx = jnp.clip(dist, 0, FAR)
    return jnp.take_along_axis(jnp.broadcast_to(tab_row, idx.shape), idx, axis=1)


def _online_softmax_update(q_ref, kb, vb, mask, dist, g, tab_ref, m_ref, l_ref, acc_ref):
    hpg = q_ref.shape[1] // HEAD_B
    heads = range(hpg)
    scale = HEAD_B ** -0.5
    tabs = [tab_ref[pl.ds(g * hpg + h, 1), :] for h in heads]
    if dist is None:
        bias = [t[:, FAR:FAR + 1] for t in tabs]
    else:
        bias = [jnp.concatenate([_bias_gather(t, dist[:, c * LANE:(c + 1) * LANE])
                                 for c in range(dist.shape[1] // LANE)], axis=1) for t in tabs]
    s = [_dot_nt(q_ref[:, h * HEAD_B:(h + 1) * HEAD_B], kb) * scale + bias[h] for h in heads]
    s = [jnp.where(mask, x, NEG) for x in s]
    m_prev = [m_ref[h][:, :1] for h in heads]
    m_new = [jnp.maximum(mp, jnp.max(x, axis=1, keepdims=True)) for mp, x in zip(m_prev, s)]
    alpha = [jnp.exp(mp - mn) for mp, mn in zip(m_prev, m_new)]
    p = [jnp.where(mask, jnp.exp(x - mn), 0.0) for x, mn in zip(s, m_new)]
    pv = [_dot(x.astype(BF16), vb) for x in p]
    for h in heads:
        l_ref[h] = alpha[h] * l_ref[h] + jnp.sum(p[h], axis=1, keepdims=True)
        acc_ref[h] = alpha[h] * acc_ref[h] + pv[h]
        m_ref[h] = jnp.broadcast_to(m_new[h], m_ref.shape[1:])


def _cmp_attn_kernel(q_ref, kc_ref, vc_ref, ovt_ref, tab_ref, o_ref, selt_ref, sc_ref,
                     *, tq, q0, ns, n_top, lanes):
    g = pl.program_id(1)
    qt = pl.program_id(2)
    hpg = q_ref.shape[2] // HEAD_B
    rr = kc_ref.shape[1]
    scale = HEAD_B ** -0.5
    kc = kc_ref[0].astype(BF16)
    vc = vc_ref[0].astype(BF16)
    qbase = q0 + qt * tq
    qrow = qbase + lax.broadcasted_iota(jnp.int32, (tq, LANE), 0)
    nchunk = rr // LANE
    dists = []
    for cix in range(nchunk):
        c_end = S_CMP * (cix * LANE + lax.broadcasted_iota(jnp.int32, (tq, LANE), 1)) + (L_CMP - 1)
        dists.append(qrow - c_end)
    mask = jnp.concatenate(dists, axis=1) >= 0
    psum = jnp.zeros((tq, rr), F32)
    outs = []
    for h in range(hpg):
        tab_row = tab_ref[pl.ds(g * hpg + h, 1), :]
        bias = jnp.concatenate([_bias_gather(tab_row, d) for d in dists], axis=1)
        s = _dot_nt(q_ref[0, :, h * HEAD_B:(h + 1) * HEAD_B], kc) * scale + bias
        s = jnp.where(mask, s, NEG)
        m = jnp.max(s, axis=1, keepdims=True)
        e = jnp.where(mask, jnp.exp(s - m), 0.0)
        p = e / jnp.maximum(jnp.sum(e, axis=1, keepdims=True), 1e-30)
        outs.append(_dot(p.astype(BF16), vc))
        psum = psum + p
    o_ref[0] = jnp.concatenate(outs, axis=1)
    if tq < lanes:
        psum = jnp.concatenate([psum, jnp.zeros((lanes - tq, rr), F32)], axis=0)
    ph, plo = _split(psum)
    ovt = ovt_ref[...]
    imp = _dot_nt(ovt, ph) + _dot_nt(ovt, plo)
    nsr = ovt.shape[0]
    j = lax.broadcasted_iota(jnp.int32, (nsr, lanes), 0)
    cur = (qbase + lax.broadcasted_iota(jnp.int32, (nsr, lanes), 1)) // L_SEL
    valid = j <= cur
    forced = (j == 0) | (j == cur) | (j == cur - 1)
    score = jnp.where(valid, imp + jnp.where(forced, FORCE_BONUS, 0.0), NEG)
    sc_ref[...] = score

    def body(jp, rank):
        rowv = sc_ref[pl.ds(jp, 1), :]
        before = (rowv > score) | ((rowv == score) & (jp < j))
        return rank + jnp.where(before, 1.0, 0.0)

    rank = lax.fori_loop(0, ns, body, jnp.zeros((nsr, lanes), F32))
    selt_ref[0, 0] = jnp.where(valid & (rank < n_top), 1.0, 0.0)


def _cmp_attn(q, kcv, tab, *, t_keys, q0, tq):
    b, tqa, d_b = q.shape
    rr = kcv.shape[1]
    gw = d_b // N_KV
    ns = -(-t_keys // L_SEL)
    nsr = _round_up(ns, SUBLANE)
    lanes = max(tq, LANE)
    nqt = tqa // tq
    ci = np.arange(rr)
    sj = np.arange(nsr)
    ov = ((S_CMP * ci) // L_SEL)[None, :] == sj[:, None]
    ov |= ((S_CMP * ci + L_CMP - 1) // L_SEL)[None, :] == sj[:, None]
    ovt = jnp.asarray(ov.astype(np.float32), BF16)
    kern = functools.partial(_cmp_attn_kernel, tq=tq, q0=q0, ns=ns, n_top=min(TOP_N, ns), lanes=lanes)
    return pl.pallas_call(
        kern, grid=(b, N_KV, nqt),
        in_specs=[pl.BlockSpec((1, tq, gw), lambda i, g, t: (i, t, g)),
                  pl.BlockSpec((1, rr, LANE), lambda i, g, t: (i * 4 + g, 0, 0)),
                  pl.BlockSpec((1, rr, LANE), lambda i, g, t: (i * 4 + N_KV + g, 0, 0)),
                  pl.BlockSpec(ovt.shape, lambda i, g, t: (0, 0)),
                  pl.BlockSpec(tab.shape, lambda i, g, t: (0, 0))],
        out_specs=[pl.BlockSpec((1, tq, gw), lambda i, g, t: (i, t, g)),
                   pl.BlockSpec((1, 1, nsr, lanes), lambda i, g, t: (i, g, 0, t))],
        out_shape=[jax.ShapeDtypeStruct((b, tqa, d_b), F32),
                   jax.ShapeDtypeStruct((b, N_KV, nsr, nqt * lanes), F32)],
        scratch_shapes=[pltpu.VMEM((nsr, lanes), F32)], compiler_params=_cp(3), name="nsa_cmp_attn",
    )(q, kcv, kcv, ovt, tab)


def _flash_kernel(*refs, mode, band, tq, tk, nj, q0, kbase, n_tail, tail_pos0):
    refs = list(refs)
    q_ref, k_ref, v_ref = refs[:3]
    pos = 3
    if n_tail:
        kt_ref, vt_ref = refs[pos:pos + 2]
        pos += 2
    if mode == "sel":
        sel_ref = refs[pos]
        pos += 1
    tab_ref, o_ref, m_ref, l_ref, acc_ref = refs[pos:pos + 5]
    g = pl.program_id(1)
    qt = pl.program_id(2)
    j = pl.program_id(3)
    hpg = q_ref.shape[1] // HEAD_B
    scale = HEAD_B ** -0.5
    qbase = q0 + qt * tq

    @pl.when(j == 0)
    def _():
        m_ref[...] = jnp.full_like(m_ref, NEG)
        l_ref[...] = jnp.zeros_like(l_ref)
        acc_ref[...] = jnp.zeros_like(acc_ref)

    def update(k_r, v_r, kpos0, width, n_valid, near):
        qrow = qbase + lax.broadcasted_iota(jnp.int32, (tq, width), 0)
        kcol = kpos0 + lax.broadcasted_iota(jnp.int32, (tq, width), 1)
        dist = qrow - kcol
        mask = dist >= 0
        if n_valid is not None:
            mask &= lax.broadcasted_iota(jnp.int32, (tq, width), 1) < n_valid
        if mode == "win":
            mask &= dist < WINDOW
        else:
            nsp = sel_ref.shape[1]
            blk = (kpos0 + lax.broadcasted_iota(jnp.int32, (nsp, width), 1)) // L_SEL
            expand = (blk == lax.broadcasted_iota(jnp.int32, (nsp, width), 0)).astype(BF16)
            mask &= _dot(sel_ref[...].astype(BF16), expand) > 0.5
        kb = k_r[...].astype(BF16)
        vb = v_r[...].astype(BF16)
        _online_softmax_update(q_ref, kb, vb, mask, dist if near else None, g, tab_ref, m_ref, l_ref, acc_ref)

    if band:
        kt_abs = qt - (nj - 1) + j
        active = kt_abs >= 0
    else:
        kt_abs = j
        active = kbase + j * tk <= qbase + tq - 1
    kpos0 = kbase + kt_abs * tk
    is_far = qbase - (kpos0 + tk - 1) >= FAR

    @pl.when(active & is_far)
    def _():
        update(k_ref, v_ref, kpos0, tk, None, False)

    @pl.when(active & jnp.logical_not(is_far))
    def _():
        update(k_ref, v_ref, kpos0, tk, None, True)

    @pl.when(j == nj - 1)
    def _():
        if n_tail:
            update(kt_ref, vt_ref, tail_pos0, kt_ref.shape[0], n_tail, True)
        outs = [acc_ref[h] / jnp.maximum(l_ref[h][:, :1], 1e-30) for h in range(hpg)]
        o_ref[...] = jnp.concatenate(outs, axis=1)


def _flash(q, k_arr, k_spec, v_arr, v_spec, tab, *, mode, band, tq, tk, nj, q0, kbase,
           tail=None, sel=None, name):
    b, tqa, d_b = q.shape
    gw = d_b // N_KV
    hpg = gw // HEAD_B
    nqt = tqa // tq
    args = [q, k_arr, v_arr]
    in_specs = [pl.BlockSpec((None, tq, gw), lambda i, g, t, j: (i, t, g)), k_spec, v_spec]
    n_tail, tail_pos0 = 0, 0
    if tail is not None:
        tk_arr, tk_spec, tv_arr, tv_spec, n_tail, tail_pos0 = tail
        args += [tk_arr, tv_arr]
        in_specs += [tk_spec, tv_spec]
    if mode == "sel":
        args.append(sel)
        in_specs.append(pl.BlockSpec((None, None, tq, sel.shape[3]), lambda i, g, t, j: (i, g, t, 0)))
    args.append(tab)
    in_specs.append(pl.BlockSpec(tab.shape, lambda i, g, t, j: (0, 0)))
    kern = functools.partial(_flash_kernel, mode=mode, band=band, tq=tq, tk=tk, nj=nj, q0=q0, kbase=kbase,
                             n_tail=n_tail, tail_pos0=tail_pos0)
    return pl.pallas_call(
        kern, grid=(b, N_KV, nqt, nj), in_specs=in_specs,
        out_specs=pl.BlockSpec((None, tq, gw), lambda i, g, t, j: (i, t, g)),
        out_shape=jax.ShapeDtypeStruct((b, tqa, d_b), F32),
        scratch_shapes=[pltpu.VMEM((hpg, tq, LANE), F32), pltpu.VMEM((hpg, tq, LANE), F32),
                        pltpu.VMEM((hpg, tq, HEAD_B), F32)],
        compiler_params=_cp(4), name=name,
    )(*args)


def _attn_res_kernel(*refs, mode, tq, ch, far_w):
    if mode == "sel":
        q_ref, k_ref, v_ref, sel_ref, tab_ref, o_ref, m_ref, l_ref, acc_ref = refs
    else:
        q_ref, k_ref, v_ref, tab_ref, o_ref, m_ref, l_ref, acc_ref = refs
    g = pl.program_id(1)
    qt = pl.program_id(2)
    hpg = q_ref.shape[1] // HEAD_B
    scale = HEAD_B ** -0.5
    qbase = qt * tq
    near0 = qbase - LANE
    m_ref[...] = jnp.full_like(m_ref, NEG)
    l_ref[...] = jnp.zeros_like(l_ref)
    acc_ref[...] = jnp.zeros_like(acc_ref)

    def update(kpos0, width, near):
        qrow = qbase + lax.broadcasted_iota(jnp.int32, (tq, width), 0)
        kcol = kpos0 + lax.broadcasted_iota(jnp.int32, (tq, width), 1)
        dist = qrow - kcol
        mask = (dist >= 0) if near else (kcol < near0)
        if mode == "win":
            mask &= dist < WINDOW
        else:
            nsp = sel_ref.shape[1]
            blk = (kpos0 + lax.broadcasted_iota(jnp.int32, (nsp, width), 1)) // L_SEL
            expand = (blk == lax.broadcasted_iota(jnp.int32, (nsp, width), 0)).astype(BF16)
            mask &= _dot(sel_ref[...].astype(BF16), expand) > 0.5
        kb = k_ref[pl.ds(kpos0, width), :].astype(BF16)
        vb = v_ref[pl.ds(kpos0, width), :].astype(BF16)
        _online_softmax_update(q_ref, kb, vb, mask, dist if near else None, g, tab_ref, m_ref, l_ref, acc_ref)

    if mode == "sel":
        def body(c, carry):
            update(pl.multiple_of(c * ch, ch), ch, False)
            return carry

        lax.fori_loop(0, (jnp.maximum(near0, 0) + ch - 1) // ch, body, 0)
    elif far_w:
        @pl.when(near0 > 0)
        def _():
            update(pl.multiple_of(jnp.maximum(qbase - WINDOW, 0), LANE), far_w, False)

    update(pl.multiple_of(jnp.maximum(near0, 0), LANE), 2 * LANE, True)
    outs = [acc_ref[h] / jnp.maximum(l_ref[h][:, :1], 1e-30) for h in range(hpg)]
    o_ref[...] = jnp.concatenate(outs, axis=1)


def _attn_res(q, kv, kcol, vcol, tab, *, mode, sel=None, name):
    b, t, d_b = q.shape
    gw = d_b // N_KV
    hpg = gw // HEAD_B
    tq = LANE
    assert t % tq == 0 and t >= 2 * LANE
    ch = next(c for c in (512, 256, 128) if t % c == 0)
    far_w = min(WINDOW - LANE, t - 2 * LANE)
    args = [q, kv, kv]
    in_specs = [pl.BlockSpec((None, tq, gw), lambda i, g, qt: (i, qt, g)),
                pl.BlockSpec((None, t, LANE), lambda i, g, qt: (i, 0, kcol + g)),
                pl.BlockSpec((None, t, LANE), lambda i, g, qt: (i, 0, vcol + g))]
    if mode == "sel":
        args.append(sel)
        in_specs.append(pl.BlockSpec((None, None, tq, sel.shape[3]), lambda i, g, qt: (i, g, qt, 0)))
    args.append(tab)
    in_specs.append(pl.BlockSpec(tab.shape, lambda i, g, qt: (0, 0)))
    return pl.pallas_call(
        functools.partial(_attn_res_kernel, mode=mode, tq=tq, ch=ch, far_w=far_w),
        grid=(b, N_KV, t // tq), in_specs=in_specs,
        out_specs=pl.BlockSpec((None, tq, gw), lambda i, g, qt: (i, qt, g)),
        out_shape=jax.ShapeDtypeStruct((b, t, d_b), F32),
        scratch_shapes=[pltpu.VMEM((hpg, tq, LANE), F32), pltpu.VMEM((hpg, tq, LANE), F32),
                        pltpu.VMEM((hpg, tq, HEAD_B), F32)],
        compiler_params=_cp(3), name=name,
    )(*args)


def _nsa_out_kernel(oc_ref, os_ref, ow_ref, gl_ref, e_ref, w_ref, y_ref):
    gate = _sigmoid(gl_ref[...])
    o = (_dot_x(gate, e_ref[0]) * oc_ref[...] + _dot_x(gate, e_ref[1]) * os_ref[...]
         + _dot_x(gate, e_ref[2]) * ow_ref[...])
    y_ref[...] = _dot(o.astype(BF16), w_ref[...])


def _nsa_out(o_c, o_s, o_w, gl, w_ob):
    m, d_b = o_c.shape
    n = w_ob.shape[1]
    h_b = d_b // HEAD_B
    tm = _tile(m, 256)
    e = np.zeros((3, LANE, d_b), np.float32)
    for br in range(3):
        for hh in range(h_b):
            e[br, br * h_b + hh, hh * HEAD_B:(hh + 1) * HEAD_B] = 1.0
    e = jnp.asarray(e, BF16)
    row = pl.BlockSpec((tm, d_b), lambda i: (i, 0))
    return pl.pallas_call(
        _nsa_out_kernel, grid=(m // tm,),
        in_specs=[row, row, row, pl.BlockSpec((tm, LANE), lambda i: (i, 0)),
                  pl.BlockSpec(e.shape, lambda i: (0, 0, 0)), pl.BlockSpec(w_ob.shape, lambda i: (0, 0))],
        out_specs=pl.BlockSpec((tm, n), lambda i: (i, 0)), out_shape=jax.ShapeDtypeStruct((m, n), F32),
        compiler_params=_cp(1), name="nsa_out",
    )(o_c, o_s, o_w, gl, e, w_ob)


def _nsa(q, rows, win, gl, tab, lw, *, t_real, past8, lidx, cwin):
    b, tqa, d_b = q.shape
    t = t_real
    ncmp = 2 * N_KV
    cw = S_CMP * LANE
    p_len = 0 if past8 is None else past8.shape[3]
    l_tot = p_len + t
    assert p_len % (S_CMP * SUBLANE) == 0 and (p_len == 0 or t <= S_CMP) and (p_len > 0 or t % LANE == 0)
    if past8 is None:
        x_arr, n_pos = rows, t
        x_spec = pl.BlockSpec((None, t, LANE), lambda i: (i // ncmp, 0, i % ncmp))
        ex0 = jnp.zeros((b * ncmp, 1, cw), F32)
    else:
        x_arr, n_pos = past8, p_len
        x_spec = pl.BlockSpec((None, None, None, p_len, LANE), lambda i: (lidx, i // ncmp, i % ncmp, 0, 0))
        new_cmp = rows[:, :, :ncmp * LANE].reshape(b, t, ncmp, LANE).transpose(0, 2, 1, 3)
        ex0 = jnp.pad(new_cmp.reshape(b * ncmp, 1, t * LANE), ((0, 0), (0, 0), (0, cw - t * LANE)))
    pe = jnp.repeat(lw["cmp_pe"].reshape(2, 1, 2, cw), N_KV, axis=1)
    pe = jnp.broadcast_to(pe[None], (b, 2, N_KV, 2, cw)).reshape(b * ncmp, 2, cw)
    ex = jnp.concatenate([ex0, pe, jnp.zeros((b * ncmp, SUBLANE - 3, cw), F32)], axis=1)
    kcv = _compress(x_arr, x_spec, n_pos, ex, lw["w1cat"], lw["w2"])
    if kcv.shape[1] % LANE:
        kcv = jnp.pad(kcv, ((0, 0), (0, _round_up(kcv.shape[1], LANE) - kcv.shape[1]), (0, 0)))
    tq = tqa if p_len else LANE
    o_c, selt = _cmp_attn(q, kcv, tab, t_keys=l_tot, q0=p_len, tq=tq)
    ns = -(-l_tot // L_SEL)
    nsp = _round_up(ns, LANE)
    sel = jnp.swapaxes(selt, 2, 3)[:, :, :tqa]
    sel = jnp.pad(sel, ((0, 0), (0, 0), (0, 0), (0, nsp - sel.shape[3])))
    if past8 is None:
        o_s = _attn_res(q, rows, 2 * N_KV, 3 * N_KV, tab, mode="sel", sel=sel, name="nsa_sel_attn")
        o_w = _attn_res(q, win, 0, N_KV, tab, mode="win", name="nsa_win_attn")
    else:
        tk = next(c for c in (2048, 1024, 512, 256, 128) if p_len % c == 0)
        rows_t = jnp.pad(rows, ((0, 0), (0, LANE - t), (0, 0)))
        win_t = jnp.pad(win, ((0, 0), (0, LANE - t), (0, 0)))
        tails = lambda col: pl.BlockSpec((None, LANE, LANE), lambda i, g, qt, j, col=col: (i, 0, col + g))
        ks = lambda s0: pl.BlockSpec((None, None, None, tk, LANE),
                                     lambda i, g, qt, j, s0=s0: (lidx, i, s0 + g, j, 0))
        o_s = _flash(q, past8, ks(2 * N_KV), past8, ks(3 * N_KV), tab, mode="sel", band=False, tq=tq, tk=tk,
                     nj=p_len // tk, q0=p_len, kbase=0, sel=sel,
                     tail=(rows_t, tails(2 * N_KV), rows_t, tails(3 * N_KV), t, p_len), name="nsa_sel_attn_paged")
        wb = cwin.shape[1]
        kw = lambda col: pl.BlockSpec((None, wb, LANE), lambda i, g, qt, j, col=col: (i, 0, col + g))
        o_w = _flash(q, cwin, kw(0), cwin, kw(N_KV), tab, mode="win", band=False, tq=tq, tk=wb,
                     nj=1, q0=p_len, kbase=p_len - wb,
                     tail=(win_t, tails(0), win_t, tails(N_KV), t, p_len), name="nsa_win_attn_cached")
    m = b * tqa
    return _nsa_out(o_c.reshape(m, d_b), o_s.reshape(m, d_b), o_w.reshape(m, d_b), gl, lw["w_ob"])


def _layer_weights(l, w_in, mu, w0, w_up, a0, a_up, g_up, k_k, k_a, r_k, lnx_w, lnx_b, w_oa,
                   cmp_pe, cmp_w1, cmp_w2, w_ob, w_o, w_ff_up, w_ff_down, norm_g):
    d_a = w_oa.shape[1]
    d_b = w_ob.shape[1]
    d = w_o.shape[1]
    rw = 3 * d_a + DECAY_LORA + A_LORA + GATE_LORA
    pw = 3 * d_a + LORA_W
    nrow = 4 * N_KV * HEAD_B
    nwin = 2 * N_KV * HEAD_B
    ngl = 3 * (d_b // HEAD_B)
    wi = w_in[l]
    o_q = rw
    o_rows = o_q + d_b
    o_win = o_rows + nrow
    o_gl = o_win + nwin
    o_pg = o_gl + ngl
    wl = jnp.zeros((LORA_W, 3 * d_a), F32)
    wl = wl.at[:DECAY_LORA, :d_a].set(w_up[l])
    wl = wl.at[DECAY_LORA:DECAY_LORA + A_LORA, d_a:2 * d_a].set(a_up[l])
    wl = wl.at[DECAY_LORA + A_LORA:DECAY_LORA + A_LORA + GATE_LORA, 2 * d_a:].set(g_up[l])
    half = S_CMP * HEAD_B
    return dict(
        w_pa=jnp.pad(wi[:, :rw], ((0, 0), (0, pw - rw))).astype(BF16),
        w_q=wi[:, o_q:o_rows].astype(BF16),
        w_rows=wi[:, o_rows:o_win].astype(BF16),
        w_win=wi[:, o_win:o_gl].astype(BF16),
        w_gl=jnp.pad(wi[:, o_gl:o_pg], ((0, 0), (0, LANE - ngl))).astype(BF16),
        w_pg=wi[:, o_pg:o_pg + 2 * d].astype(BF16),
        mu=jnp.pad(mu[l], (0, pw - rw))[None], wl=wl.astype(BF16),
        w0=w0[l][None], a0=a0[l][None], k_k=k_k[l][None], k_a=k_a[l][None],
        r_k=r_k[l].reshape(1, d_a), lnx_w=lnx_w[l][None], lnx_b=lnx_b[l][None],
        w_oa=w_oa[l].astype(BF16), w_ob=w_ob[l].astype(BF16), w_o=w_o[l].astype(BF16),
        w_up=w_ff_up[l].astype(BF16), w_down=w_ff_down[l].astype(BF16),
        cmp_pe=cmp_pe[l],
        w1cat=jnp.concatenate([cmp_w1[l][:, :half], cmp_w1[l][:, half:]], axis=2).astype(BF16),
        w2=cmp_w2[l].astype(BF16), g=norm_g[l], rw=rw,
    )


def _layer(x, lw, tab, shift0, s0, past8, lidx, cwin):
    b, t, d = x.shape
    m = b * t
    d_a = lw["w_oa"].shape[0]
    x2 = x.reshape(m, d)
    g = lw["g"]
    xn = _rmsnorm_cast(x2, g[0:1])
    pa = _matmul(xn, lw["w_pa"], F32, name="proj_rwkv")
    q = _matmul(xn, lw["w_q"], BF16, name="proj_q")
    rows = _matmul(xn, lw["w_rows"], F32, name="proj_rows")
    win = _matmul(xn, lw["w_win"], F32, name="proj_win")
    gl = _matmul(xn, lw["w_gl"], F32, name="proj_gl")
    pg = _matmul(xn, lw["w_pg"], F32, name="proj_pg")
    pw = pa.shape[1]
    d_b = q.shape[1]
    tp = t if t % RWKV_CHUNK == 0 else _round_up(t, SUBLANE)
    c = RWKV_CHUNK if t % RWKV_CHUNK == 0 else tp
    pa3 = pa.reshape(b, t, pw)
    pa_p = pa3 if tp == t else jnp.pad(pa3, ((0, 0), (0, tp - t), (0, 0)))
    prev = jnp.pad(shift0, ((0, 0), (0, pw - shift0.shape[1])))[:, None]
    r, lgw, k2, v, kk, a, gg = _rwkv_prep(pa_p, prev, lw["mu"], lw["wl"], lw["w0"], lw["a0"], lw["k_k"], lw["k_a"],
                                          None if tp == t else t)
    o, s_fin = _rwkv_scan(r, lgw, k2, v, kk, a, s0, c)
    flat = lambda u: u.reshape(b * tp, d_a)
    ya = _rwkv_post(flat(o), flat(r), flat(k2), flat(v), flat(gg), lw["lnx_w"], lw["lnx_b"], lw["r_k"], lw["w_oa"])
    if tp != t:
        ya = ya.reshape(b, tp, d)[:, :t].reshape(m, d)
    sh = pa3[:, t - 1, :lw["rw"]]
    tqa = t if t % LANE == 0 else _round_up(t, SUBLANE)
    q3 = q.reshape(b, t, d_b)
    gl_p = gl
    if tqa != t:
        q3 = jnp.pad(q3, ((0, 0), (0, tqa - t), (0, 0)))
        gl_p = jnp.pad(gl.reshape(b, t, LANE), ((0, 0), (0, tqa - t), (0, 0))).reshape(b * tqa, LANE)
    rows3 = rows.reshape(b, t, rows.shape[1])
    win3 = win.reshape(b, t, win.shape[1])
    yb = _nsa(q3, rows3, win3, gl_p, tab, lw, t_real=t, past8=past8, lidx=lidx, cwin=cwin)
    if tqa != t:
        yb = yb.reshape(b, tqa, d)[:, :t].reshape(m, d)
    x1 = _merge(ya, yb, pg, x2, lw["w_o"], g[1:2])
    h = _matmul(_rmsnorm_cast(x1, g[2:3]), lw["w_up"], BF16, relu2=True, name="ffn_up")
    x_out = _ffn_down(h, lw["w_down"], x1, g[3:4])
    wctx = win3 if cwin is None else jnp.concatenate([cwin, win3], axis=1)
    n_keep = min(WINDOW, wctx.shape[1])
    return x_out.reshape(b, t, d), rows3, wctx[:, wctx.shape[1] - n_keep:], sh, s_fin


def _pair_states(s):
    b, h, n, _ = s.shape
    s = s.reshape(b, h // 2, 2, n, n)
    z = jnp.zeros((b, h // 2, n, n), s.dtype)
    top = jnp.concatenate([s[:, :, 0], z], axis=3)
    bot = jnp.concatenate([z, s[:, :, 1]], axis=3)
    return jnp.concatenate([top, bot], axis=2)


def _unpair_states(s):
    n = HEAD_A
    b, hp = s.shape[:2]
    return jnp.stack([s[:, :, :n, :n], s[:, :, n:, n:]], axis=2).reshape(b, 2 * hp, n, n)


def _trunk(x, shift0, wkv0, past8, cache_win, tab, layers):
    rows, wins, shifts, wkvs = [], [], [], []
    for l, lw in enumerate(layers):
        x, nr, nw, sh, st = _layer(x, lw, tab, shift0[l], _pair_states(wkv0[l]),
                                   past8, l, None if cache_win is None else cache_win[l])
        b, t = nr.shape[:2]
        rows.append(nr.reshape(b, t, 4, N_KV, HEAD_B))
        wins.append(nw.reshape(b, nw.shape[1], 2, N_KV, HEAD_B))
        shifts.append(sh)
        wkvs.append(_unpair_states(st))
    return x, jnp.stack(rows), jnp.stack(wins), jnp.stack(shifts), jnp.stack(wkvs)


def kernel(x_prompt, x_sample, cache_kv, cache_win, state_shift, state_wkv, page_table, w_in, mu, w0, w_up, a0, a_up, g_up, k_k, k_a, r_k, lnx_w, lnx_b, w_oa, cmp_pe, cmp_w1, cmp_w2, w_ob, w_o, w_ff_up, w_ff_down, norm_g, rel_bias):
    depth = w_in.shape[0]
    layers = [_layer_weights(l, w_in, mu, w0, w_up, a0, a_up, g_up, k_k, k_a, r_k, lnx_w, lnx_b, w_oa,
                             cmp_pe, cmp_w1, cmp_w2, w_ob, w_o, w_ff_up, w_ff_down, norm_g) for l in range(depth)]
    tab = rel_bias[_t5_bucket_table()].T
    bp = x_prompt.shape[0]
    rw = state_shift.shape[2]
    zeros_shift = jnp.zeros((depth, bp, rw), x_prompt.dtype)
    zeros_wkv = jnp.zeros((depth, bp) + state_wkv.shape[2:], state_wkv.dtype)
    y_p, kv_p, win_p, sh_p, wkv_p = _trunk(x_prompt, zeros_shift, zeros_wkv, None, None, tab, layers)
    nl, n_pool, page = cache_kv.shape[:3]
    ns = 4 * N_KV
    past8 = _page_gather(cache_kv.reshape(nl, n_pool, page * ns, HEAD_B), page_table, ns, page)
    cwin = cache_win.reshape(cache_win.shape[:3] + (-1,))
    y_s, kv_s, win_s, sh_s, wkv_s = _trunk(x_sample, state_shift, state_wkv, past8, cwin, tab, layers)
    return (y_p, y_s, kv_p, kv_s, win_p, win_s, sh_p, sh_s, wkv_p, wkv_s)
```

```python
import functools
import math

import numpy as np
import jax
import jax.numpy as jnp
from jax import lax
from jax.experimental import pallas as pl
from jax.experimental.pallas import tpu as pltpu

F32 = jnp.float32
BF16 = jnp.bfloat16

HEAD_A = 64
DECAY_LORA = 64
A_LORA = 64
GATE_LORA = 160
LNX_EPS = 64e-5
N_KV = 2
HEAD_B = 128
L_CMP = 32
S_CMP = 16
L_SEL = 64
TOP_N = 16
WINDOW = 512
FORCE_BONUS = 1e4
NEG = -1e30
NUM_BUCKETS = 32
MAX_DIST = 128
EPS = 1e-6
PAGE_SIZE = 128

LANE = 128
SUBLANE = 8
VMEM_LIMIT = 56 * 1024 * 1024
RWKV_CHUNK = 64
LORA_W = 384
FAR = LANE - 1


def _cp(n_axes):
    return pltpu.CompilerParams(dimension_semantics=("arbitrary",) * n_axes,
                                vmem_limit_bytes=VMEM_LIMIT)


def _round_up(x, m):
    return -(-x // m) * m


def _tile(n, pref, mult=SUBLANE):
    if n <= pref:
        return n
    for t in range(pref - pref % mult, 0, -mult):
        if n % t == 0:
            return t
    return n


def _dot(a, b):
    return jnp.dot(a, b, preferred_element_type=F32)


def _dot_nt(a, b):
    return lax.dot_general(a, b, (((1,), (1,)), ((), ())), preferred_element_type=F32)


def _dot_tn(a, b):
    return lax.dot_general(a, b, (((0,), (0,)), ((), ())), preferred_element_type=F32)


def _split(x):
    hi = x.astype(BF16)
    lo = (x - hi.astype(F32)).astype(BF16)
    return hi, lo


def _dot_x(a, b):
    hi, lo = _split(a)
    return _dot(hi, b) + _dot(lo, b)


def _dot3(a, b):
    ah, al = _split(a)
    bh, bl = _split(b)
    return _dot(ah, bh) + _dot(ah, bl) + _dot(al, bh)


def _dot3_nt(a, b):
    ah, al = _split(a)
    bh, bl = _split(b)
    return _dot_nt(ah, bh) + _dot_nt(ah, bl) + _dot_nt(al, bh)


def _dot3_tn(a, b):
    ah, al = _split(a)
    bh, bl = _split(b)
    return _dot_tn(ah, bh) + _dot_tn(ah, bl) + _dot_tn(al, bh)


def _sigmoid(x):
    return 1.0 / (1.0 + jnp.exp(-x))


def _rms(y, g):
    return y * lax.rsqrt(jnp.mean(y * y, axis=-1, keepdims=True) + EPS) * g


def _rmsnorm_kernel(x_ref, g_ref, o_ref):
    o_ref[...] = _rms(x_ref[...], g_ref[...]).astype(o_ref.dtype)


def _rmsnorm_cast(x, g):
    m, d = x.shape
    tm = _tile(m, 256)
    return pl.pallas_call(
        _rmsnorm_kernel, grid=(m // tm,),
        in_specs=[pl.BlockSpec((tm, d), lambda i: (i, 0)), pl.BlockSpec((1, d), lambda i: (0, 0))],
        out_specs=pl.BlockSpec((tm, d), lambda i: (i, 0)),
        out_shape=jax.ShapeDtypeStruct((m, d), BF16), compiler_params=_cp(1), name="rmsnorm_cast",
    )(x, g)


def _mm_kernel(a_ref, w_ref, o_ref, *, relu2):
    y = _dot(a_ref[...], w_ref[...])
    if relu2:
        y = jnp.square(jnp.maximum(y, 0.0))
    o_ref[...] = y.astype(o_ref.dtype)


def _matmul(a, w, out_dtype, relu2=False, name="matmul"):
    m, k = a.shape
    n = w.shape[1]
    tm = _tile(m, 1024)
    tn = _tile(n, 1280, LANE)
    return pl.pallas_call(
        functools.partial(_mm_kernel, relu2=relu2), grid=(m // tm, n // tn),
        in_specs=[pl.BlockSpec((tm, k), lambda i, j: (i, 0)), pl.BlockSpec((k, tn), lambda i, j: (0, j))],
        out_specs=pl.BlockSpec((tm, tn), lambda i, j: (i, j)),
        out_shape=jax.ShapeDtypeStruct((m, n), out_dtype), compiler_params=_cp(2), name=name,
    )(a, w)


def _ffn_down_kernel(h_ref, w_ref, x_ref, g_ref, o_ref, acc_ref):
    k = pl.program_id(1)

    @pl.when(k == 0)
    def _():
        acc_ref[...] = jnp.zeros_like(acc_ref)

    acc_ref[...] += _dot(h_ref[...], w_ref[...])

    @pl.when(k == pl.num_programs(1) - 1)
    def _():
        o_ref[...] = x_ref[...] + _rms(acc_ref[...], g_ref[...])


def _ffn_down(h, w, x, g):
    m, kdim = h.shape
    n = w.shape[1]
    tm = _tile(m, 512)
    tk = _tile(kdim, 1024, LANE)
    return pl.pallas_call(
        _ffn_down_kernel, grid=(m // tm, kdim // tk),
        in_specs=[pl.BlockSpec((tm, tk), lambda i, k: (i, k)), pl.BlockSpec((tk, n), lambda i, k: (k, 0)),
                  pl.BlockSpec((tm, n), lambda i, k: (i, 0)), pl.BlockSpec((1, n), lambda i, k: (0, 0))],
        out_specs=pl.BlockSpec((tm, n), lambda i, k: (i, 0)),
        out_shape=jax.ShapeDtypeStruct((m, n), F32),
        scratch_shapes=[pltpu.VMEM((tm, n), F32)], compiler_params=_cp(2), name="ffn_down",
    )(h, w, x, g)


def _merge_kernel(ya_ref, yb_ref, pg_ref, x_ref, w_ref, g_ref, o_ref):
    d = ya_ref.shape[1]
    pg = pg_ref[...]
    mix = _sigmoid(pg[:, :d]) * ya_ref[...] + _sigmoid(pg[:, d:]) * yb_ref[...]
    y = _dot(mix.astype(BF16), w_ref[...])
    o_ref[...] = x_ref[...] + _rms(y, g_ref[...])


def _merge(ya, yb, pg, x, w_o, g):
    m, d = x.shape
    tm = _tile(m, 256)
    row = lambda c: pl.BlockSpec((tm, c), lambda i: (i, 0))
    return pl.pallas_call(
        _merge_kernel, grid=(m // tm,),
        in_specs=[row(d), row(d), row(2 * d), row(d),
                  pl.BlockSpec((d, d), lambda i: (0, 0)), pl.BlockSpec((1, d), lambda i: (0, 0))],
        out_specs=row(d), out_shape=jax.ShapeDtypeStruct((m, d), F32),
        compiler_params=_cp(1), name="merge",
    )(ya, yb, pg, x, w_o, g)


def _head_indicator(d_a):
    h = np.zeros((d_a, LANE), np.float32)
    h[np.arange(d_a), np.arange(d_a) // HEAD_A] = 1.0
    return jnp.asarray(h, BF16), jnp.asarray(h.T, BF16)


def _rwkv_prep_kernel(pa_ref, prev_ref, mu_ref, wl_ref, w0_ref, a0_ref, kk_ref, ka_ref, hd_ref, hdt_ref,
                      r_ref, lw_ref, k2_ref, v_ref, kkn_ref, a_ref, g_ref, carry_ref, *, d_a, t_real):
    t = pl.program_id(1)
    tt = pa_ref.shape[1]

    @pl.when(t == 0)
    def _():
        carry_ref[...] = prev_ref[0]

    x = pa_ref[0]
    row = lax.broadcasted_iota(jnp.int32, x.shape, 0)
    xprev = jnp.where(row == 0, carry_ref[...], pltpu.roll(x, 1, 0))
    carry_ref[...] = pa_ref[0, pl.ds(tt - 1, 1), :]
    xs = x + (xprev - x) * mu_ref[...]
    r = xs[:, :d_a]
    k = xs[:, d_a:2 * d_a]
    v = xs[:, 2 * d_a:3 * d_a]
    lo = xs[:, 3 * d_a:]
    lane = lax.broadcasted_iota(jnp.int32, lo.shape, 1)
    act = jnp.where(lane < DECAY_LORA, jnp.tanh(lo),
                    jnp.where(lane < DECAY_LORA + A_LORA, lo, _sigmoid(lo)))
    lin = _dot(act.astype(BF16), wl_ref[...])
    z = -(w0_ref[...] + lin[:, :d_a])
    w = -(jnp.maximum(z, 0.0) + jnp.log(1.0 + jnp.exp(-jnp.abs(z)))) - 0.5
    logw = -jnp.exp(w)
    a = _sigmoid(a0_ref[...] + lin[:, d_a:2 * d_a])
    g = lin[:, 2 * d_a:]
    kkr = k * kk_ref[...]
    ss = _dot_x(kkr * kkr, hd_ref[...])
    inv = 1.0 / jnp.maximum(jnp.sqrt(ss), 1e-12)
    kkn = kkr * _dot_x(inv, hdt_ref[...])
    k2 = k * (1.0 + (a - 1.0) * ka_ref[...])
    if t_real is not None:
        live = (t * tt + lax.broadcasted_iota(jnp.int32, r.shape, 0)) < t_real
        zero = jnp.zeros_like(r)
        r, logw, k2, v, kkn, a = (jnp.where(live, u, zero) for u in (r, logw, k2, v, kkn, a))
    r_ref[0] = r
    lw_ref[0] = logw
    k2_ref[0] = k2
    v_ref[0] = v
    kkn_ref[0] = kkn
    a_ref[0] = a
    g_ref[0] = g


def _rwkv_prep(pa, prev, mu, wl, w0, a0, k_k, k_a, t_real):
    b, t, p = pa.shape
    d_a = w0.shape[1]
    tt = _tile(t, 256)
    hd, hdt = _head_indicator(d_a)
    full = lambda a: pl.BlockSpec(a.shape, lambda i, j: (0,) * a.ndim)
    out = jax.ShapeDtypeStruct((b, t, d_a), F32)
    ospec = pl.BlockSpec((1, tt, d_a), lambda i, j: (i, j, 0))
    return pl.pallas_call(
        functools.partial(_rwkv_prep_kernel, d_a=d_a, t_real=t_real), grid=(b, t // tt),
        in_specs=[pl.BlockSpec((1, tt, p), lambda i, j: (i, j, 0)), pl.BlockSpec((1, 1, p), lambda i, j: (i, 0, 0)),
                  full(mu), full(wl), full(w0), full(a0), full(k_k), full(k_a), full(hd), full(hdt)],
        out_specs=[ospec] * 7, out_shape=[out] * 7,
        scratch_shapes=[pltpu.VMEM((1, p), F32)], compiler_params=_cp(2), name="rwkv_prep",
    )(pa, prev, mu, wl, w0, a0, k_k, k_a, hd, hdt)


def _mmb(a, b, form="nn"):
    dot = {"nn": _dot, "nt": _dot_nt, "tn": _dot_tn}[form]
    return dot(a.astype(BF16), b.astype(BF16))


def _rwkv_scan_kernel(r_ref, lw_ref, k2_ref, v_ref, kk_ref, a_ref, s0_ref, o_ref, sT_ref, s_ref, *, c):
    @pl.when(pl.program_id(1) == 0)
    def _():
        s_ref[...] = s0_ref[0]

    hk = HEAD_A
    pairs = range(s_ref.shape[0])
    sls = [slice(p * LANE, (p + 1) * LANE) for p in pairs]
    lane = lax.broadcasted_iota(jnp.int32, (c, LANE), 1)
    m0 = lane < hk
    rowi = lax.broadcasted_iota(jnp.int32, (c, c), 0)
    coli = lax.broadcasted_iota(jnp.int32, (c, c), 1)
    ltri = (coli <= rowi).astype(BF16)
    lw = [lw_ref[0, :, sl] for sl in sls]
    cum = [_dot_x_rhs(ltri, x) for x in lw]
    e_wi = [jnp.exp(-x) for x in cum]
    at = [-kk_ref[0, :, sl] * jnp.exp(x - y) for sl, x, y in zip(sls, cum, lw)]
    bt = [kk_ref[0, :, sl] * a_ref[0, :, sl] * e for sl, e in zip(sls, e_wi)]
    kt = [k2_ref[0, :, sl] * e for sl, e in zip(sls, e_wi)]
    rt = [r_ref[0, :, sl] * jnp.exp(x) for sl, x in zip(sls, cum)]
    v = [v_ref[0, :, sl] for sl in sls]
    w_c = [jnp.exp(x[c - 1:c, :]) for x in cum]

    def bd(z):
        zero = jnp.zeros_like(z)
        return jnp.concatenate([jnp.where(m0, z, zero), jnp.where(m0, zero, z)], axis=0)

    prow = lax.broadcasted_iota(jnp.int32, (c, 2 * c), 0)
    pcol = lax.broadcasted_iota(jnp.int32, (c, 2 * c), 1)
    pcol = jnp.where(pcol >= c, pcol - c, pcol)
    strict = pcol < prow
    incl = pcol <= prow
    zero_cc = jnp.zeros((c, 2 * c), F32)
    x2 = [jnp.concatenate([x, y], axis=0) for x, y in zip(at, rt)]
    xb = [_mmb(x, bd(y), "nt") for x, y in zip(x2, bt)]
    xk = [_mmb(x, bd(y), "nt") for x, y in zip(x2, kt)]
    a_ab = [jnp.where(strict, x[:c], zero_cc) for x in xb]
    a_rb = [jnp.where(incl, x[c:], zero_cc) for x in xb]
    a_ak = [jnp.where(strict, x[:c], zero_cc) for x in xk]
    a_rk = [jnp.where(incl, x[c:], zero_cc) for x in xk]

    if 2 * c == LANE:
        bdc = bd
    else:
        mc = lax.broadcasted_iota(jnp.int32, (c, 2 * c), 1) < c

        def bdc(z):
            zero = jnp.zeros_like(z)
            return jnp.concatenate([jnp.where(mc, z, zero), jnp.where(mc, zero, z)], axis=0)

    eye = (pcol == prow).astype(F32)
    tmat = [eye + x for x in a_ab]
    npow = a_ab
    steps = 1
    while 2 * steps < c:
        npow = [_dot3(x, bdc(x)) for x in npow]
        tmat = [x + _dot3(x, bdc(y)) for x, y in zip(tmat, npow)]
        steps *= 2
    ta = [_mmb(x, bd(y)) for x, y in zip(tmat, at)]
    xv = [_mmb(x, bd(y)) for x, y in zip(a_ak, v)]
    tx = [_mmb(x, bd(y)) for x, y in zip(tmat, xv)]
    p_c = [x + _mmb(y, bd(z)) for x, y, z in zip(rt, a_rb, ta)]
    q_c = [_mmb(x, bd(y)) + _mmb(z, bd(u)) for x, y, z, u in zip(a_rb, tx, a_rk, v)]
    lr = lax.broadcasted_iota(jnp.int32, (LANE, LANE), 0)
    lc = lax.broadcasted_iota(jnp.int32, (LANE, LANE), 1)
    same = (lr < hk) == (lc < hk)
    eye_l = (lr == lc).astype(F32)
    zero_l = jnp.zeros((LANE, LANE), F32)
    m_c = [(eye_l + jnp.where(same, _mmb(x, y, "tn"), zero_l)) * w for x, y, w in zip(ta, bt, w_c)]
    n_c = [jnp.where(same, _mmb(x, y, "tn") + _mmb(z, u, "tn"), zero_l) * w
           for x, y, z, u, w in zip(tx, bt, v, kt, w_c)]
    s = [s_ref[p] for p in pairs]
    o = [_mmb(x, y, "nt") + z for x, y, z in zip(p_c, s, q_c)]
    s_new = [_mmb(x, y) + z for x, y, z in zip(s, m_c, n_c)]
    for p in pairs:
        o_ref[0, :, sls[p]] = o[p]
        s_ref[p] = s_new[p]
        sT_ref[0, p] = s_new[p]


def _dot_x_rhs(a, b):
    hi, lo = _split(b)
    return _dot(a, hi) + _dot(a, lo)


def _rwkv_scan(r, lw, k2, v, kk, a, s0, c):
    b, t, d_a = r.shape
    npair = d_a // LANE
    seq = pl.BlockSpec((1, c, d_a), lambda i, j: (i, j, 0))
    st = pl.BlockSpec((1, npair, LANE, LANE), lambda i, j: (i, 0, 0, 0))
    return pl.pallas_call(
        functools.partial(_rwkv_scan_kernel, c=c), grid=(b, t // c),
        in_specs=[seq] * 6 + [st], out_specs=[seq, st],
        out_shape=[jax.ShapeDtypeStruct((b, t, d_a), F32), jax.ShapeDtypeStruct((b, npair, LANE, LANE), F32)],
        scratch_shapes=[pltpu.VMEM((npair, LANE, LANE), F32)], compiler_params=_cp(2), name="rwkv_scan",
    )(r, lw, k2, v, kk, a, s0)


def _rwkv_post_kernel(o_ref, r_ref, k2_ref, v_ref, g_ref, lw_ref, lb_ref, rk_ref, hd_ref, hdt_ref, w_ref, y_ref):
    o = o_ref[...]
    hd = hd_ref[...]
    hdt = hdt_ref[...]
    inv_n = 1.0 / HEAD_A
    mean = _dot_x(_dot_x(o, hd) * inv_n, hdt)
    d = o - mean
    var = _dot_x(d * d, hd) * inv_n
    xo = d * _dot_x(lax.rsqrt(var + LNX_EPS), hdt) * lw_ref[...] + lb_ref[...]
    bonus = _dot_x(_dot_x(r_ref[...] * k2_ref[...] * rk_ref[...], hd), hdt) * v_ref[...]
    y_ref[...] = _dot(((xo + bonus) * g_ref[...]).astype(BF16), w_ref[...])


def _rwkv_post(o, r, k2, v, g, lnx_w, lnx_b, r_k, w_oa):
    m, d_a = o.shape
    n = w_oa.shape[1]
    tm = _tile(m, 256)
    hd, hdt = _head_indicator(d_a)
    row = pl.BlockSpec((tm, d_a), lambda i: (i, 0))
    full = lambda a: pl.BlockSpec(a.shape, lambda i: (0,) * a.ndim)
    return pl.pallas_call(
        _rwkv_post_kernel, grid=(m // tm,),
        in_specs=[row] * 5 + [full(lnx_w), full(lnx_b), full(r_k), full(hd), full(hdt), full(w_oa)],
        out_specs=pl.BlockSpec((tm, n), lambda i: (i, 0)), out_shape=jax.ShapeDtypeStruct((m, n), F32),
        compiler_params=_cp(1), name="rwkv_post",
    )(o, r, k2, v, g, lnx_w, lnx_b, r_k, hd, hdt, w_oa)


def _t5_bucket_table():
    d = np.arange(LANE)
    max_exact = NUM_BUCKETS // 2
    df = np.maximum(d, 1).astype(np.float32)
    large = max_exact + (np.log(df / np.float32(max_exact)) / np.float32(math.log(MAX_DIST / max_exact))
                         * np.float32(NUM_BUCKETS - max_exact)).astype(np.int32)
    large = np.minimum(large, NUM_BUCKETS - 1)
    tab = np.where(d < max_exact, d, large)
    assert tab[FAR] == NUM_BUCKETS - 1
    return tab


def _page_gather_kernel(pt_ref, *refs, ns, page):
    o_ref = refs[-1]
    for k, x_ref in enumerate(refs[:-1]):
        for s in range(ns):
            o_ref[0, 0, s, k * page:(k + 1) * page, :] = x_ref[0, 0, pl.ds(s, page, stride=ns), :]


def _page_gather(cache, page_table, ns, page):
    nl = cache.shape[0]
    b, n_pages = page_table.shape
    npp = next(k for k in (4, 2, 1) if n_pages % k == 0)
    in_spec = lambda k: pl.BlockSpec((1, 1, page * ns, LANE), lambda l, i, p, pt: (l, pt[i, p * npp + k], 0, 0))
    grid_spec = pltpu.PrefetchScalarGridSpec(
        num_scalar_prefetch=1, grid=(nl, b, n_pages // npp),
        in_specs=[in_spec(k) for k in range(npp)],
        out_specs=pl.BlockSpec((1, 1, ns, npp * page, LANE), lambda l, i, p, pt: (l, i, 0, p, 0)))
    return pl.pallas_call(
        functools.partial(_page_gather_kernel, ns=ns, page=page), grid_spec=grid_spec,
        out_shape=jax.ShapeDtypeStruct((nl, b, ns, n_pages * page, LANE), cache.dtype),
        compiler_params=_cp(3), name="page_gather",
    )(page_table, *([cache] * npp))


def _gelu_tanh(x):
    return 0.5 * x * (1.0 + jnp.tanh(math.sqrt(2.0 / math.pi) * (x + 0.044715 * (x * x * x))))


def _compress_kernel(x_ref, ex_ref, w1_ref, w2_ref, o_ref):
    rr = x_ref.shape[0] // S_CMP
    w1 = w1_ref[0]
    y = jnp.zeros((rr, 2 * LANE), F32)
    for j in range(S_CMP):
        xj = x_ref[pl.ds(j, rr, stride=S_CMP), :].astype(BF16)
        y = y + _dot(xj, w1[j * LANE:(j + 1) * LANE, :])
    e = _dot(ex_ref[0].astype(BF16), w1)
    ybot = y[:, LANE:]
    nxt = pltpu.roll(ybot, rr - 1, 0)
    row = lax.broadcasted_iota(jnp.int32, nxt.shape, 0)
    nxt = jnp.where(row == rr - 1, e[0:1, LANE:], nxt)
    pre = y[:, :LANE] + nxt + (e[1:2, :LANE] + e[2:3, LANE:])
    o_ref[0] = _dot(_gelu_tanh(pre).astype(BF16), w2_ref[0])


def _compress(x, x_spec, n_pos, ex, w1cat, w2):
    nb, _, cw = ex.shape
    rr = n_pos // S_CMP
    c_of = lambda i: (i // N_KV) % 2
    return pl.pallas_call(
        _compress_kernel, grid=(nb,),
        in_specs=[x_spec, pl.BlockSpec((1, SUBLANE, cw), lambda i: (i, 0, 0)),
                  pl.BlockSpec((1, cw, 2 * LANE), lambda i: (c_of(i), 0, 0)),
                  pl.BlockSpec((1, LANE, LANE), lambda i: (c_of(i), 0, 0))],
        out_specs=pl.BlockSpec((1, rr, LANE), lambda i: (i, 0, 0)),
        out_shape=jax.ShapeDtypeStruct((nb, rr, LANE), F32), compiler_params=_cp(1), name="nsa_compress",
    )(x, ex, w1cat, w2)


def _bias_gather(tab_row, dist):
    idx = jnp.clip(dist, 0, FAR)
    return jnp.take_along_axis(jnp.broadcast_to(tab_row, idx.shape), idx, axis=1)


def _stack_heads(q_ref):
    return jnp.concatenate([q_ref[:, h * HEAD_B:(h + 1) * HEAD_B] for h in range(q_ref.shape[1] // HEAD_B)], axis=0)


def _rel_bias(dist, g, hpg, tab_ref):
    rel = []
    for h in range(hpg):
        tab_row = tab_ref[pl.ds(g * hpg + h, 1), :]
        far = tab_row[:, FAR:FAR + 1]
        rel.append(jnp.concatenate([_bias_gather(tab_row, dist[:, c * LANE:(c + 1) * LANE]) - far
                                    for c in range(dist.shape[1] // LANE)], axis=1))
    return jnp.stack(rel)


def _online_softmax_update(qs, tiles):
    hpg, tq, _ = tiles[0][4].shape
    scores = [_dot_nt(qs, kb) for kb, *_ in tiles]
    probs, alphas = [], []
    for s, (kb, vb, mask, rel, m_ref, l_ref, acc_ref) in zip(scores, tiles):
        width = kb.shape[0]
        s = s.reshape(hpg, tq, width) * (HEAD_B ** -0.5)
        if rel is None:
            s = s + jnp.where(mask, 0.0, 2 * NEG)[None]
        else:
            s = s + jnp.where(mask[None], rel, 2 * NEG)
        m_prev = m_ref[...][:, :, :1]
        m_new = jnp.maximum(m_prev, jnp.max(s, axis=2, keepdims=True))
        alpha = jnp.exp(m_prev - m_new)
        p = jnp.exp(s - m_new)
        m_ref[...] = jnp.broadcast_to(m_new, m_ref.shape)
        probs.append(p.reshape(hpg * tq, width).astype(BF16))
        alphas.append(alpha)
    pvs = [_dot(p, jnp.concatenate([t[1], jnp.ones_like(t[1])], axis=1)) for p, t in zip(probs, tiles)]
    for pv, alpha, t in zip(pvs, alphas, tiles):
        pv = pv.reshape(hpg, tq, 2 * HEAD_B)
        t[5][...] = alpha * t[5][...] + pv[:, :, HEAD_B:]
        t[6][...] = alpha * t[6][...] + pv[:, :, :HEAD_B]


def _cmp_attn_kernel(q_ref, kc_ref, vc_ref, ovt_ref, tab_ref, o_ref, imp_ref, *, tq, q0, lanes):
    g = pl.program_id(1)
    qt = pl.program_id(2)
    hpg = q_ref.shape[2] // HEAD_B
    rr = kc_ref.shape[1]
    kc = kc_ref[0].astype(BF16)
    vc = vc_ref[0].astype(BF16)
    qbase = q0 + qt * tq
    qrow = qbase + lax.broadcasted_iota(jnp.int32, (tq, LANE), 0)
    dists = []
    for cix in range(rr // LANE):
        c_end = S_CMP * (cix * LANE + lax.broadcasted_iota(jnp.int32, (tq, LANE), 1)) + (L_CMP - 1)
        dists.append(qrow - c_end)
    mask = (jnp.concatenate(dists, axis=1) >= 0)[None]
    bias = jnp.stack([jnp.concatenate([_bias_gather(tab_ref[pl.ds(g * hpg + h, 1), :], d) for d in dists], axis=1)
                      for h in range(hpg)])
    s = _dot_nt(_stack_heads(q_ref.at[0]), kc).reshape(hpg, tq, rr) * (HEAD_B ** -0.5) + bias
    s = jnp.where(mask, s, NEG)
    e = jnp.where(mask, jnp.exp(s - jnp.max(s, axis=2, keepdims=True)), 0.0)
    p = e / jnp.maximum(jnp.sum(e, axis=2, keepdims=True), 1e-30)
    o = _dot(p.reshape(hpg * tq, rr).astype(BF16), vc).reshape(hpg, tq, HEAD_B)
    o_ref[0] = jnp.concatenate([o[h] for h in range(hpg)], axis=1)
    psum = jnp.sum(p, axis=0)
    if tq < lanes:
        psum = jnp.concatenate([psum, jnp.zeros((lanes - tq, rr), F32)], axis=0)
    ph, plo = _split(psum)
    ovt = ovt_ref[...]
    imp_ref[0, 0] = _dot_nt(ovt, ph) + _dot_nt(ovt, plo)


def _select_kernel(imp_ref, qpos_ref, selt_ref, sc_ref, *, ns, n_top):
    nsr, lanes = sc_ref.shape
    j = lax.broadcasted_iota(jnp.int32, (nsr, lanes), 0)
    cur = qpos_ref[...] // L_SEL
    valid = j <= cur
    forced = (j == 0) | (j == cur) | (j == cur - 1)
    score = jnp.where(valid, imp_ref[0, 0] + jnp.where(forced, FORCE_BONUS, 0.0), NEG)
    sc_ref[...] = score

    def body(jp, rank):
        rowv = sc_ref[pl.ds(jp, 1), :]
        before = (rowv > score) | ((rowv == score) & (jp < j))
        return rank + jnp.where(before, 1.0, 0.0)

    rank = lax.fori_loop(0, ns, body, jnp.zeros((nsr, lanes), F32))
    selt_ref[0, 0] = jnp.where(valid & (rank < n_top), 1.0, 0.0)


def _select(imp, qpos, ns):
    x, gg, nsr, l = imp.shape
    lt = next(c for c in (512, 256, 128) if l % c == 0)
    blk = pl.BlockSpec((1, 1, nsr, lt), lambda i, g, t: (i, g, 0, t))
    return pl.pallas_call(
        functools.partial(_select_kernel, ns=ns, n_top=min(TOP_N, ns)), grid=(x, gg, l // lt),
        in_specs=[blk, pl.BlockSpec((1, lt), lambda i, g, t: (0, t))], out_specs=blk,
        out_shape=jax.ShapeDtypeStruct(imp.shape, F32),
        scratch_shapes=[pltpu.VMEM((nsr, lt), F32)], compiler_params=_cp(3), name="nsa_select",
    )(imp, qpos)


def _cmp_attn(q, kcv, tab, *, t_keys, q0, tq):
    b, tqa, d_b = q.shape
    rr = kcv.shape[1]
    gw = d_b // N_KV
    ns = -(-t_keys // L_SEL)
    nsr = _round_up(ns, SUBLANE)
    lanes = max(tq, LANE)
    nqt = tqa // tq
    ci = np.arange(rr)
    sj = np.arange(nsr)
    ov = ((S_CMP * ci) // L_SEL)[None, :] == sj[:, None]
    ov |= ((S_CMP * ci + L_CMP - 1) // L_SEL)[None, :] == sj[:, None]
    ovt = jnp.asarray(ov.astype(np.float32), BF16)
    kern = functools.partial(_cmp_attn_kernel, tq=tq, q0=q0, lanes=lanes)
    return pl.pallas_call(
        kern, grid=(b, N_KV, nqt),
        in_specs=[pl.BlockSpec((1, tq, gw), lambda i, g, t: (i, t, g)),
                  pl.BlockSpec((1, rr, LANE), lambda i, g, t: (i * 4 + g, 0, 0)),
                  pl.BlockSpec((1, rr, LANE), lambda i, g, t: (i * 4 + N_KV + g, 0, 0)),
                  pl.BlockSpec(ovt.shape, lambda i, g, t: (0, 0)),
                  pl.BlockSpec(tab.shape, lambda i, g, t: (0, 0))],
        out_specs=[pl.BlockSpec((1, tq, gw), lambda i, g, t: (i, t, g)),
                   pl.BlockSpec((1, 1, nsr, lanes), lambda i, g, t: (i, g, 0, t))],
        out_shape=[jax.ShapeDtypeStruct((b, tqa, d_b), F32),
                   jax.ShapeDtypeStruct((b, N_KV, nsr, nqt * lanes), F32)],
        compiler_params=_cp(3), name="nsa_cmp_attn",
    )(q, kcv, kcv, ovt, tab)


def _flash_kernel(*refs, mode, band, tq, tk, nj, q0, kbase, n_tail, tail_pos0):
    refs = list(refs)
    q_ref, k_ref, v_ref = refs[:3]
    pos = 3
    if n_tail:
        kt_ref, vt_ref = refs[pos:pos + 2]
        pos += 2
    if mode == "sel":
        sel_ref = refs[pos]
        pos += 1
    tab_ref, o_ref, m_ref, l_ref, acc_ref = refs[pos:pos + 5]
    g = pl.program_id(1)
    qt = pl.program_id(2)
    j = pl.program_id(3)
    hpg = q_ref.shape[1] // HEAD_B
    qbase = q0 + qt * tq
    qs = _stack_heads(q_ref)

    @pl.when(j == 0)
    def _():
        m_ref[...] = jnp.full_like(m_ref, NEG)
        l_ref[...] = jnp.zeros_like(l_ref)
        acc_ref[...] = jnp.zeros_like(acc_ref)

    def update(k_r, v_r, kpos0, width, n_valid, near):
        qrow = qbase + lax.broadcasted_iota(jnp.int32, (tq, width), 0)
        kcol = kpos0 + lax.broadcasted_iota(jnp.int32, (tq, width), 1)
        dist = qrow - kcol
        mask = dist >= 0
        if n_valid is not None:
            mask &= lax.broadcasted_iota(jnp.int32, (tq, width), 1) < n_valid
        if mode == "win":
            mask &= dist < WINDOW
        else:
            nsp = sel_ref.shape[1]
            blk = (kpos0 + lax.broadcasted_iota(jnp.int32, (nsp, width), 1)) // L_SEL
            expand = (blk == lax.broadcasted_iota(jnp.int32, (nsp, width), 0)).astype(BF16)
            mask &= _dot(sel_ref[...].astype(BF16), expand) > 0.5
        kb = k_r[...].astype(BF16)
        vb = v_r[...].astype(BF16)
        rel = _rel_bias(dist, g, hpg, tab_ref) if near else None
        _online_softmax_update(qs, [(kb, vb, mask, rel, m_ref, l_ref, acc_ref)])

    if band:
        kt_abs = qt - (nj - 1) + j
        active = kt_abs >= 0
    else:
        kt_abs = j
        active = kbase + j * tk <= qbase + tq - 1
    kpos0 = kbase + kt_abs * tk
    is_far = qbase - (kpos0 + tk - 1) >= FAR

    @pl.when(active & is_far)
    def _():
        update(k_ref, v_ref, kpos0, tk, None, False)

    @pl.when(active & jnp.logical_not(is_far))
    def _():
        update(k_ref, v_ref, kpos0, tk, None, True)

    @pl.when(j == nj - 1)
    def _():
        if n_tail:
            update(kt_ref, vt_ref, tail_pos0, kt_ref.shape[0], n_tail, True)
        outs = [acc_ref[h] / jnp.maximum(l_ref[h][:, :1], 1e-30) for h in range(hpg)]
        o_ref[...] = jnp.concatenate(outs, axis=1)


def _flash(q, k_arr, k_spec, v_arr, v_spec, tab, *, mode, band, tq, tk, nj, q0, kbase,
           tail=None, sel=None, name):
    b, tqa, d_b = q.shape
    gw = d_b // N_KV
    hpg = gw // HEAD_B
    nqt = tqa // tq
    args = [q, k_arr, v_arr]
    in_specs = [pl.BlockSpec((None, tq, gw), lambda i, g, t, j: (i, t, g)), k_spec, v_spec]
    n_tail, tail_pos0 = 0, 0
    if tail is not None:
        tk_arr, tk_spec, tv_arr, tv_spec, n_tail, tail_pos0 = tail
        args += [tk_arr, tv_arr]
        in_specs += [tk_spec, tv_spec]
    if mode == "sel":
        args.append(sel)
        in_specs.append(pl.BlockSpec((None, None, tq, sel.shape[3]), lambda i, g, t, j: (i, g, t, 0)))
    args.append(tab)
    in_specs.append(pl.BlockSpec(tab.shape, lambda i, g, t, j: (0, 0)))
    kern = functools.partial(_flash_kernel, mode=mode, band=band, tq=tq, tk=tk, nj=nj, q0=q0, kbase=kbase,
                             n_tail=n_tail, tail_pos0=tail_pos0)
    return pl.pallas_call(
        kern, grid=(b, N_KV, nqt, nj), in_specs=in_specs,
        out_specs=pl.BlockSpec((None, tq, gw), lambda i, g, t, j: (i, t, g)),
        out_shape=jax.ShapeDtypeStruct((b, tqa, d_b), F32),
        scratch_shapes=[pltpu.VMEM((hpg, tq, LANE), F32), pltpu.VMEM((hpg, tq, LANE), F32),
                        pltpu.VMEM((hpg, tq, HEAD_B), F32)],
        compiler_params=_cp(4), name=name,
    )(*args)


def _attn_res_kernel(*refs, mode, tq, ch, far_w):
    if mode == "sel":
        q_ref, k_ref, v_ref, sel_ref, tab_ref, o_ref, m_ref, l_ref, acc_ref = refs
    else:
        q_ref, k_ref, v_ref, tab_ref, o_ref, m_ref, l_ref, acc_ref = refs
    g = pl.program_id(1)
    qt = pl.program_id(2)
    hpg = q_ref.shape[1] // HEAD_B
    qbase = qt * tq
    near0 = qbase - LANE
    qs = _stack_heads(q_ref)
    m_ref[...] = jnp.full_like(m_ref, NEG)
    l_ref[...] = jnp.zeros_like(l_ref)
    acc_ref[...] = jnp.zeros_like(acc_ref)

    def tile(stream, kpos0, width, near):
        qrow = qbase + lax.broadcasted_iota(jnp.int32, (tq, width), 0)
        kcol = kpos0 + lax.broadcasted_iota(jnp.int32, (tq, width), 1)
        dist = qrow - kcol
        mask = (dist >= 0) if near else (kcol < near0)
        if mode == "win":
            mask &= dist < WINDOW
        else:
            nsp = sel_ref.shape[1]
            blk = (kpos0 + lax.broadcasted_iota(jnp.int32, (nsp, width), 1)) // L_SEL
            expand = (blk == lax.broadcasted_iota(jnp.int32, (nsp, width), 0)).astype(BF16)
            mask &= _dot(sel_ref[...].astype(BF16), expand) > 0.5
        kb = k_ref[pl.ds(kpos0, width), :].astype(BF16)
        vb = v_ref[pl.ds(kpos0, width), :].astype(BF16)
        rel = _rel_bias(dist, g, hpg, tab_ref) if near else None
        return (kb, vb, mask, rel, m_ref.at[stream], l_ref.at[stream], acc_ref.at[stream])

    near = lambda: tile(1, pl.multiple_of(jnp.maximum(near0, 0), LANE), 2 * LANE, True)
    if mode == "sel":
        def body(c, carry):
            _online_softmax_update(qs, [tile(0, pl.multiple_of(c * ch, ch), ch, False)])
            return carry

        lax.fori_loop(0, (jnp.maximum(near0, 0) + ch - 1) // ch, body, 0)
        _online_softmax_update(qs, [near()])
    elif far_w:
        far = tile(0, pl.multiple_of(jnp.maximum(qbase - WINDOW, 0), LANE), far_w, False)
        _online_softmax_update(qs, [far, near()])
    else:
        _online_softmax_update(qs, [near()])
    m0, m1 = m_ref[0][:, :, :1], m_ref[1][:, :, :1]
    m = jnp.maximum(m0, m1)
    w0, w1 = jnp.exp(m0 - m), jnp.exp(m1 - m)
    l = w0 * l_ref[0][:, :, :1] + w1 * l_ref[1][:, :, :1]
    out = (w0 * acc_ref[0] + w1 * acc_ref[1]) / jnp.maximum(l, 1e-30)
    o_ref[...] = jnp.concatenate([out[h] for h in range(hpg)], axis=1)


def _attn_res(q, kv, kcol, vcol, tab, *, mode, sel=None, name):
    b, t, d_b = q.shape
    gw = d_b // N_KV
    hpg = gw // HEAD_B
    tq = LANE
    assert t % tq == 0 and t >= 2 * LANE
    ch = next(c for c in (1024, 512, 256, 128) if t % c == 0)
    far_w = min(WINDOW - LANE, t - 2 * LANE)
    args = [q, kv, kv]
    in_specs = [pl.BlockSpec((None, tq, gw), lambda i, g, qt: (i, qt, g)),
                pl.BlockSpec((None, t, LANE), lambda i, g, qt: (i, 0, kcol + g)),
                pl.BlockSpec((None, t, LANE), lambda i, g, qt: (i, 0, vcol + g))]
    if mode == "sel":
        args.append(sel)
        in_specs.append(pl.BlockSpec((None, None, tq, sel.shape[3]), lambda i, g, qt: (i, g, qt, 0)))
    args.append(tab)
    in_specs.append(pl.BlockSpec(tab.shape, lambda i, g, qt: (0, 0)))
    return pl.pallas_call(
        functools.partial(_attn_res_kernel, mode=mode, tq=tq, ch=ch, far_w=far_w),
        grid=(b, N_KV, t // tq), in_specs=in_specs,
        out_specs=pl.BlockSpec((None, tq, gw), lambda i, g, qt: (i, qt, g)),
        out_shape=jax.ShapeDtypeStruct((b, t, d_b), F32),
        scratch_shapes=[pltpu.VMEM((2, hpg, tq, LANE), F32), pltpu.VMEM((2, hpg, tq, LANE), F32),
                        pltpu.VMEM((2, hpg, tq, HEAD_B), F32)],
        compiler_params=_cp(3), name=name,
    )(*args)


def _nsa_out_kernel(oc_ref, os_ref, ow_ref, gl_ref, e_ref, w_ref, y_ref):
    gate = _sigmoid(gl_ref[...])
    o = (_dot_x(gate, e_ref[0]) * oc_ref[...] + _dot_x(gate, e_ref[1]) * os_ref[...]
         + _dot_x(gate, e_ref[2]) * ow_ref[...])
    y_ref[...] = _dot(o.astype(BF16), w_ref[...])


def _nsa_out(o_c, o_s, o_w, gl, w_ob):
    m, d_b = o_c.shape
    n = w_ob.shape[1]
    h_b = d_b // HEAD_B
    tm = _tile(m, 256)
    e = np.zeros((3, LANE, d_b), np.float32)
    for br in range(3):
        for hh in range(h_b):
            e[br, br * h_b + hh, hh * HEAD_B:(hh + 1) * HEAD_B] = 1.0
    e = jnp.asarray(e, BF16)
    row = pl.BlockSpec((tm, d_b), lambda i: (i, 0))
    return pl.pallas_call(
        _nsa_out_kernel, grid=(m // tm,),
        in_specs=[row, row, row, pl.BlockSpec((tm, LANE), lambda i: (i, 0)),
                  pl.BlockSpec(e.shape, lambda i: (0, 0, 0)), pl.BlockSpec(w_ob.shape, lambda i: (0, 0))],
        out_specs=pl.BlockSpec((tm, n), lambda i: (i, 0)), out_shape=jax.ShapeDtypeStruct((m, n), F32),
        compiler_params=_cp(1), name="nsa_out",
    )(o_c, o_s, o_w, gl, e, w_ob)


def _nsa(q, rows, win, gl, tab, lw, *, t_real, past8, lidx, cwin):
    b, tqa, d_b = q.shape
    t = t_real
    ncmp = 2 * N_KV
    cw = S_CMP * LANE
    p_len = 0 if past8 is None else past8.shape[3]
    l_tot = p_len + t
    assert p_len % (S_CMP * SUBLANE) == 0 and (p_len == 0 or t <= S_CMP) and (p_len > 0 or t % LANE == 0)
    if past8 is None:
        x_arr, n_pos = rows, t
        x_spec = pl.BlockSpec((None, t, LANE), lambda i: (i // ncmp, 0, i % ncmp))
        ex0 = jnp.zeros((b * ncmp, 1, cw), F32)
    else:
        x_arr, n_pos = past8, p_len
        x_spec = pl.BlockSpec((None, None, None, p_len, LANE), lambda i: (lidx, i // ncmp, i % ncmp, 0, 0))
        new_cmp = rows[:, :, :ncmp * LANE].reshape(b, t, ncmp, LANE).transpose(0, 2, 1, 3)
        ex0 = jnp.pad(new_cmp.reshape(b * ncmp, 1, t * LANE), ((0, 0), (0, 0), (0, cw - t * LANE)))
    pe = jnp.repeat(lw["cmp_pe"].reshape(2, 1, 2, cw), N_KV, axis=1)
    pe = jnp.broadcast_to(pe[None], (b, 2, N_KV, 2, cw)).reshape(b * ncmp, 2, cw)
    ex = jnp.concatenate([ex0, pe, jnp.zeros((b * ncmp, SUBLANE - 3, cw), F32)], axis=1)
    kcv = _compress(x_arr, x_spec, n_pos, ex, lw["w1cat"], lw["w2"])
    if kcv.shape[1] % LANE:
        kcv = jnp.pad(kcv, ((0, 0), (0, _round_up(kcv.shape[1], LANE) - kcv.shape[1]), (0, 0)))
    tq = tqa if p_len else LANE
    o_c, imp = _cmp_attn(q, kcv, tab, t_keys=l_tot, q0=p_len, tq=tq)
    ns = -(-l_tot // L_SEL)
    nsp = _round_up(ns, LANE)
    if p_len == 0:
        sel = jnp.swapaxes(_select(imp, jnp.arange(tqa, dtype=jnp.int32)[None], ns), 2, 3)
    else:
        nq = b * tqa
        nql = _round_up(nq, LANE)
        nsr = imp.shape[2]
        impl = jnp.pad(imp[..., :tqa].transpose(1, 2, 0, 3).reshape(1, N_KV, nsr, nq),
                       ((0, 0), (0, 0), (0, 0), (0, nql - nq)))
        qpos = (p_len + jnp.arange(nql, dtype=jnp.int32) % tqa)[None]
        sel = _select(impl, qpos, ns)[0, :, :, :nq].reshape(N_KV, nsr, b, tqa).transpose(2, 0, 3, 1)
    sel = jnp.pad(sel, ((0, 0), (0, 0), (0, 0), (0, nsp - sel.shape[3])))
    if past8 is None:
        o_s = _attn_res(q, rows, 2 * N_KV, 3 * N_KV, tab, mode="sel", sel=sel, name="nsa_sel_attn")
        o_w = _attn_res(q, win, 0, N_KV, tab, mode="win", name="nsa_win_attn")
    else:
        tk = next(c for c in (2048, 1024, 512, 256, 128) if p_len % c == 0)
        rows_t = jnp.pad(rows, ((0, 0), (0, LANE - t), (0, 0)))
        win_t = jnp.pad(win, ((0, 0), (0, LANE - t), (0, 0)))
        tails = lambda col: pl.BlockSpec((None, LANE, LANE), lambda i, g, qt, j, col=col: (i, 0, col + g))
        ks = lambda s0: pl.BlockSpec((None, None, None, tk, LANE),
                                     lambda i, g, qt, j, s0=s0: (lidx, i, s0 + g, j, 0))
        o_s = _flash(q, past8, ks(2 * N_KV), past8, ks(3 * N_KV), tab, mode="sel", band=False, tq=tq, tk=tk,
                     nj=p_len // tk, q0=p_len, kbase=0, sel=sel,
                     tail=(rows_t, tails(2 * N_KV), rows_t, tails(3 * N_KV), t, p_len), name="nsa_sel_attn_paged")
        wb = cwin.shape[1]
        kw = lambda col: pl.BlockSpec((None, wb, LANE), lambda i, g, qt, j, col=col: (i, 0, col + g))
        o_w = _flash(q, cwin, kw(0), cwin, kw(N_KV), tab, mode="win", band=False, tq=tq, tk=wb,
                     nj=1, q0=p_len, kbase=p_len - wb,
                     tail=(win_t, tails(0), win_t, tails(N_KV), t, p_len), name="nsa_win_attn_cached")
    m = b * tqa
    return _nsa_out(o_c.reshape(m, d_b), o_s.reshape(m, d_b), o_w.reshape(m, d_b), gl, lw["w_ob"])


def _layer_weights(l, w_in, mu, w0, w_up, a0, a_up, g_up, k_k, k_a, r_k, lnx_w, lnx_b, w_oa,
                   cmp_pe, cmp_w1, cmp_w2, w_ob, w_o, w_ff_up, w_ff_down, norm_g):
    d_a = w_oa.shape[1]
    d_b = w_ob.shape[1]
    d = w_o.shape[1]
    rw = 3 * d_a + DECAY_LORA + A_LORA + GATE_LORA
    pw = 3 * d_a + LORA_W
    nrow = 4 * N_KV * HEAD_B
    nwin = 2 * N_KV * HEAD_B
    ngl = 3 * (d_b // HEAD_B)
    wi = w_in[l]
    o_q = rw
    o_rows = o_q + d_b
    o_win = o_rows + nrow
    o_gl = o_win + nwin
    o_pg = o_gl + ngl
    wl = jnp.zeros((LORA_W, 3 * d_a), F32)
    wl = wl.at[:DECAY_LORA, :d_a].set(w_up[l])
    wl = wl.at[DECAY_LORA:DECAY_LORA + A_LORA, d_a:2 * d_a].set(a_up[l])
    wl = wl.at[DECAY_LORA + A_LORA:DECAY_LORA + A_LORA + GATE_LORA, 2 * d_a:].set(g_up[l])
    half = S_CMP * HEAD_B
    return dict(
        w_pa=jnp.pad(wi[:, :rw], ((0, 0), (0, pw - rw))).astype(BF16),
        w_q=wi[:, o_q:o_rows].astype(BF16),
        w_rows=wi[:, o_rows:o_win].astype(BF16),
        w_win=wi[:, o_win:o_gl].astype(BF16),
        w_gl=jnp.pad(wi[:, o_gl:o_pg], ((0, 0), (0, LANE - ngl))).astype(BF16),
        w_pg=wi[:, o_pg:o_pg + 2 * d].astype(BF16),
        mu=jnp.pad(mu[l], (0, pw - rw))[None], wl=wl.astype(BF16),
        w0=w0[l][None], a0=a0[l][None], k_k=k_k[l][None], k_a=k_a[l][None],
        r_k=r_k[l].reshape(1, d_a), lnx_w=lnx_w[l][None], lnx_b=lnx_b[l][None],
        w_oa=w_oa[l].astype(BF16), w_ob=w_ob[l].astype(BF16), w_o=w_o[l].astype(BF16),
        w_up=w_ff_up[l].astype(BF16), w_down=w_ff_down[l].astype(BF16),
        cmp_pe=cmp_pe[l],
        w1cat=jnp.concatenate([cmp_w1[l][:, :half], cmp_w1[l][:, half:]], axis=2).astype(BF16),
        w2=cmp_w2[l].astype(BF16), g=norm_g[l], rw=rw,
    )


def _layer(x, lw, tab, shift0, s0, past8, lidx, cwin):
    b, t, d = x.shape
    m = b * t
    d_a = lw["w_oa"].shape[0]
    x2 = x.reshape(m, d)
    g = lw["g"]
    xn = _rmsnorm_cast(x2, g[0:1])
    pa = _matmul(xn, lw["w_pa"], F32, name="proj_rwkv")
    q = _matmul(xn, lw["w_q"], BF16, name="proj_q")
    rows = _matmul(xn, lw["w_rows"], F32, name="proj_rows")
    win = _matmul(xn, lw["w_win"], F32, name="proj_win")
    gl = _matmul(xn, lw["w_gl"], F32, name="proj_gl")
    pg = _matmul(xn, lw["w_pg"], F32, name="proj_pg")
    pw = pa.shape[1]
    d_b = q.shape[1]
    tp = t if t % RWKV_CHUNK == 0 else _round_up(t, SUBLANE)
    c = RWKV_CHUNK if t % RWKV_CHUNK == 0 else tp
    pa3 = pa.reshape(b, t, pw)
    pa_p = pa3 if tp == t else jnp.pad(pa3, ((0, 0), (0, tp - t), (0, 0)))
    prev = jnp.pad(shift0, ((0, 0), (0, pw - shift0.shape[1])))[:, None]
    r, lgw, k2, v, kk, a, gg = _rwkv_prep(pa_p, prev, lw["mu"], lw["wl"], lw["w0"], lw["a0"], lw["k_k"], lw["k_a"],
                                          None if tp == t else t)
    o, s_fin = _rwkv_scan(r, lgw, k2, v, kk, a, s0, c)
    flat = lambda u: u.reshape(b * tp, d_a)
    ya = _rwkv_post(flat(o), flat(r), flat(k2), flat(v), flat(gg), lw["lnx_w"], lw["lnx_b"], lw["r_k"], lw["w_oa"])
    if tp != t:
        ya = ya.reshape(b, tp, d)[:, :t].reshape(m, d)
    sh = pa3[:, t - 1, :lw["rw"]]
    tqa = t if t % LANE == 0 else _round_up(t, SUBLANE)
    q3 = q.reshape(b, t, d_b)
    gl_p = gl
    if tqa != t:
        q3 = jnp.pad(q3, ((0, 0), (0, tqa - t), (0, 0)))
        gl_p = jnp.pad(gl.reshape(b, t, LANE), ((0, 0), (0, tqa - t), (0, 0))).reshape(b * tqa, LANE)
    rows3 = rows.reshape(b, t, rows.shape[1])
    win3 = win.reshape(b, t, win.shape[1])
    yb = _nsa(q3, rows3, win3, gl_p, tab, lw, t_real=t, past8=past8, lidx=lidx, cwin=cwin)
    if tqa != t:
        yb = yb.reshape(b, tqa, d)[:, :t].reshape(m, d)
    x1 = _merge(ya, yb, pg, x2, lw["w_o"], g[1:2])
    h = _matmul(_rmsnorm_cast(x1, g[2:3]), lw["w_up"], BF16, relu2=True, name="ffn_up")
    x_out = _ffn_down(h, lw["w_down"], x1, g[3:4])
    wctx = win3 if cwin is None else jnp.concatenate([cwin, win3], axis=1)
    n_keep = min(WINDOW, wctx.shape[1])
    return x_out.reshape(b, t, d), rows3, wctx[:, wctx.shape[1] - n_keep:], sh, s_fin


def _pair_states(s):
    b, h, n, _ = s.shape
    s = s.reshape(b, h // 2, 2, n, n)
    z = jnp.zeros((b, h // 2, n, n), s.dtype)
    top = jnp.concatenate([s[:, :, 0], z], axis=3)
    bot = jnp.concatenate([z, s[:, :, 1]], axis=3)
    return jnp.concatenate([top, bot], axis=2)


def _unpair_states(s):
    n = HEAD_A
    b, hp = s.shape[:2]
    return jnp.stack([s[:, :, :n, :n], s[:, :, n:, n:]], axis=2).reshape(b, 2 * hp, n, n)


def _trunk(x, shift0, wkv0, past8, cache_win, tab, layers):
    rows, wins, shifts, wkvs = [], [], [], []
    for l, lw in enumerate(layers):
        x, nr, nw, sh, st = _layer(x, lw, tab, shift0[l], _pair_states(wkv0[l]),
                                   past8, l, None if cache_win is None else cache_win[l])
        b, t = nr.shape[:2]
        rows.append(nr.reshape(b, t, 4, N_KV, HEAD_B))
        wins.append(nw.reshape(b, nw.shape[1], 2, N_KV, HEAD_B))
        shifts.append(sh)
        wkvs.append(_unpair_states(st))
    return x, jnp.stack(rows), jnp.stack(wins), jnp.stack(shifts), jnp.stack(wkvs)


def kernel(x_prompt, x_sample, cache_kv, cache_win, state_shift, state_wkv, page_table, w_in, mu, w0, w_up, a0, a_up, g_up, k_k, k_a, r_k, lnx_w, lnx_b, w_oa, cmp_pe, cmp_w1, cmp_w2, w_ob, w_o, w_ff_up, w_ff_down, norm_g, rel_bias):
    depth = w_in.shape[0]
    layers = [_layer_weights(l, w_in, mu, w0, w_up, a0, a_up, g_up, k_k, k_a, r_k, lnx_w, lnx_b, w_oa,
                             cmp_pe, cmp_w1, cmp_w2, w_ob, w_o, w_ff_up, w_ff_down, norm_g) for l in range(depth)]
    tab = rel_bias[_t5_bucket_table()].T
    bp = x_prompt.shape[0]
    rw = state_shift.shape[2]
    zeros_shift = jnp.zeros((depth, bp, rw), x_prompt.dtype)
    zeros_wkv = jnp.zeros((depth, bp) + state_wkv.shape[2:], state_wkv.dtype)
    y_p, kv_p, win_p, sh_p, wkv_p = _trunk(x_prompt, zeros_shift, zeros_wkv, None, None, tab, layers)
    nl, n_pool, page = cache_kv.shape[:3]
    ns = 4 * N_KV
    past8 = _page_gather(cache_kv.reshape(nl, n_pool, page * ns, HEAD_B), page_table, ns, page)
    cwin = cache_win.reshape(cache_win.shape[:3] + (-1,))
    y_s, kv_s, win_s, sh_s, wkv_s = _trunk(x_sample, state_shift, state_wkv, past8, cwin, tab, layers)
    return (y_p, y_s, kv_p, kv_s, win_p, win_s, sh_p, sh_s, wkv_p, wkv_s)
```

```python
import functools
import math

import numpy as np
import jax
import jax.numpy as jnp
from jax import lax
from jax.experimental import pallas as pl
from jax.experimental.pallas import tpu as pltpu

F32 = jnp.float32
BF16 = jnp.bfloat16

HEAD_A = 64
DECAY_LORA = 64
A_LORA = 64
GATE_LORA = 160
LNX_EPS = 64e-5
N_KV = 2
HEAD_B = 128
L_CMP = 32
S_CMP = 16
L_SEL = 64
TOP_N = 16
WINDOW = 512
FORCE_BONUS = 1e4
NEG = -1e30
NUM_BUCKETS = 32
MAX_DIST = 128
EPS = 1e-6
PAGE_SIZE = 128

LANE = 128
SUBLANE = 8
VMEM_LIMIT = 56 * 1024 * 1024
RWKV_CHUNK = 64
LORA_W = 384
FAR = LANE - 1


def _cp(n_axes):
    return pltpu.CompilerParams(dimension_semantics=("arbitrary",) * n_axes,
                                vmem_limit_bytes=VMEM_LIMIT)


def _round_up(x, m):
    return -(-x // m) * m


def _tile(n, pref, mult=SUBLANE):
    if n <= pref:
        return n
    for t in range(pref - pref % mult, 0, -mult):
        if n % t == 0:
            return t
    return n


def _dot(a, b):
    return jnp.dot(a, b, preferred_element_type=F32)


def _dot_nt(a, b):
    return lax.dot_general(a, b, (((1,), (1,)), ((), ())), preferred_element_type=F32)


def _dot_tn(a, b):
    return lax.dot_general(a, b, (((0,), (0,)), ((), ())), preferred_element_type=F32)


def _split(x):
    hi = x.astype(BF16)
    lo = (x - hi.astype(F32)).astype(BF16)
    return hi, lo


def _dot_x(a, b):
    hi, lo = _split(a)
    return _dot(hi, b) + _dot(lo, b)


def _dot3(a, b):
    ah, al = _split(a)
    bh, bl = _split(b)
    return _dot(ah, bh) + _dot(ah, bl) + _dot(al, bh)


def _dot3_nt(a, b):
    ah, al = _split(a)
    bh, bl = _split(b)
    return _dot_nt(ah, bh) + _dot_nt(ah, bl) + _dot_nt(al, bh)


def _dot3_tn(a, b):
    ah, al = _split(a)
    bh, bl = _split(b)
    return _dot_tn(ah, bh) + _dot_tn(ah, bl) + _dot_tn(al, bh)


def _sigmoid(x):
    return 1.0 / (1.0 + jnp.exp(-x))


def _rms(y, g):
    return y * lax.rsqrt(jnp.mean(y * y, axis=-1, keepdims=True) + EPS) * g


def _rmsnorm_kernel(x_ref, g_ref, o_ref):
    o_ref[...] = _rms(x_ref[...], g_ref[...]).astype(o_ref.dtype)


def _rmsnorm_cast(x, g):
    m, d = x.shape
    tm = _tile(m, 256)
    return pl.pallas_call(
        _rmsnorm_kernel, grid=(m // tm,),
        in_specs=[pl.BlockSpec((tm, d), lambda i: (i, 0)), pl.BlockSpec((1, d), lambda i: (0, 0))],
        out_specs=pl.BlockSpec((tm, d), lambda i: (i, 0)),
        out_shape=jax.ShapeDtypeStruct((m, d), BF16), compiler_params=_cp(1), name="rmsnorm_cast",
    )(x, g)


def _mm_kernel(a_ref, w_ref, o_ref, *, relu2):
    y = _dot(a_ref[...], w_ref[...])
    if relu2:
        y = jnp.square(jnp.maximum(y, 0.0))
    o_ref[...] = y.astype(o_ref.dtype)


def _matmul(a, w, out_dtype, relu2=False, name="matmul"):
    m, k = a.shape
    n = w.shape[1]
    tm = _tile(m, 1024)
    tn = _tile(n, 1280, LANE)
    return pl.pallas_call(
        functools.partial(_mm_kernel, relu2=relu2), grid=(m // tm, n // tn),
        in_specs=[pl.BlockSpec((tm, k), lambda i, j: (i, 0)), pl.BlockSpec((k, tn), lambda i, j: (0, j))],
        out_specs=pl.BlockSpec((tm, tn), lambda i, j: (i, j)),
        out_shape=jax.ShapeDtypeStruct((m, n), out_dtype), compiler_params=_cp(2), name=name,
    )(a, w)


def _ffn_down_kernel(h_ref, w_ref, x_ref, g_ref, o_ref, acc_ref):
    k = pl.program_id(1)

    @pl.when(k == 0)
    def _():
        acc_ref[...] = jnp.zeros_like(acc_ref)

    acc_ref[...] += _dot(h_ref[...], w_ref[...])

    @pl.when(k == pl.num_programs(1) - 1)
    def _():
        o_ref[...] = x_ref[...] + _rms(acc_ref[...], g_ref[...])


def _ffn_down(h, w, x, g):
    m, kdim = h.shape
    n = w.shape[1]
    tm = _tile(m, 512)
    tk = _tile(kdim, 2048, LANE)
    return pl.pallas_call(
        _ffn_down_kernel, grid=(m // tm, kdim // tk),
        in_specs=[pl.BlockSpec((tm, tk), lambda i, k: (i, k)), pl.BlockSpec((tk, n), lambda i, k: (k, 0)),
                  pl.BlockSpec((tm, n), lambda i, k: (i, 0)), pl.BlockSpec((1, n), lambda i, k: (0, 0))],
        out_specs=pl.BlockSpec((tm, n), lambda i, k: (i, 0)),
        out_shape=jax.ShapeDtypeStruct((m, n), F32),
        scratch_shapes=[pltpu.VMEM((tm, n), F32)], compiler_params=_cp(2), name="ffn_down",
    )(h, w, x, g)


def _merge_kernel(ya_ref, yb_ref, pg_ref, x_ref, w_ref, g_ref, o_ref):
    d = ya_ref.shape[1]
    pg = pg_ref[...]
    mix = _sigmoid(pg[:, :d]) * ya_ref[...] + _sigmoid(pg[:, d:]) * yb_ref[...]
    y = _dot(mix.astype(BF16), w_ref[...])
    o_ref[...] = x_ref[...] + _rms(y, g_ref[...])


def _merge(ya, yb, pg, x, w_o, g):
    m, d = x.shape
    tm = _tile(m, 256)
    row = lambda c: pl.BlockSpec((tm, c), lambda i: (i, 0))
    return pl.pallas_call(
        _merge_kernel, grid=(m // tm,),
        in_specs=[row(d), row(d), row(2 * d), row(d),
                  pl.BlockSpec((d, d), lambda i: (0, 0)), pl.BlockSpec((1, d), lambda i: (0, 0))],
        out_specs=row(d), out_shape=jax.ShapeDtypeStruct((m, d), F32),
        compiler_params=_cp(1), name="merge",
    )(ya, yb, pg, x, w_o, g)


def _head_indicator(d_a):
    h = np.zeros((d_a, LANE), np.float32)
    h[np.arange(d_a), np.arange(d_a) // HEAD_A] = 1.0
    return jnp.asarray(h, BF16), jnp.asarray(h.T, BF16)


def _rwkv_prep_kernel(pa_ref, prev_ref, mu_ref, wl_ref, w0_ref, a0_ref, kk_ref, ka_ref, hd_ref, hdt_ref,
                      r_ref, lw_ref, k2_ref, v_ref, kkn_ref, a_ref, g_ref, carry_ref, *, d_a, t_real):
    t = pl.program_id(1)
    tt = pa_ref.shape[1]

    @pl.when(t == 0)
    def _():
        carry_ref[...] = prev_ref[0]

    x = pa_ref[0]
    row = lax.broadcasted_iota(jnp.int32, x.shape, 0)
    xprev = jnp.where(row == 0, carry_ref[...], pltpu.roll(x, 1, 0))
    carry_ref[...] = pa_ref[0, pl.ds(tt - 1, 1), :]
    xs = x + (xprev - x) * mu_ref[...]
    r = xs[:, :d_a]
    k = xs[:, d_a:2 * d_a]
    v = xs[:, 2 * d_a:3 * d_a]
    lo = xs[:, 3 * d_a:]
    lane = lax.broadcasted_iota(jnp.int32, lo.shape, 1)
    act = jnp.where(lane < DECAY_LORA, jnp.tanh(lo),
                    jnp.where(lane < DECAY_LORA + A_LORA, lo, _sigmoid(lo)))
    lin = _dot(act.astype(BF16), wl_ref[...])
    z = -(w0_ref[...] + lin[:, :d_a])
    w = -(jnp.maximum(z, 0.0) + jnp.log(1.0 + jnp.exp(-jnp.abs(z)))) - 0.5
    logw = -jnp.exp(w)
    a = _sigmoid(a0_ref[...] + lin[:, d_a:2 * d_a])
    g = lin[:, 2 * d_a:]
    kkr = k * kk_ref[...]
    ss = _dot_x(kkr * kkr, hd_ref[...])
    inv = 1.0 / jnp.maximum(jnp.sqrt(ss), 1e-12)
    kkn = kkr * _dot_x(inv, hdt_ref[...])
    k2 = k * (1.0 + (a - 1.0) * ka_ref[...])
    if t_real is not None:
        live = (t * tt + lax.broadcasted_iota(jnp.int32, r.shape, 0)) < t_real
        zero = jnp.zeros_like(r)
        r, logw, k2, v, kkn, a = (jnp.where(live, u, zero) for u in (r, logw, k2, v, kkn, a))
    r_ref[0] = r
    lw_ref[0] = logw
    k2_ref[0] = k2
    v_ref[0] = v
    kkn_ref[0] = kkn
    a_ref[0] = a
    g_ref[0] = g


def _rwkv_prep(pa, prev, mu, wl, w0, a0, k_k, k_a, t_real):
    b, t, p = pa.shape
    d_a = w0.shape[1]
    tt = _tile(t, 256)
    hd, hdt = _head_indicator(d_a)
    full = lambda a: pl.BlockSpec(a.shape, lambda i, j: (0,) * a.ndim)
    out = jax.ShapeDtypeStruct((b, t, d_a), F32)
    ospec = pl.BlockSpec((1, tt, d_a), lambda i, j: (i, j, 0))
    return pl.pallas_call(
        functools.partial(_rwkv_prep_kernel, d_a=d_a, t_real=t_real), grid=(b, t // tt),
        in_specs=[pl.BlockSpec((1, tt, p), lambda i, j: (i, j, 0)), pl.BlockSpec((1, 1, p), lambda i, j: (i, 0, 0)),
                  full(mu), full(wl), full(w0), full(a0), full(k_k), full(k_a), full(hd), full(hdt)],
        out_specs=[ospec] * 7, out_shape=[out] * 7,
        scratch_shapes=[pltpu.VMEM((1, p), F32)], compiler_params=_cp(2), name="rwkv_prep",
    )(pa, prev, mu, wl, w0, a0, k_k, k_a, hd, hdt)


def _mmb(a, b, form="nn"):
    dot = {"nn": _dot, "nt": _dot_nt, "tn": _dot_tn}[form]
    return dot(a.astype(BF16), b.astype(BF16))


def _rwkv_scan_kernel(r_ref, lw_ref, k2_ref, v_ref, kk_ref, a_ref, s0_ref, o_ref, sT_ref, s_ref, *, c):
    @pl.when(pl.program_id(1) == 0)
    def _():
        s_ref[...] = s0_ref[0]

    hk = HEAD_A
    pairs = range(s_ref.shape[0])
    sls = [slice(p * LANE, (p + 1) * LANE) for p in pairs]
    lane = lax.broadcasted_iota(jnp.int32, (c, LANE), 1)
    m0 = lane < hk
    rowi = lax.broadcasted_iota(jnp.int32, (c, c), 0)
    coli = lax.broadcasted_iota(jnp.int32, (c, c), 1)
    ltri = (coli <= rowi).astype(BF16)
    lw = [lw_ref[0, :, sl] for sl in sls]
    cum = [_dot_x_rhs(ltri, x) for x in lw]
    e_wi = [jnp.exp(-x) for x in cum]
    at = [-kk_ref[0, :, sl] * jnp.exp(x - y) for sl, x, y in zip(sls, cum, lw)]
    bt = [kk_ref[0, :, sl] * a_ref[0, :, sl] * e for sl, e in zip(sls, e_wi)]
    kt = [k2_ref[0, :, sl] * e for sl, e in zip(sls, e_wi)]
    rt = [r_ref[0, :, sl] * jnp.exp(x) for sl, x in zip(sls, cum)]
    v = [v_ref[0, :, sl] for sl in sls]
    w_c = [jnp.exp(x[c - 1:c, :]) for x in cum]

    def bd(z):
        zero = jnp.zeros_like(z)
        return jnp.concatenate([jnp.where(m0, z, zero), jnp.where(m0, zero, z)], axis=0)

    prow = lax.broadcasted_iota(jnp.int32, (c, 2 * c), 0)
    pcol = lax.broadcasted_iota(jnp.int32, (c, 2 * c), 1)
    pcol = jnp.where(pcol >= c, pcol - c, pcol)
    strict = pcol < prow
    incl = pcol <= prow
    zero_cc = jnp.zeros((c, 2 * c), F32)
    x2 = [jnp.concatenate([x, y], axis=0) for x, y in zip(at, rt)]
    xb = [_mmb(x, bd(y), "nt") for x, y in zip(x2, bt)]
    xk = [_mmb(x, bd(y), "nt") for x, y in zip(x2, kt)]
    a_ab = [jnp.where(strict, x[:c], zero_cc) for x in xb]
    a_rb = [jnp.where(incl, x[c:], zero_cc) for x in xb]
    a_ak = [jnp.where(strict, x[:c], zero_cc) for x in xk]
    a_rk = [jnp.where(incl, x[c:], zero_cc) for x in xk]

    if 2 * c == LANE:
        bdc = bd
    else:
        mc = lax.broadcasted_iota(jnp.int32, (c, 2 * c), 1) < c

        def bdc(z):
            zero = jnp.zeros_like(z)
            return jnp.concatenate([jnp.where(mc, z, zero), jnp.where(mc, zero, z)], axis=0)

    eye = (pcol == prow).astype(F32)
    tmat = [eye + x for x in a_ab]
    npow = a_ab
    steps = 1
    while 2 * steps < c:
        npow = [_dot3(x, bdc(x)) for x in npow]
        tmat = [x + _dot3(x, bdc(y)) for x, y in zip(tmat, npow)]
        steps *= 2
    ta = [_mmb(x, bd(y)) for x, y in zip(tmat, at)]
    xv = [_mmb(x, bd(y)) for x, y in zip(a_ak, v)]
    tx = [_mmb(x, bd(y)) for x, y in zip(tmat, xv)]
    p_c = [x + _mmb(y, bd(z)) for x, y, z in zip(rt, a_rb, ta)]
    q_c = [_mmb(x, bd(y)) + _mmb(z, bd(u)) for x, y, z, u in zip(a_rb, tx, a_rk, v)]
    lr = lax.broadcasted_iota(jnp.int32, (LANE, LANE), 0)
    lc = lax.broadcasted_iota(jnp.int32, (LANE, LANE), 1)
    same = (lr < hk) == (lc < hk)
    eye_l = (lr == lc).astype(F32)
    zero_l = jnp.zeros((LANE, LANE), F32)
    m_c = [(eye_l + jnp.where(same, _mmb(x, y, "tn"), zero_l)) * w for x, y, w in zip(ta, bt, w_c)]
    n_c = [jnp.where(same, _mmb(x, y, "tn") + _mmb(z, u, "tn"), zero_l) * w
           for x, y, z, u, w in zip(tx, bt, v, kt, w_c)]
    s = [s_ref[p] for p in pairs]
    o = [_mmb(x, y, "nt") + z for x, y, z in zip(p_c, s, q_c)]
    s_new = [_mmb(x, y) + z for x, y, z in zip(s, m_c, n_c)]
    for p in pairs:
        o_ref[0, :, sls[p]] = o[p]
        s_ref[p] = s_new[p]
        sT_ref[0, p] = s_new[p]


def _dot_x_rhs(a, b):
    hi, lo = _split(b)
    return _dot(a, hi) + _dot(a, lo)


def _rwkv_scan(r, lw, k2, v, kk, a, s0, c):
    b, t, d_a = r.shape
    npair = d_a // LANE
    seq = pl.BlockSpec((1, c, d_a), lambda i, j: (i, j, 0))
    st = pl.BlockSpec((1, npair, LANE, LANE), lambda i, j: (i, 0, 0, 0))
    return pl.pallas_call(
        functools.partial(_rwkv_scan_kernel, c=c), grid=(b, t // c),
        in_specs=[seq] * 6 + [st], out_specs=[seq, st],
        out_shape=[jax.ShapeDtypeStruct((b, t, d_a), F32), jax.ShapeDtypeStruct((b, npair, LANE, LANE), F32)],
        scratch_shapes=[pltpu.VMEM((npair, LANE, LANE), F32)], compiler_params=_cp(2), name="rwkv_scan",
    )(r, lw, k2, v, kk, a, s0)


def _rwkv_post_kernel(o_ref, r_ref, k2_ref, v_ref, g_ref, lw_ref, lb_ref, rk_ref, hd_ref, hdt_ref, w_ref, y_ref):
    o = o_ref[...]
    hd = hd_ref[...]
    hdt = hdt_ref[...]
    inv_n = 1.0 / HEAD_A
    mean = _dot_x(_dot_x(o, hd) * inv_n, hdt)
    d = o - mean
    var = _dot_x(d * d, hd) * inv_n
    xo = d * _dot_x(lax.rsqrt(var + LNX_EPS), hdt) * lw_ref[...] + lb_ref[...]
    bonus = _dot_x(_dot_x(r_ref[...] * k2_ref[...] * rk_ref[...], hd), hdt) * v_ref[...]
    y_ref[...] = _dot(((xo + bonus) * g_ref[...]).astype(BF16), w_ref[...])


def _rwkv_post(o, r, k2, v, g, lnx_w, lnx_b, r_k, w_oa):
    m, d_a = o.shape
    n = w_oa.shape[1]
    tm = _tile(m, 256)
    hd, hdt = _head_indicator(d_a)
    row = pl.BlockSpec((tm, d_a), lambda i: (i, 0))
    full = lambda a: pl.BlockSpec(a.shape, lambda i: (0,) * a.ndim)
    return pl.pallas_call(
        _rwkv_post_kernel, grid=(m // tm,),
        in_specs=[row] * 5 + [full(lnx_w), full(lnx_b), full(r_k), full(hd), full(hdt), full(w_oa)],
        out_specs=pl.BlockSpec((tm, n), lambda i: (i, 0)), out_shape=jax.ShapeDtypeStruct((m, n), F32),
        compiler_params=_cp(1), name="rwkv_post",
    )(o, r, k2, v, g, lnx_w, lnx_b, r_k, hd, hdt, w_oa)


def _t5_bucket_table():
    d = np.arange(LANE)
    max_exact = NUM_BUCKETS // 2
    df = np.maximum(d, 1).astype(np.float32)
    large = max_exact + (np.log(df / np.float32(max_exact)) / np.float32(math.log(MAX_DIST / max_exact))
                         * np.float32(NUM_BUCKETS - max_exact)).astype(np.int32)
    large = np.minimum(large, NUM_BUCKETS - 1)
    tab = np.where(d < max_exact, d, large)
    assert tab[FAR] == NUM_BUCKETS - 1
    return tab


PAGES_PER_STEP = 8


def _page_specs(cache, lidx, npp):
    blk = (None, None) + cache.shape[2:]
    return [pl.BlockSpec(blk, lambda i, p, pt, k=k: (lidx, pt[i, p * npp + k], 0, 0)) for k in range(npp)]


def _gelu_tanh(x):
    return 0.5 * x * (1.0 + jnp.tanh(math.sqrt(2.0 / math.pi) * (x + 0.044715 * (x * x * x))))


def _compress_finish(y, ex_ref, w1, w2_ref, o_ref):
    rr = y.shape[0]
    e = _dot(ex_ref[0].astype(BF16), w1)
    nxt = pltpu.roll(y[:, LANE:], rr - 1, 0)
    row = lax.broadcasted_iota(jnp.int32, nxt.shape, 0)
    nxt = jnp.where(row == rr - 1, e[0:1, LANE:], nxt)
    pre = y[:, :LANE] + nxt + (e[1:2, :LANE] + e[2:3, LANE:])
    o_ref[0] = _dot(_gelu_tanh(pre).astype(BF16), w2_ref[0])


def _compress_kernel(x_ref, ex_ref, w1_ref, w2_ref, o_ref):
    rr = x_ref.shape[0] // S_CMP
    w1 = w1_ref[0]
    y = jnp.zeros((rr, 2 * LANE), F32)
    for j in range(S_CMP):
        xj = x_ref[pl.ds(j, rr, stride=S_CMP), :].astype(BF16)
        y = y + _dot(xj, w1[j * LANE:(j + 1) * LANE, :])
    _compress_finish(y, ex_ref, w1, w2_ref, o_ref)


def _compress_finish_kernel(y_ref, ex_ref, w1_ref, w2_ref, o_ref):
    _compress_finish(y_ref[0], ex_ref, w1_ref[0], w2_ref, o_ref)


def _compress_paged_kernel(pt_ref, *refs, ns, ncmp):
    x_refs, (w1_ref, y_ref) = refs[:-2], refs[-2:]
    cpp = x_refs[0].shape[0] // (ns * S_CMP)
    for s in range(ncmp):
        y = jnp.zeros((len(x_refs) * cpp, 2 * LANE), F32)
        for j in range(S_CMP):
            xj = jnp.concatenate([x[pl.ds(j * ns + s, cpp, stride=S_CMP * ns), :] for x in x_refs], axis=0)
            y = y + _dot(xj.astype(BF16), w1_ref[s // N_KV, j * LANE:(j + 1) * LANE, :])
        y_ref[0, s] = y


def _compress(x, x_spec, n_pos, ex, w1cat, w2):
    nb, _, cw = ex.shape
    rr = n_pos // S_CMP
    c_of = lambda i: (i // N_KV) % 2
    return pl.pallas_call(
        _compress_kernel, grid=(nb,),
        in_specs=[x_spec, pl.BlockSpec((1, SUBLANE, cw), lambda i: (i, 0, 0)),
                  pl.BlockSpec((1, cw, 2 * LANE), lambda i: (c_of(i), 0, 0)),
                  pl.BlockSpec((1, LANE, LANE), lambda i: (c_of(i), 0, 0))],
        out_specs=pl.BlockSpec((1, rr, LANE), lambda i: (i, 0, 0)),
        out_shape=jax.ShapeDtypeStruct((nb, rr, LANE), F32), compiler_params=_cp(1), name="nsa_compress",
    )(x, ex, w1cat, w2)


def _compress_paged(cache, lidx, page_table, ns, ex, w1cat, w2):
    b, n_pages = page_table.shape
    ncmp = ex.shape[0] // b
    cw = ex.shape[2]
    page = cache.shape[2] // ns
    npp = next(k for k in (PAGES_PER_STEP, 4, 2, 1) if n_pages % k == 0)
    rr = n_pages * page // S_CMP
    rt = npp * page // S_CMP
    grid_spec = pltpu.PrefetchScalarGridSpec(
        num_scalar_prefetch=1, grid=(b, n_pages // npp),
        in_specs=_page_specs(cache, lidx, npp) + [pl.BlockSpec(w1cat.shape, lambda i, p, pt: (0, 0, 0))],
        out_specs=pl.BlockSpec((1, ncmp, rt, 2 * LANE), lambda i, p, pt: (i, 0, p, 0)))
    y = pl.pallas_call(
        functools.partial(_compress_paged_kernel, ns=ns, ncmp=ncmp), grid_spec=grid_spec,
        out_shape=jax.ShapeDtypeStruct((b, ncmp, rr, 2 * LANE), F32),
        compiler_params=_cp(2), name="nsa_compress_paged",
    )(page_table, *([cache] * npp), w1cat)
    c_of = lambda i: (i // N_KV) % 2
    return pl.pallas_call(
        _compress_finish_kernel, grid=(b * ncmp,),
        in_specs=[pl.BlockSpec((1, rr, 2 * LANE), lambda i: (i, 0, 0)), pl.BlockSpec((1, SUBLANE, cw), lambda i: (i, 0, 0)),
                  pl.BlockSpec((1, cw, 2 * LANE), lambda i: (c_of(i), 0, 0)),
                  pl.BlockSpec((1, LANE, LANE), lambda i: (c_of(i), 0, 0))],
        out_specs=pl.BlockSpec((1, rr, LANE), lambda i: (i, 0, 0)),
        out_shape=jax.ShapeDtypeStruct((b * ncmp, rr, LANE), F32), compiler_params=_cp(1), name="nsa_compress_finish",
    )(y.reshape(b * ncmp, rr, 2 * LANE), ex, w1cat, w2)


def _bias_gather(tab_row, dist):
    idx = jnp.clip(dist, 0, FAR)
    return jnp.take_along_axis(jnp.broadcast_to(tab_row, idx.shape), idx, axis=1)


def _stack_heads(q_ref, col0=0, hpg=None):
    hpg = q_ref.shape[1] // HEAD_B if hpg is None else hpg
    return jnp.concatenate([q_ref[:, col0 + h * HEAD_B:col0 + (h + 1) * HEAD_B] for h in range(hpg)], axis=0)


def _rel_bias(dist, g, hpg, tab_ref):
    rel = []
    for h in range(hpg):
        tab_row = tab_ref[pl.ds(g * hpg + h, 1), :]
        far = tab_row[:, FAR:FAR + 1]
        rel.append(jnp.concatenate([_bias_gather(tab_row, dist[:, c * LANE:(c + 1) * LANE]) - far
                                    for c in range(dist.shape[1] // LANE)], axis=1))
    return jnp.stack(rel)


def _online_softmax_update(qs, tiles):
    hpg, tq, _ = tiles[0][4].shape
    scores = [_dot_nt(qs, kb) for kb, *_ in tiles]
    probs, alphas = [], []
    for s, (kb, vb, mask, rel, m_ref, l_ref, acc_ref) in zip(scores, tiles):
        width = kb.shape[0]
        s = s.reshape(hpg, tq, width) * (HEAD_B ** -0.5)
        if rel is None:
            s = s + jnp.where(mask, 0.0, 2 * NEG)[None]
        else:
            s = s + jnp.where(mask[None], rel, 2 * NEG)
        m_prev = m_ref[...][:, :, :1]
        m_new = jnp.maximum(m_prev, jnp.max(s, axis=2, keepdims=True))
        alpha = jnp.exp(m_prev - m_new)
        p = jnp.exp(s - m_new)
        m_ref[...] = jnp.broadcast_to(m_new, m_ref.shape)
        probs.append(p.reshape(hpg * tq, width).astype(BF16))
        alphas.append(alpha)
    pvs = [_dot(p, jnp.concatenate([t[1], jnp.ones_like(t[1])], axis=1)) for p, t in zip(probs, tiles)]
    for pv, alpha, t in zip(pvs, alphas, tiles):
        pv = pv.reshape(hpg, tq, 2 * HEAD_B)
        t[5][...] = alpha * t[5][...] + pv[:, :, HEAD_B:]
        t[6][...] = alpha * t[6][...] + pv[:, :, :HEAD_B]


def _cmp_attn_kernel(q_ref, kc_ref, vc_ref, ovt_ref, tab_ref, o_ref, imp_ref, *, tq, q0, lanes):
    g = pl.program_id(1)
    qt = pl.program_id(2)
    hpg = q_ref.shape[2] // HEAD_B
    rr = kc_ref.shape[1]
    kc = kc_ref[0].astype(BF16)
    vc = vc_ref[0].astype(BF16)
    qbase = q0 + qt * tq
    qrow = qbase + lax.broadcasted_iota(jnp.int32, (tq, LANE), 0)
    dists = []
    for cix in range(rr // LANE):
        c_end = S_CMP * (cix * LANE + lax.broadcasted_iota(jnp.int32, (tq, LANE), 1)) + (L_CMP - 1)
        dists.append(qrow - c_end)
    mask = (jnp.concatenate(dists, axis=1) >= 0)[None]
    bias = jnp.stack([jnp.concatenate([_bias_gather(tab_ref[pl.ds(g * hpg + h, 1), :], d) for d in dists], axis=1)
                      for h in range(hpg)])
    s = _dot_nt(_stack_heads(q_ref.at[0]), kc).reshape(hpg, tq, rr) * (HEAD_B ** -0.5) + bias
    s = jnp.where(mask, s, NEG)
    e = jnp.where(mask, jnp.exp(s - jnp.max(s, axis=2, keepdims=True)), 0.0)
    p = e / jnp.maximum(jnp.sum(e, axis=2, keepdims=True), 1e-30)
    o = _dot(p.reshape(hpg * tq, rr).astype(BF16), vc).reshape(hpg, tq, HEAD_B)
    o_ref[0] = jnp.concatenate([o[h] for h in range(hpg)], axis=1)
    psum = jnp.sum(p, axis=0)
    if tq < lanes:
        psum = jnp.concatenate([psum, jnp.zeros((lanes - tq, rr), F32)], axis=0)
    ph, plo = _split(psum)
    ovt = ovt_ref[...]
    imp_ref[0, 0] = _dot_nt(ovt, ph) + _dot_nt(ovt, plo)


def _select_kernel(imp_ref, qpos_ref, selt_ref, sc_ref, *, ns, n_top):
    nsr, lanes = sc_ref.shape
    j = lax.broadcasted_iota(jnp.int32, (nsr, lanes), 0)
    cur = qpos_ref[...] // L_SEL
    valid = j <= cur
    forced = (j == 0) | (j == cur) | (j == cur - 1)
    score = jnp.where(valid, imp_ref[0, 0] + jnp.where(forced, FORCE_BONUS, 0.0), NEG)
    sc_ref[...] = score

    def body(jp, rank):
        rowv = sc_ref[pl.ds(jp, 1), :]
        before = (rowv > score) | ((rowv == score) & (jp < j))
        return rank + jnp.where(before, 1.0, 0.0)

    rank = lax.fori_loop(0, ns, body, jnp.zeros((nsr, lanes), F32))
    selt_ref[0, 0] = jnp.where(valid & (rank < n_top), 1.0, 0.0)


def _select(imp, qpos, ns):
    x, gg, nsr, l = imp.shape
    lt = next(c for c in (512, 256, 128) if l % c == 0)
    blk = pl.BlockSpec((1, 1, nsr, lt), lambda i, g, t: (i, g, 0, t))
    return pl.pallas_call(
        functools.partial(_select_kernel, ns=ns, n_top=min(TOP_N, ns)), grid=(x, gg, l // lt),
        in_specs=[blk, pl.BlockSpec((1, lt), lambda i, g, t: (0, t))], out_specs=blk,
        out_shape=jax.ShapeDtypeStruct(imp.shape, F32),
        scratch_shapes=[pltpu.VMEM((nsr, lt), F32)], compiler_params=_cp(3), name="nsa_select",
    )(imp, qpos)


def _cmp_attn(q, kcv, tab, *, t_keys, q0, tq):
    b, tqa, d_b = q.shape
    rr = kcv.shape[1]
    gw = d_b // N_KV
    ns = -(-t_keys // L_SEL)
    nsr = _round_up(ns, SUBLANE)
    lanes = max(tq, LANE)
    nqt = tqa // tq
    ci = np.arange(rr)
    sj = np.arange(nsr)
    ov = ((S_CMP * ci) // L_SEL)[None, :] == sj[:, None]
    ov |= ((S_CMP * ci + L_CMP - 1) // L_SEL)[None, :] == sj[:, None]
    ovt = jnp.asarray(ov.astype(np.float32), BF16)
    kern = functools.partial(_cmp_attn_kernel, tq=tq, q0=q0, lanes=lanes)
    return pl.pallas_call(
        kern, grid=(b, N_KV, nqt),
        in_specs=[pl.BlockSpec((1, tq, gw), lambda i, g, t: (i, t, g)),
                  pl.BlockSpec((1, rr, LANE), lambda i, g, t: (i * 4 + g, 0, 0)),
                  pl.BlockSpec((1, rr, LANE), lambda i, g, t: (i * 4 + N_KV + g, 0, 0)),
                  pl.BlockSpec(ovt.shape, lambda i, g, t: (0, 0)),
                  pl.BlockSpec(tab.shape, lambda i, g, t: (0, 0))],
        out_specs=[pl.BlockSpec((1, tq, gw), lambda i, g, t: (i, t, g)),
                   pl.BlockSpec((1, 1, nsr, lanes), lambda i, g, t: (i, g, 0, t))],
        out_shape=[jax.ShapeDtypeStruct((b, tqa, d_b), F32),
                   jax.ShapeDtypeStruct((b, N_KV, nsr, nqt * lanes), F32)],
        compiler_params=_cp(3), name="nsa_cmp_attn",
    )(q, kcv, kcv, ovt, tab)


def _flash_kernel(*refs, mode, band, tq, tk, nj, q0, kbase, n_tail, tail_pos0):
    refs = list(refs)
    q_ref, k_ref, v_ref = refs[:3]
    pos = 3
    if n_tail:
        kt_ref, vt_ref = refs[pos:pos + 2]
        pos += 2
    if mode == "sel":
        sel_ref = refs[pos]
        pos += 1
    tab_ref, o_ref, m_ref, l_ref, acc_ref = refs[pos:pos + 5]
    g = pl.program_id(1)
    qt = pl.program_id(2)
    j = pl.program_id(3)
    hpg = q_ref.shape[1] // HEAD_B
    qbase = q0 + qt * tq
    qs = _stack_heads(q_ref)

    @pl.when(j == 0)
    def _():
        m_ref[...] = jnp.full_like(m_ref, NEG)
        l_ref[...] = jnp.zeros_like(l_ref)
        acc_ref[...] = jnp.zeros_like(acc_ref)

    def update(k_r, v_r, kpos0, width, n_valid, near):
        qrow = qbase + lax.broadcasted_iota(jnp.int32, (tq, width), 0)
        kcol = kpos0 + lax.broadcasted_iota(jnp.int32, (tq, width), 1)
        dist = qrow - kcol
        mask = dist >= 0
        if n_valid is not None:
            mask &= lax.broadcasted_iota(jnp.int32, (tq, width), 1) < n_valid
        if mode == "win":
            mask &= dist < WINDOW
        else:
            nsp = sel_ref.shape[1]
            blk = (kpos0 + lax.broadcasted_iota(jnp.int32, (nsp, width), 1)) // L_SEL
            expand = (blk == lax.broadcasted_iota(jnp.int32, (nsp, width), 0)).astype(BF16)
            mask &= _dot(sel_ref[...].astype(BF16), expand) > 0.5
        kb = k_r[...].astype(BF16)
        vb = v_r[...].astype(BF16)
        rel = _rel_bias(dist, g, hpg, tab_ref) if near else None
        _online_softmax_update(qs, [(kb, vb, mask, rel, m_ref, l_ref, acc_ref)])

    if band:
        kt_abs = qt - (nj - 1) + j
        active = kt_abs >= 0
    else:
        kt_abs = j
        active = kbase + j * tk <= qbase + tq - 1
    kpos0 = kbase + kt_abs * tk
    is_far = qbase - (kpos0 + tk - 1) >= FAR

    @pl.when(active & is_far)
    def _():
        update(k_ref, v_ref, kpos0, tk, None, False)

    @pl.when(active & jnp.logical_not(is_far))
    def _():
        update(k_ref, v_ref, kpos0, tk, None, True)

    @pl.when(j == nj - 1)
    def _():
        if n_tail:
            update(kt_ref, vt_ref, tail_pos0, kt_ref.shape[0], n_tail, True)
        outs = [acc_ref[h] / jnp.maximum(l_ref[h][:, :1], 1e-30) for h in range(hpg)]
        o_ref[...] = jnp.concatenate(outs, axis=1)


def _sel_paged_kernel(pt_ref, *refs, ns, n_tail, p_len):
    npp = len(refs) - 8
    q_ref, x_refs = refs[0], refs[1:1 + npp]
    tail_ref, sel_ref, tab_ref, o_ref, m_ref, l_ref, acc_ref = refs[1 + npp:]
    j = pl.program_id(1)
    nj = pl.num_programs(1)
    tq = q_ref.shape[0]
    gw = q_ref.shape[1] // N_KV
    hpg = gw // HEAD_B
    page = x_refs[0].shape[0] // ns
    width = npp * page

    @pl.when(j == 0)
    def _():
        m_ref[...] = jnp.full_like(m_ref, NEG)
        l_ref[...] = jnp.zeros_like(l_ref)
        acc_ref[...] = jnp.zeros_like(acc_ref)

    def fold(g, kb, vb, kpos0, n_valid, near):
        w = kb.shape[0]
        col = lax.broadcasted_iota(jnp.int32, (tq, w), 1)
        dist = p_len + lax.broadcasted_iota(jnp.int32, (tq, w), 0) - (kpos0 + col)
        mask = dist >= 0
        if n_valid is not None:
            mask &= col < n_valid
        nsp = sel_ref.shape[2]
        blk = (kpos0 + lax.broadcasted_iota(jnp.int32, (nsp, w), 1)) // L_SEL
        expand = (blk == lax.broadcasted_iota(jnp.int32, (nsp, w), 0)).astype(BF16)
        mask &= _dot(sel_ref[g].astype(BF16), expand) > 0.5
        rel = _rel_bias(dist, g, hpg, tab_ref) if near else None
        _online_softmax_update(_stack_heads(q_ref, g * gw, hpg),
                               [(kb, vb, mask, rel, m_ref.at[g], l_ref.at[g], acc_ref.at[g])])

    def pages(stream):
        return jnp.concatenate([x[pl.ds(stream, page, stride=ns), :] for x in x_refs], axis=0).astype(BF16)

    def main(near):
        for g in range(N_KV):
            fold(g, pages(2 * N_KV + g), pages(3 * N_KV + g), j * width, None, near)

    @pl.when(j < nj - 1)
    def _():
        main(False)

    @pl.when(j == nj - 1)
    def _():
        main(True)
        for g in range(N_KV):
            kt = tail_ref[:, (2 * N_KV + g) * LANE:(2 * N_KV + g + 1) * LANE].astype(BF16)
            vt = tail_ref[:, (3 * N_KV + g) * LANE:(3 * N_KV + g + 1) * LANE].astype(BF16)
            fold(g, kt, vt, p_len, n_tail, True)
        out = acc_ref[...] / jnp.maximum(l_ref[...][:, :, :, :1], 1e-30)
        o_ref[...] = jnp.concatenate([out[g, h] for g in range(N_KV) for h in range(hpg)], axis=1)


def _sel_paged(q, cache, lidx, page_table, ns, rows_t, n_tail, sel, tab):
    b, tq, d_b = q.shape
    n_pages = page_table.shape[1]
    page = cache.shape[2] // ns
    npp = next(k for k in (PAGES_PER_STEP, 4, 2, 1) if n_pages % k == 0)
    assert npp * page > FAR
    hpg = d_b // N_KV // HEAD_B
    whole = lambda a: pl.BlockSpec((None,) + a.shape[1:], lambda i, p, pt: (i,) + (0,) * (a.ndim - 1))
    grid_spec = pltpu.PrefetchScalarGridSpec(
        num_scalar_prefetch=1, grid=(b, n_pages // npp),
        in_specs=[whole(q)] + _page_specs(cache, lidx, npp) + [whole(rows_t), whole(sel),
                  pl.BlockSpec(tab.shape, lambda i, p, pt: (0, 0))],
        out_specs=pl.BlockSpec((None, tq, d_b), lambda i, p, pt: (i, 0, 0)),
        scratch_shapes=[pltpu.VMEM((N_KV, hpg, tq, LANE), F32), pltpu.VMEM((N_KV, hpg, tq, LANE), F32),
                        pltpu.VMEM((N_KV, hpg, tq, HEAD_B), F32)])
    return pl.pallas_call(
        functools.partial(_sel_paged_kernel, ns=ns, n_tail=n_tail, p_len=n_pages * page), grid_spec=grid_spec,
        out_shape=jax.ShapeDtypeStruct((b, tq, d_b), F32), compiler_params=_cp(2), name="nsa_sel_attn_paged",
    )(page_table, q, *([cache] * npp), rows_t, sel, tab)


def _flash(q, k_arr, k_spec, v_arr, v_spec, tab, *, mode, band, tq, tk, nj, q0, kbase,
           tail=None, sel=None, name):
    b, tqa, d_b = q.shape
    gw = d_b // N_KV
    hpg = gw // HEAD_B
    nqt = tqa // tq
    args = [q, k_arr, v_arr]
    in_specs = [pl.BlockSpec((None, tq, gw), lambda i, g, t, j: (i, t, g)), k_spec, v_spec]
    n_tail, tail_pos0 = 0, 0
    if tail is not None:
        tk_arr, tk_spec, tv_arr, tv_spec, n_tail, tail_pos0 = tail
        args += [tk_arr, tv_arr]
        in_specs += [tk_spec, tv_spec]
    if mode == "sel":
        args.append(sel)
        in_specs.append(pl.BlockSpec((None, None, tq, sel.shape[3]), lambda i, g, t, j: (i, g, t, 0)))
    args.append(tab)
    in_specs.append(pl.BlockSpec(tab.shape, lambda i, g, t, j: (0, 0)))
    kern = functools.partial(_flash_kernel, mode=mode, band=band, tq=tq, tk=tk, nj=nj, q0=q0, kbase=kbase,
                             n_tail=n_tail, tail_pos0=tail_pos0)
    return pl.pallas_call(
        kern, grid=(b, N_KV, nqt, nj), in_specs=in_specs,
        out_specs=pl.BlockSpec((None, tq, gw), lambda i, g, t, j: (i, t, g)),
        out_shape=jax.ShapeDtypeStruct((b, tqa, d_b), F32),
        scratch_shapes=[pltpu.VMEM((hpg, tq, LANE), F32), pltpu.VMEM((hpg, tq, LANE), F32),
                        pltpu.VMEM((hpg, tq, HEAD_B), F32)],
        compiler_params=_cp(4), name=name,
    )(*args)


def _attn_res_kernel(*refs, mode, tq, ch, far_w):
    if mode == "sel":
        q_ref, k_ref, v_ref, sel_ref, tab_ref, o_ref, m_ref, l_ref, acc_ref = refs
    else:
        q_ref, k_ref, v_ref, tab_ref, o_ref, m_ref, l_ref, acc_ref = refs
    g = pl.program_id(1)
    qt = pl.program_id(2)
    hpg = q_ref.shape[1] // HEAD_B
    qbase = qt * tq
    near0 = qbase - LANE
    qs = _stack_heads(q_ref)
    m_ref[...] = jnp.full_like(m_ref, NEG)
    l_ref[...] = jnp.zeros_like(l_ref)
    acc_ref[...] = jnp.zeros_like(acc_ref)

    def tile(stream, kpos0, width, near):
        qrow = qbase + lax.broadcasted_iota(jnp.int32, (tq, width), 0)
        kcol = kpos0 + lax.broadcasted_iota(jnp.int32, (tq, width), 1)
        dist = qrow - kcol
        mask = (dist >= 0) if near else (kcol < near0)
        if mode == "win":
            mask &= dist < WINDOW
        else:
            nsp = sel_ref.shape[1]
            blk = (kpos0 + lax.broadcasted_iota(jnp.int32, (nsp, width), 1)) // L_SEL
            expand = (blk == lax.broadcasted_iota(jnp.int32, (nsp, width), 0)).astype(BF16)
            mask &= _dot(sel_ref[...].astype(BF16), expand) > 0.5
        kb = k_ref[pl.ds(kpos0, width), :].astype(BF16)
        vb = v_ref[pl.ds(kpos0, width), :].astype(BF16)
        rel = _rel_bias(dist, g, hpg, tab_ref) if near else None
        return (kb, vb, mask, rel, m_ref.at[stream], l_ref.at[stream], acc_ref.at[stream])

    near = lambda: tile(1, pl.multiple_of(jnp.maximum(near0, 0), LANE), 2 * LANE, True)
    if mode == "sel":
        def body(c, carry):
            _online_softmax_update(qs, [tile(0, pl.multiple_of(c * ch, ch), ch, False)])
            return carry

        lax.fori_loop(0, (jnp.maximum(near0, 0) + ch - 1) // ch, body, 0)
        _online_softmax_update(qs, [near()])
    elif far_w:
        far = tile(0, pl.multiple_of(jnp.maximum(qbase - WINDOW, 0), LANE), far_w, False)
        _online_softmax_update(qs, [far, near()])
    else:
        _online_softmax_update(qs, [near()])
    m0, m1 = m_ref[0][:, :, :1], m_ref[1][:, :, :1]
    m = jnp.maximum(m0, m1)
    w0, w1 = jnp.exp(m0 - m), jnp.exp(m1 - m)
    l = w0 * l_ref[0][:, :, :1] + w1 * l_ref[1][:, :, :1]
    out = (w0 * acc_ref[0] + w1 * acc_ref[1]) / jnp.maximum(l, 1e-30)
    o_ref[...] = jnp.concatenate([out[h] for h in range(hpg)], axis=1)


def _attn_res(q, kv, kcol, vcol, tab, *, mode, sel=None, name):
    b, t, d_b = q.shape
    gw = d_b // N_KV
    hpg = gw // HEAD_B
    tq = LANE
    assert t % tq == 0 and t >= 2 * LANE
    ch = next(c for c in (1024, 512, 256, 128) if t % c == 0)
    far_w = min(WINDOW - LANE, t - 2 * LANE)
    args = [q, kv, kv]
    in_specs = [pl.BlockSpec((None, tq, gw), lambda i, g, qt: (i, qt, g)),
                pl.BlockSpec((None, t, LANE), lambda i, g, qt: (i, 0, kcol + g)),
                pl.BlockSpec((None, t, LANE), lambda i, g, qt: (i, 0, vcol + g))]
    if mode == "sel":
        args.append(sel)
        in_specs.append(pl.BlockSpec((None, None, tq, sel.shape[3]), lambda i, g, qt: (i, g, qt, 0)))
    args.append(tab)
    in_specs.append(pl.BlockSpec(tab.shape, lambda i, g, qt: (0, 0)))
    return pl.pallas_call(
        functools.partial(_attn_res_kernel, mode=mode, tq=tq, ch=ch, far_w=far_w),
        grid=(b, N_KV, t // tq), in_specs=in_specs,
        out_specs=pl.BlockSpec((None, tq, gw), lambda i, g, qt: (i, qt, g)),
        out_shape=jax.ShapeDtypeStruct((b, t, d_b), F32),
        scratch_shapes=[pltpu.VMEM((2, hpg, tq, LANE), F32), pltpu.VMEM((2, hpg, tq, LANE), F32),
                        pltpu.VMEM((2, hpg, tq, HEAD_B), F32)],
        compiler_params=_cp(3), name=name,
    )(*args)


def _nsa_out_kernel(oc_ref, os_ref, ow_ref, gl_ref, e_ref, w_ref, y_ref):
    gate = _sigmoid(gl_ref[...])
    o = (_dot_x(gate, e_ref[0]) * oc_ref[...] + _dot_x(gate, e_ref[1]) * os_ref[...]
         + _dot_x(gate, e_ref[2]) * ow_ref[...])
    y_ref[...] = _dot(o.astype(BF16), w_ref[...])


def _nsa_out(o_c, o_s, o_w, gl, w_ob):
    m, d_b = o_c.shape
    n = w_ob.shape[1]
    h_b = d_b // HEAD_B
    tm = _tile(m, 256)
    e = np.zeros((3, LANE, d_b), np.float32)
    for br in range(3):
        for hh in range(h_b):
            e[br, br * h_b + hh, hh * HEAD_B:(hh + 1) * HEAD_B] = 1.0
    e = jnp.asarray(e, BF16)
    row = pl.BlockSpec((tm, d_b), lambda i: (i, 0))
    return pl.pallas_call(
        _nsa_out_kernel, grid=(m // tm,),
        in_specs=[row, row, row, pl.BlockSpec((tm, LANE), lambda i: (i, 0)),
                  pl.BlockSpec(e.shape, lambda i: (0, 0, 0)), pl.BlockSpec(w_ob.shape, lambda i: (0, 0))],
        out_specs=pl.BlockSpec((tm, n), lambda i: (i, 0)), out_shape=jax.ShapeDtypeStruct((m, n), F32),
        compiler_params=_cp(1), name="nsa_out",
    )(o_c, o_s, o_w, gl, e, w_ob)


def _nsa(q, rows, win, gl, tab, lw, *, t_real, paged, lidx, cwin):
    b, tqa, d_b = q.shape
    t = t_real
    ncmp = 2 * N_KV
    nstream = 4 * N_KV
    cw = S_CMP * LANE
    p_len = 0 if paged is None else paged[1].shape[1] * (paged[0].shape[2] // nstream)
    l_tot = p_len + t
    assert p_len % (S_CMP * SUBLANE) == 0 and (p_len == 0 or t <= S_CMP) and (p_len > 0 or t % LANE == 0)
    pe = jnp.repeat(lw["cmp_pe"].reshape(2, 1, 2, cw), N_KV, axis=1)
    pe = jnp.broadcast_to(pe[None], (b, 2, N_KV, 2, cw)).reshape(b * ncmp, 2, cw)
    pad_rows = jnp.zeros((b * ncmp, SUBLANE - 3, cw), F32)
    if paged is None:
        ex = jnp.concatenate([jnp.zeros((b * ncmp, 1, cw), F32), pe, pad_rows], axis=1)
        x_spec = pl.BlockSpec((None, t, LANE), lambda i: (i // ncmp, 0, i % ncmp))
        kcv = _compress(rows, x_spec, t, ex, lw["w1cat"], lw["w2"])
    else:
        new_cmp = rows[:, :, :ncmp * LANE].reshape(b, t, ncmp, LANE).transpose(0, 2, 1, 3)
        ex0 = jnp.pad(new_cmp.reshape(b * ncmp, 1, t * LANE), ((0, 0), (0, 0), (0, cw - t * LANE)))
        ex = jnp.concatenate([ex0, pe, pad_rows], axis=1)
        kcv = _compress_paged(paged[0], lidx, paged[1], nstream, ex, lw["w1cat"], lw["w2"])
    if kcv.shape[1] % LANE:
        kcv = jnp.pad(kcv, ((0, 0), (0, _round_up(kcv.shape[1], LANE) - kcv.shape[1]), (0, 0)))
    tq = tqa if p_len else LANE
    o_c, imp = _cmp_attn(q, kcv, tab, t_keys=l_tot, q0=p_len, tq=tq)
    ns = -(-l_tot // L_SEL)
    nsp = _round_up(ns, LANE)
    if p_len == 0:
        sel = jnp.swapaxes(_select(imp, jnp.arange(tqa, dtype=jnp.int32)[None], ns), 2, 3)
    else:
        nq = b * tqa
        nql = _round_up(nq, LANE)
        nsr = imp.shape[2]
        impl = jnp.pad(imp[..., :tqa].transpose(1, 2, 0, 3).reshape(1, N_KV, nsr, nq),
                       ((0, 0), (0, 0), (0, 0), (0, nql - nq)))
        qpos = (p_len + jnp.arange(nql, dtype=jnp.int32) % tqa)[None]
        sel = _select(impl, qpos, ns)[0, :, :, :nq].reshape(N_KV, nsr, b, tqa).transpose(2, 0, 3, 1)
    sel = jnp.pad(sel, ((0, 0), (0, 0), (0, 0), (0, nsp - sel.shape[3])))
    if paged is None:
        o_s = _attn_res(q, rows, 2 * N_KV, 3 * N_KV, tab, mode="sel", sel=sel, name="nsa_sel_attn")
        o_w = _attn_res(q, win, 0, N_KV, tab, mode="win", name="nsa_win_attn")
    else:
        rows_t = jnp.pad(rows, ((0, 0), (0, LANE - t), (0, 0)))
        win_t = jnp.pad(win, ((0, 0), (0, LANE - t), (0, 0)))
        tails = lambda col: pl.BlockSpec((None, LANE, LANE), lambda i, g, qt, j, col=col: (i, 0, col + g))
        o_s = _sel_paged(q, paged[0], lidx, paged[1], nstream, rows_t, t, sel, tab)
        wb = cwin.shape[1]
        kw = lambda col: pl.BlockSpec((None, wb, LANE), lambda i, g, qt, j, col=col: (i, 0, col + g))
        o_w = _flash(q, cwin, kw(0), cwin, kw(N_KV), tab, mode="win", band=False, tq=tq, tk=wb,
                     nj=1, q0=p_len, kbase=p_len - wb,
                     tail=(win_t, tails(0), win_t, tails(N_KV), t, p_len), name="nsa_win_attn_cached")
    m = b * tqa
    return _nsa_out(o_c.reshape(m, d_b), o_s.reshape(m, d_b), o_w.reshape(m, d_b), gl, lw["w_ob"])


def _layer_weights(l, w_in, mu, w0, w_up, a0, a_up, g_up, k_k, k_a, r_k, lnx_w, lnx_b, w_oa,
                   cmp_pe, cmp_w1, cmp_w2, w_ob, w_o, w_ff_up, w_ff_down, norm_g):
    d_a = w_oa.shape[1]
    d_b = w_ob.shape[1]
    d = w_o.shape[1]
    rw = 3 * d_a + DECAY_LORA + A_LORA + GATE_LORA
    pw = 3 * d_a + LORA_W
    nrow = 4 * N_KV * HEAD_B
    nwin = 2 * N_KV * HEAD_B
    ngl = 3 * (d_b // HEAD_B)
    wi = w_in[l]
    o_q = rw
    o_rows = o_q + d_b
    o_win = o_rows + nrow
    o_gl = o_win + nwin
    o_pg = o_gl + ngl
    wl = jnp.zeros((LORA_W, 3 * d_a), F32)
    wl = wl.at[:DECAY_LORA, :d_a].set(w_up[l])
    wl = wl.at[DECAY_LORA:DECAY_LORA + A_LORA, d_a:2 * d_a].set(a_up[l])
    wl = wl.at[DECAY_LORA + A_LORA:DECAY_LORA + A_LORA + GATE_LORA, 2 * d_a:].set(g_up[l])
    half = S_CMP * HEAD_B
    return dict(
        w_pa=jnp.pad(wi[:, :rw], ((0, 0), (0, pw - rw))).astype(BF16),
        w_q=wi[:, o_q:o_rows].astype(BF16),
        w_rows=wi[:, o_rows:o_win].astype(BF16),
        w_win=wi[:, o_win:o_gl].astype(BF16),
        w_gl=jnp.pad(wi[:, o_gl:o_pg], ((0, 0), (0, LANE - ngl))).astype(BF16),
        w_pg=wi[:, o_pg:o_pg + 2 * d].astype(BF16),
        mu=jnp.pad(mu[l], (0, pw - rw))[None], wl=wl.astype(BF16),
        w0=w0[l][None], a0=a0[l][None], k_k=k_k[l][None], k_a=k_a[l][None],
        r_k=r_k[l].reshape(1, d_a), lnx_w=lnx_w[l][None], lnx_b=lnx_b[l][None],
        w_oa=w_oa[l].astype(BF16), w_ob=w_ob[l].astype(BF16), w_o=w_o[l].astype(BF16),
        w_up=w_ff_up[l].astype(BF16), w_down=w_ff_down[l].astype(BF16),
        cmp_pe=cmp_pe[l],
        w1cat=jnp.concatenate([cmp_w1[l][:, :half], cmp_w1[l][:, half:]], axis=2).astype(BF16),
        w2=cmp_w2[l].astype(BF16), g=norm_g[l], rw=rw,
    )


def _layer(x, lw, tab, shift0, s0, paged, lidx, cwin):
    b, t, d = x.shape
    m = b * t
    d_a = lw["w_oa"].shape[0]
    x2 = x.reshape(m, d)
    g = lw["g"]
    xn = _rmsnorm_cast(x2, g[0:1])
    pa = _matmul(xn, lw["w_pa"], F32, name="proj_rwkv")
    q = _matmul(xn, lw["w_q"], BF16, name="proj_q")
    rows = _matmul(xn, lw["w_rows"], F32, name="proj_rows")
    win = _matmul(xn, lw["w_win"], F32, name="proj_win")
    gl = _matmul(xn, lw["w_gl"], F32, name="proj_gl")
    pg = _matmul(xn, lw["w_pg"], F32, name="proj_pg")
    pw = pa.shape[1]
    d_b = q.shape[1]
    tp = t if t % RWKV_CHUNK == 0 else _round_up(t, SUBLANE)
    c = RWKV_CHUNK if t % RWKV_CHUNK == 0 else tp
    pa3 = pa.reshape(b, t, pw)
    pa_p = pa3 if tp == t else jnp.pad(pa3, ((0, 0), (0, tp - t), (0, 0)))
    prev = jnp.pad(shift0, ((0, 0), (0, pw - shift0.shape[1])))[:, None]
    r, lgw, k2, v, kk, a, gg = _rwkv_prep(pa_p, prev, lw["mu"], lw["wl"], lw["w0"], lw["a0"], lw["k_k"], lw["k_a"],
                                          None if tp == t else t)
    o, s_fin = _rwkv_scan(r, lgw, k2, v, kk, a, s0, c)
    flat = lambda u: u.reshape(b * tp, d_a)
    ya = _rwkv_post(flat(o), flat(r), flat(k2), flat(v), flat(gg), lw["lnx_w"], lw["lnx_b"], lw["r_k"], lw["w_oa"])
    if tp != t:
        ya = ya.reshape(b, tp, d)[:, :t].reshape(m, d)
    sh = pa3[:, t - 1, :lw["rw"]]
    tqa = t if t % LANE == 0 else _round_up(t, SUBLANE)
    q3 = q.reshape(b, t, d_b)
    gl_p = gl
    if tqa != t:
        q3 = jnp.pad(q3, ((0, 0), (0, tqa - t), (0, 0)))
        gl_p = jnp.pad(gl.reshape(b, t, LANE), ((0, 0), (0, tqa - t), (0, 0))).reshape(b * tqa, LANE)
    rows3 = rows.reshape(b, t, rows.shape[1])
    win3 = win.reshape(b, t, win.shape[1])
    yb = _nsa(q3, rows3, win3, gl_p, tab, lw, t_real=t, paged=paged, lidx=lidx, cwin=cwin)
    if tqa != t:
        yb = yb.reshape(b, tqa, d)[:, :t].reshape(m, d)
    x1 = _merge(ya, yb, pg, x2, lw["w_o"], g[1:2])
    h = _matmul(_rmsnorm_cast(x1, g[2:3]), lw["w_up"], BF16, relu2=True, name="ffn_up")
    x_out = _ffn_down(h, lw["w_down"], x1, g[3:4])
    wctx = win3 if cwin is None else jnp.concatenate([cwin, win3], axis=1)
    n_keep = min(WINDOW, wctx.shape[1])
    return x_out.reshape(b, t, d), rows3, wctx[:, wctx.shape[1] - n_keep:], sh, s_fin


def _pair_states(s):
    b, h, n, _ = s.shape
    s = s.reshape(b, h // 2, 2, n, n)
    z = jnp.zeros((b, h // 2, n, n), s.dtype)
    top = jnp.concatenate([s[:, :, 0], z], axis=3)
    bot = jnp.concatenate([z, s[:, :, 1]], axis=3)
    return jnp.concatenate([top, bot], axis=2)


def _unpair_states(s):
    n = HEAD_A
    b, hp = s.shape[:2]
    return jnp.stack([s[:, :, :n, :n], s[:, :, n:, n:]], axis=2).reshape(b, 2 * hp, n, n)


def _trunk(x, shift0, wkv0, paged, cache_win, tab, layers):
    rows, wins, shifts, wkvs = [], [], [], []
    for l, lw in enumerate(layers):
        x, nr, nw, sh, st = _layer(x, lw, tab, shift0[l], _pair_states(wkv0[l]),
                                   paged, l, None if cache_win is None else cache_win[l])
        b, t = nr.shape[:2]
        rows.append(nr.reshape(b, t, 4, N_KV, HEAD_B))
        wins.append(nw.reshape(b, nw.shape[1], 2, N_KV, HEAD_B))
        shifts.append(sh)
        wkvs.append(_unpair_states(st))
    return x, jnp.stack(rows), jnp.stack(wins), jnp.stack(shifts), jnp.stack(wkvs)


def kernel(x_prompt, x_sample, cache_kv, cache_win, state_shift, state_wkv, page_table, w_in, mu, w0, w_up, a0, a_up, g_up, k_k, k_a, r_k, lnx_w, lnx_b, w_oa, cmp_pe, cmp_w1, cmp_w2, w_ob, w_o, w_ff_up, w_ff_down, norm_g, rel_bias):
    depth = w_in.shape[0]
    layers = [_layer_weights(l, w_in, mu, w0, w_up, a0, a_up, g_up, k_k, k_a, r_k, lnx_w, lnx_b, w_oa,
                             cmp_pe, cmp_w1, cmp_w2, w_ob, w_o, w_ff_up, w_ff_down, norm_g) for l in range(depth)]
    tab = rel_bias[_t5_bucket_table()].T
    bp = x_prompt.shape[0]
    rw = state_shift.shape[2]
    zeros_shift = jnp.zeros((depth, bp, rw), x_prompt.dtype)
    zeros_wkv = jnp.zeros((depth, bp) + state_wkv.shape[2:], state_wkv.dtype)
    y_p, kv_p, win_p, sh_p, wkv_p = _trunk(x_prompt, zeros_shift, zeros_wkv, None, None, tab, layers)
    nl, n_pool, page = cache_kv.shape[:3]
    paged = (cache_kv.reshape(nl, n_pool, page * 4 * N_KV, HEAD_B), page_table)
    cwin = cache_win.reshape(cache_win.shape[:3] + (-1,))
    y_s, kv_s, win_s, sh_s, wkv_s = _trunk(x_sample, state_shift, state_wkv, paged, cwin, tab, layers)
    return (y_p, y_s, kv_p, kv_s, win_p, win_s, sh_p, sh_s, wkv_p, wkv_s)
```

```python
import functools
import math

import numpy as np
import jax
import jax.numpy as jnp
from jax import lax
from jax.experimental import pallas as pl
from jax.experimental.pallas import tpu as pltpu

F32 = jnp.float32
BF16 = jnp.bfloat16

HEAD_A = 64
DECAY_LORA = 64
A_LORA = 64
GATE_LORA = 160
LNX_EPS = 64e-5
N_KV = 2
HEAD_B = 128
L_CMP = 32
S_CMP = 16
L_SEL = 64
TOP_N = 16
WINDOW = 512
FORCE_BONUS = 1e4
NEG = -1e30
NUM_BUCKETS = 32
MAX_DIST = 128
EPS = 1e-6
PAGE_SIZE = 128

LANE = 128
SUBLANE = 8
VMEM_LIMIT = 56 * 1024 * 1024
RWKV_CHUNK = 64
LORA_W = 384
FAR = LANE - 1


def _cp(n_axes):
    return pltpu.CompilerParams(dimension_semantics=("arbitrary",) * n_axes,
                                vmem_limit_bytes=VMEM_LIMIT)


def _round_up(x, m):
    return -(-x // m) * m


def _tile(n, pref, mult=SUBLANE):
    if n <= pref:
        return n
    for t in range(pref - pref % mult, 0, -mult):
        if n % t == 0:
            return t
    return n


def _dot(a, b):
    return jnp.dot(a, b, preferred_element_type=F32)


def _dot_nt(a, b):
    return lax.dot_general(a, b, (((1,), (1,)), ((), ())), preferred_element_type=F32)


def _dot_tn(a, b):
    return lax.dot_general(a, b, (((0,), (0,)), ((), ())), preferred_element_type=F32)


def _split(x):
    hi = x.astype(BF16)
    lo = (x - hi.astype(F32)).astype(BF16)
    return hi, lo


def _dot_x(a, b):
    hi, lo = _split(a)
    return _dot(hi, b) + _dot(lo, b)


def _dot3(a, b):
    ah, al = _split(a)
    bh, bl = _split(b)
    return _dot(ah, bh) + _dot(ah, bl) + _dot(al, bh)


def _dot3_nt(a, b):
    ah, al = _split(a)
    bh, bl = _split(b)
    return _dot_nt(ah, bh) + _dot_nt(ah, bl) + _dot_nt(al, bh)


def _dot3_tn(a, b):
    ah, al = _split(a)
    bh, bl = _split(b)
    return _dot_tn(ah, bh) + _dot_tn(ah, bl) + _dot_tn(al, bh)


def _sigmoid(x):
    return 1.0 / (1.0 + jnp.exp(-x))


def _rms(y, g):
    return y * lax.rsqrt(jnp.mean(y * y, axis=-1, keepdims=True) + EPS) * g


def _rmsnorm_kernel(x_ref, g_ref, o_ref):
    o_ref[...] = _rms(x_ref[...], g_ref[...]).astype(o_ref.dtype)


def _rmsnorm_cast(x, g):
    m, d = x.shape
    tm = _tile(m, 256)
    return pl.pallas_call(
        _rmsnorm_kernel, grid=(m // tm,),
        in_specs=[pl.BlockSpec((tm, d), lambda i: (i, 0)), pl.BlockSpec((1, d), lambda i: (0, 0))],
        out_specs=pl.BlockSpec((tm, d), lambda i: (i, 0)),
        out_shape=jax.ShapeDtypeStruct((m, d), BF16), compiler_params=_cp(1), name="rmsnorm_cast",
    )(x, g)


def _mm_kernel(a_ref, w_ref, o_ref, *, relu2):
    y = _dot(a_ref[...], w_ref[...])
    if relu2:
        y = jnp.square(jnp.maximum(y, 0.0))
    o_ref[...] = y.astype(o_ref.dtype)


def _matmul(a, w, out_dtype, relu2=False, name="matmul"):
    m, k = a.shape
    n = w.shape[1]
    tm = _tile(m, 1024)
    tn = _tile(n, 1280, LANE)
    return pl.pallas_call(
        functools.partial(_mm_kernel, relu2=relu2), grid=(m // tm, n // tn),
        in_specs=[pl.BlockSpec((tm, k), lambda i, j: (i, 0)), pl.BlockSpec((k, tn), lambda i, j: (0, j))],
        out_specs=pl.BlockSpec((tm, tn), lambda i, j: (i, j)),
        out_shape=jax.ShapeDtypeStruct((m, n), out_dtype), compiler_params=_cp(2), name=name,
    )(a, w)


def _ffn_down_kernel(h_ref, w_ref, x_ref, g_ref, o_ref, n_ref, acc_ref):
    k = pl.program_id(1)

    @pl.when(k == 0)
    def _():
        acc_ref[...] = jnp.zeros_like(acc_ref)

    acc_ref[...] += _dot(h_ref[...], w_ref[...])

    @pl.when(k == pl.num_programs(1) - 1)
    def _():
        o = x_ref[...] + _rms(acc_ref[...], g_ref[0:1])
        o_ref[...] = o
        n_ref[...] = _rms(o, g_ref[1:2]).astype(n_ref.dtype)


def _ffn_down(h, w, x, g):
    m, kdim = h.shape
    n = w.shape[1]
    tm = _tile(m, 512)
    tk = _tile(kdim, 2048, LANE)
    row = pl.BlockSpec((tm, n), lambda i, k: (i, 0))
    return pl.pallas_call(
        _ffn_down_kernel, grid=(m // tm, kdim // tk),
        in_specs=[pl.BlockSpec((tm, tk), lambda i, k: (i, k)), pl.BlockSpec((tk, n), lambda i, k: (k, 0)),
                  row, pl.BlockSpec((2, n), lambda i, k: (0, 0))],
        out_specs=[row, row],
        out_shape=[jax.ShapeDtypeStruct((m, n), F32), jax.ShapeDtypeStruct((m, n), BF16)],
        scratch_shapes=[pltpu.VMEM((tm, n), F32)], compiler_params=_cp(2), name="ffn_down",
    )(h, w, x, g)


def _merge_kernel(ya_ref, yb_ref, pg_ref, x_ref, w_ref, g_ref, o_ref, n_ref):
    d = ya_ref.shape[1]
    pg = pg_ref[...].astype(F32)
    mix = _sigmoid(pg[:, :d]) * ya_ref[...].astype(F32) + _sigmoid(pg[:, d:]) * yb_ref[...].astype(F32)
    y = _dot(mix.astype(BF16), w_ref[...])
    o = x_ref[...] + _rms(y, g_ref[0:1])
    o_ref[...] = o
    n_ref[...] = _rms(o, g_ref[1:2]).astype(n_ref.dtype)


def _merge(ya, yb, pg, x, w_o, g):
    m, d = x.shape
    tm = _tile(m, 256)
    row = lambda c: pl.BlockSpec((tm, c), lambda i: (i, 0))
    return pl.pallas_call(
        _merge_kernel, grid=(m // tm,),
        in_specs=[row(d), row(d), row(2 * d), row(d),
                  pl.BlockSpec((d, d), lambda i: (0, 0)), pl.BlockSpec((2, d), lambda i: (0, 0))],
        out_specs=[row(d), row(d)],
        out_shape=[jax.ShapeDtypeStruct((m, d), F32), jax.ShapeDtypeStruct((m, d), BF16)],
        compiler_params=_cp(1), name="merge",
    )(ya, yb, pg, x, w_o, g)


def _head_indicator(d_a):
    h = np.zeros((d_a, LANE), np.float32)
    h[np.arange(d_a), np.arange(d_a) // HEAD_A] = 1.0
    return jnp.asarray(h, BF16), jnp.asarray(h.T, BF16)


def _rwkv_prep_kernel(pa_ref, prev_ref, mu_ref, wl_ref, w0_ref, a0_ref, kk_ref, ka_ref, hd_ref, hdt_ref,
                      r_ref, lw_ref, k2_ref, v_ref, kkn_ref, a_ref, g_ref, carry_ref, *, d_a, t_real):
    t = pl.program_id(1)
    tt = pa_ref.shape[1]

    @pl.when(t == 0)
    def _():
        carry_ref[...] = prev_ref[0]

    x = pa_ref[0]
    row = lax.broadcasted_iota(jnp.int32, x.shape, 0)
    xprev = jnp.where(row == 0, carry_ref[...], pltpu.roll(x, 1, 0))
    carry_ref[...] = pa_ref[0, pl.ds(tt - 1, 1), :]
    xs = x + (xprev - x) * mu_ref[...]
    r = xs[:, :d_a]
    k = xs[:, d_a:2 * d_a]
    v = xs[:, 2 * d_a:3 * d_a]
    lo = xs[:, 3 * d_a:]
    lane = lax.broadcasted_iota(jnp.int32, lo.shape, 1)
    act = jnp.where(lane < DECAY_LORA, jnp.tanh(lo),
                    jnp.where(lane < DECAY_LORA + A_LORA, lo, _sigmoid(lo)))
    lin = _dot(act.astype(BF16), wl_ref[...])
    z = -(w0_ref[...] + lin[:, :d_a])
    w = -(jnp.maximum(z, 0.0) + jnp.log(1.0 + jnp.exp(-jnp.abs(z)))) - 0.5
    logw = -jnp.exp(w)
    a = _sigmoid(a0_ref[...] + lin[:, d_a:2 * d_a])
    g = lin[:, 2 * d_a:]
    kkr = k * kk_ref[...]
    ss = _dot_x(kkr * kkr, hd_ref[...])
    inv = 1.0 / jnp.maximum(jnp.sqrt(ss), 1e-12)
    kkn = kkr * _dot_x(inv, hdt_ref[...])
    k2 = k * (1.0 + (a - 1.0) * ka_ref[...])
    if t_real is not None:
        live = (t * tt + lax.broadcasted_iota(jnp.int32, r.shape, 0)) < t_real
        zero = jnp.zeros_like(r)
        r, logw, k2, v, kkn, a = (jnp.where(live, u, zero) for u in (r, logw, k2, v, kkn, a))
    r_ref[0] = r.astype(r_ref.dtype)
    lw_ref[0] = logw
    k2_ref[0] = k2.astype(k2_ref.dtype)
    v_ref[0] = v.astype(v_ref.dtype)
    kkn_ref[0] = kkn.astype(kkn_ref.dtype)
    a_ref[0] = a.astype(a_ref.dtype)
    g_ref[0] = g.astype(g_ref.dtype)


def _rwkv_prep(pa, prev, mu, wl, w0, a0, k_k, k_a, t_real):
    b, t, p = pa.shape
    d_a = w0.shape[1]
    tt = _tile(t, 256)
    hd, hdt = _head_indicator(d_a)
    full = lambda a: pl.BlockSpec(a.shape, lambda i, j: (0,) * a.ndim)
    out = [jax.ShapeDtypeStruct((b, t, d_a), F32 if i == 1 else BF16) for i in range(7)]
    ospec = pl.BlockSpec((1, tt, d_a), lambda i, j: (i, j, 0))
    return pl.pallas_call(
        functools.partial(_rwkv_prep_kernel, d_a=d_a, t_real=t_real), grid=(b, t // tt),
        in_specs=[pl.BlockSpec((1, tt, p), lambda i, j: (i, j, 0)), pl.BlockSpec((1, 1, p), lambda i, j: (i, 0, 0)),
                  full(mu), full(wl), full(w0), full(a0), full(k_k), full(k_a), full(hd), full(hdt)],
        out_specs=[ospec] * 7, out_shape=out,
        scratch_shapes=[pltpu.VMEM((1, p), F32)], compiler_params=_cp(2), name="rwkv_prep",
    )(pa, prev, mu, wl, w0, a0, k_k, k_a, hd, hdt)


def _mmb(a, b, form="nn"):
    dot = {"nn": _dot, "nt": _dot_nt, "tn": _dot_tn}[form]
    return dot(a.astype(BF16), b.astype(BF16))


def _rwkv_scan_kernel(r_ref, lw_ref, k2_ref, v_ref, kk_ref, a_ref, s0_ref, o_ref, sT_ref, s_ref, *, c):
    @pl.when(pl.program_id(1) == 0)
    def _():
        s_ref[...] = s0_ref[0]

    hk = HEAD_A
    pairs = range(s_ref.shape[0])
    sls = [slice(p * LANE, (p + 1) * LANE) for p in pairs]
    lane = lax.broadcasted_iota(jnp.int32, (c, LANE), 1)
    m0 = lane < hk
    rowi = lax.broadcasted_iota(jnp.int32, (c, c), 0)
    coli = lax.broadcasted_iota(jnp.int32, (c, c), 1)
    ltri = (coli <= rowi).astype(BF16)
    lw = [lw_ref[0, :, sl] for sl in sls]
    cum = [_dot_x_rhs(ltri, x) for x in lw]
    e_wi = [jnp.exp(-x) for x in cum]
    kk = [kk_ref[0, :, sl].astype(F32) for sl in sls]
    at = [-k * jnp.exp(x - y) for k, x, y in zip(kk, cum, lw)]
    bt = [k * a_ref[0, :, sl].astype(F32) * e for k, sl, e in zip(kk, sls, e_wi)]
    kt = [k2_ref[0, :, sl].astype(F32) * e for sl, e in zip(sls, e_wi)]
    rt = [r_ref[0, :, sl].astype(F32) * jnp.exp(x) for sl, x in zip(sls, cum)]
    v = [v_ref[0, :, sl] for sl in sls]
    w_c = [jnp.exp(x[c - 1:c, :]) for x in cum]

    def bd(z):
        zero = jnp.zeros_like(z)
        return jnp.concatenate([jnp.where(m0, z, zero), jnp.where(m0, zero, z)], axis=0)

    prow = lax.broadcasted_iota(jnp.int32, (c, 2 * c), 0)
    pcol = lax.broadcasted_iota(jnp.int32, (c, 2 * c), 1)
    pcol = jnp.where(pcol >= c, pcol - c, pcol)
    strict = pcol < prow
    incl = pcol <= prow
    zero_cc = jnp.zeros((c, 2 * c), F32)
    x2 = [jnp.concatenate([x, y], axis=0) for x, y in zip(at, rt)]
    xb = [_mmb(x, bd(y), "nt") for x, y in zip(x2, bt)]
    xk = [_mmb(x, bd(y), "nt") for x, y in zip(x2, kt)]
    a_ab = [jnp.where(strict, x[:c], zero_cc) for x in xb]
    a_rb = [jnp.where(incl, x[c:], zero_cc) for x in xb]
    a_ak = [jnp.where(strict, x[:c], zero_cc) for x in xk]
    a_rk = [jnp.where(incl, x[c:], zero_cc) for x in xk]

    if 2 * c == LANE:
        bdc = bd
    else:
        mc = lax.broadcasted_iota(jnp.int32, (c, 2 * c), 1) < c

        def bdc(z):
            zero = jnp.zeros_like(z)
            return jnp.concatenate([jnp.where(mc, z, zero), jnp.where(mc, zero, z)], axis=0)

    eye = (pcol == prow).astype(F32)
    tmat = [eye + x for x in a_ab]
    npow = a_ab
    steps = 1
    while 2 * steps < c:
        npow = [_dot3(x, bdc(x)) for x in npow]
        tmat = [x + _dot3(x, bdc(y)) for x, y in zip(tmat, npow)]
        steps *= 2
    ta = [_mmb(x, bd(y)) for x, y in zip(tmat, at)]
    xv = [_mmb(x, bd(y)) for x, y in zip(a_ak, v)]
    tx = [_mmb(x, bd(y)) for x, y in zip(tmat, xv)]
    p_c = [x + _mmb(y, bd(z)) for x, y, z in zip(rt, a_rb, ta)]
    q_c = [_mmb(x, bd(y)) + _mmb(z, bd(u)) for x, y, z, u in zip(a_rb, tx, a_rk, v)]
    lr = lax.broadcasted_iota(jnp.int32, (LANE, LANE), 0)
    lc = lax.broadcasted_iota(jnp.int32, (LANE, LANE), 1)
    same = (lr < hk) == (lc < hk)
    eye_l = (lr == lc).astype(F32)
    zero_l = jnp.zeros((LANE, LANE), F32)
    m_c = [(eye_l + jnp.where(same, _mmb(x, y, "tn"), zero_l)) * w for x, y, w in zip(ta, bt, w_c)]
    n_c = [jnp.where(same, _mmb(x, y, "tn") + _mmb(z, u, "tn"), zero_l) * w
           for x, y, z, u, w in zip(tx, bt, v, kt, w_c)]
    s = [s_ref[p] for p in pairs]
    o = [_mmb(x, y, "nt") + z for x, y, z in zip(p_c, s, q_c)]
    s_new = [_mmb(x, y) + z for x, y, z in zip(s, m_c, n_c)]
    for p in pairs:
        o_ref[0, :, sls[p]] = o[p]
        s_ref[p] = s_new[p]
        sT_ref[0, p] = s_new[p]


def _dot_x_rhs(a, b):
    hi, lo = _split(b)
    return _dot(a, hi) + _dot(a, lo)


def _rwkv_scan(r, lw, k2, v, kk, a, s0, c):
    b, t, d_a = r.shape
    npair = d_a // LANE
    seq = pl.BlockSpec((1, c, d_a), lambda i, j: (i, j, 0))
    st = pl.BlockSpec((1, npair, LANE, LANE), lambda i, j: (i, 0, 0, 0))
    return pl.pallas_call(
        functools.partial(_rwkv_scan_kernel, c=c), grid=(b, t // c),
        in_specs=[seq] * 6 + [st], out_specs=[seq, st],
        out_shape=[jax.ShapeDtypeStruct((b, t, d_a), F32), jax.ShapeDtypeStruct((b, npair, LANE, LANE), F32)],
        scratch_shapes=[pltpu.VMEM((npair, LANE, LANE), F32)], compiler_params=_cp(2), name="rwkv_scan",
    )(r, lw, k2, v, kk, a, s0)


def _rwkv_post_kernel(o_ref, r_ref, k2_ref, v_ref, g_ref, lw_ref, lb_ref, rk_ref, hd_ref, hdt_ref, w_ref, y_ref):
    o = o_ref[...]
    hd = hd_ref[...]
    hdt = hdt_ref[...]
    inv_n = 1.0 / HEAD_A
    mean = _dot_x(_dot_x(o, hd) * inv_n, hdt)
    d = o - mean
    var = _dot_x(d * d, hd) * inv_n
    xo = d * _dot_x(lax.rsqrt(var + LNX_EPS), hdt) * lw_ref[...] + lb_ref[...]
    rk2 = r_ref[...].astype(F32) * k2_ref[...].astype(F32) * rk_ref[...]
    bonus = _dot_x(_dot_x(rk2, hd), hdt) * v_ref[...].astype(F32)
    y_ref[...] = _dot(((xo + bonus) * g_ref[...].astype(F32)).astype(BF16), w_ref[...]).astype(y_ref.dtype)


def _rwkv_post(o, r, k2, v, g, lnx_w, lnx_b, r_k, w_oa):
    m, d_a = o.shape
    n = w_oa.shape[1]
    tm = _tile(m, 256)
    hd, hdt = _head_indicator(d_a)
    row = pl.BlockSpec((tm, d_a), lambda i: (i, 0))
    full = lambda a: pl.BlockSpec(a.shape, lambda i: (0,) * a.ndim)
    return pl.pallas_call(
        _rwkv_post_kernel, grid=(m // tm,),
        in_specs=[row] * 5 + [full(lnx_w), full(lnx_b), full(r_k), full(hd), full(hdt), full(w_oa)],
        out_specs=pl.BlockSpec((tm, n), lambda i: (i, 0)), out_shape=jax.ShapeDtypeStruct((m, n), BF16),
        compiler_params=_cp(1), name="rwkv_post",
    )(o, r, k2, v, g, lnx_w, lnx_b, r_k, hd, hdt, w_oa)


def _t5_bucket_table():
    d = np.arange(LANE)
    max_exact = NUM_BUCKETS // 2
    df = np.maximum(d, 1).astype(np.float32)
    large = max_exact + (np.log(df / np.float32(max_exact)) / np.float32(math.log(MAX_DIST / max_exact))
                         * np.float32(NUM_BUCKETS - max_exact)).astype(np.int32)
    large = np.minimum(large, NUM_BUCKETS - 1)
    tab = np.where(d < max_exact, d, large)
    assert tab[FAR] == NUM_BUCKETS - 1
    return tab


PAGES_PER_STEP = 8


def _page_specs(cache, lidx, npp):
    blk = (None, None) + cache.shape[2:]
    return [pl.BlockSpec(blk, lambda i, p, pt, k=k: (lidx, pt[i, p * npp + k], 0, 0)) for k in range(npp)]


def _gelu_tanh(x):
    return 0.5 * x * (1.0 + jnp.tanh(math.sqrt(2.0 / math.pi) * (x + 0.044715 * (x * x * x))))


def _compress_finish(y, ex_ref, w1, w2_ref, o_ref):
    rr = y.shape[0]
    e = _dot(ex_ref[0].astype(BF16), w1)
    nxt = pltpu.roll(y[:, LANE:], rr - 1, 0)
    row = lax.broadcasted_iota(jnp.int32, nxt.shape, 0)
    nxt = jnp.where(row == rr - 1, e[0:1, LANE:], nxt)
    pre = y[:, :LANE] + nxt + (e[1:2, :LANE] + e[2:3, LANE:])
    o_ref[0] = _dot(_gelu_tanh(pre).astype(BF16), w2_ref[0])


def _compress_kernel(x_ref, ex_ref, w1_ref, w2_ref, o_ref):
    rr = x_ref.shape[0] // S_CMP
    w1 = w1_ref[0]
    y = jnp.zeros((rr, 2 * LANE), F32)
    for j in range(S_CMP):
        xj = x_ref[pl.ds(j, rr, stride=S_CMP), :].astype(BF16)
        y = y + _dot(xj, w1[j * LANE:(j + 1) * LANE, :])
    _compress_finish(y, ex_ref, w1, w2_ref, o_ref)


def _compress_finish_kernel(y_ref, ex_ref, w1_ref, w2_ref, o_ref):
    _compress_finish(y_ref[0], ex_ref, w1_ref[0], w2_ref, o_ref)


def _compress_paged_kernel(pt_ref, *refs, ns, ncmp):
    x_refs, (w1_ref, y_ref) = refs[:-2], refs[-2:]
    cpp = x_refs[0].shape[0] // (ns * S_CMP)
    for s in range(ncmp):
        y = jnp.zeros((len(x_refs) * cpp, 2 * LANE), F32)
        for j in range(S_CMP):
            xj = jnp.concatenate([x[pl.ds(j * ns + s, cpp, stride=S_CMP * ns), :] for x in x_refs], axis=0)
            y = y + _dot(xj.astype(BF16), w1_ref[s // N_KV, j * LANE:(j + 1) * LANE, :])
        y_ref[0, s] = y


def _compress(x, x_spec, n_pos, ex, w1cat, w2):
    nb, _, cw = ex.shape
    rr = n_pos // S_CMP
    c_of = lambda i: (i // N_KV) % 2
    return pl.pallas_call(
        _compress_kernel, grid=(nb,),
        in_specs=[x_spec, pl.BlockSpec((1, SUBLANE, cw), lambda i: (i, 0, 0)),
                  pl.BlockSpec((1, cw, 2 * LANE), lambda i: (c_of(i), 0, 0)),
                  pl.BlockSpec((1, LANE, LANE), lambda i: (c_of(i), 0, 0))],
        out_specs=pl.BlockSpec((1, rr, LANE), lambda i: (i, 0, 0)),
        out_shape=jax.ShapeDtypeStruct((nb, rr, LANE), F32), compiler_params=_cp(1), name="nsa_compress",
    )(x, ex, w1cat, w2)


def _compress_paged(cache, lidx, page_table, ns, ex, w1cat, w2):
    b, n_pages = page_table.shape
    ncmp = ex.shape[0] // b
    cw = ex.shape[2]
    page = cache.shape[2] // ns
    npp = next(k for k in (PAGES_PER_STEP, 4, 2, 1) if n_pages % k == 0)
    rr = n_pages * page // S_CMP
    rt = npp * page // S_CMP
    grid_spec = pltpu.PrefetchScalarGridSpec(
        num_scalar_prefetch=1, grid=(b, n_pages // npp),
        in_specs=_page_specs(cache, lidx, npp) + [pl.BlockSpec(w1cat.shape, lambda i, p, pt: (0, 0, 0))],
        out_specs=pl.BlockSpec((1, ncmp, rt, 2 * LANE), lambda i, p, pt: (i, 0, p, 0)))
    y = pl.pallas_call(
        functools.partial(_compress_paged_kernel, ns=ns, ncmp=ncmp), grid_spec=grid_spec,
        out_shape=jax.ShapeDtypeStruct((b, ncmp, rr, 2 * LANE), F32),
        compiler_params=_cp(2), name="nsa_compress_paged",
    )(page_table, *([cache] * npp), w1cat)
    c_of = lambda i: (i // N_KV) % 2
    return pl.pallas_call(
        _compress_finish_kernel, grid=(b * ncmp,),
        in_specs=[pl.BlockSpec((1, rr, 2 * LANE), lambda i: (i, 0, 0)), pl.BlockSpec((1, SUBLANE, cw), lambda i: (i, 0, 0)),
                  pl.BlockSpec((1, cw, 2 * LANE), lambda i: (c_of(i), 0, 0)),
                  pl.BlockSpec((1, LANE, LANE), lambda i: (c_of(i), 0, 0))],
        out_specs=pl.BlockSpec((1, rr, LANE), lambda i: (i, 0, 0)),
        out_shape=jax.ShapeDtypeStruct((b * ncmp, rr, LANE), F32), compiler_params=_cp(1), name="nsa_compress_finish",
    )(y.reshape(b * ncmp, rr, 2 * LANE), ex, w1cat, w2)


def _bias_gather(tab_row, dist):
    idx = jnp.clip(dist, 0, FAR)
    return jnp.take_along_axis(jnp.broadcast_to(tab_row, idx.shape), idx, axis=1)


def _stack_heads(q_ref, col0=0, hpg=None):
    hpg = q_ref.shape[1] // HEAD_B if hpg is None else hpg
    return jnp.concatenate([q_ref[:, col0 + h * HEAD_B:col0 + (h + 1) * HEAD_B] for h in range(hpg)], axis=0)


def _rel_bias(dist, g, hpg, tab_ref):
    rel = []
    for h in range(hpg):
        tab_row = tab_ref[pl.ds(g * hpg + h, 1), :]
        far = tab_row[:, FAR:FAR + 1]
        rel.append(jnp.concatenate([_bias_gather(tab_row, dist[:, c * LANE:(c + 1) * LANE]) - far
                                    for c in range(dist.shape[1] // LANE)], axis=1))
    return jnp.stack(rel)


def _online_softmax_update(qs, tiles):
    hpg, tq, _ = tiles[0][4].shape
    scores = [_dot_nt(qs, kb) for kb, *_ in tiles]
    probs, alphas = [], []
    for s, (kb, vb, mask, rel, m_ref, l_ref, acc_ref) in zip(scores, tiles):
        width = kb.shape[0]
        s = s.reshape(hpg, tq, width) * (HEAD_B ** -0.5)
        if rel is None:
            s = s + jnp.where(mask, 0.0, 2 * NEG)[None]
        else:
            s = s + jnp.where(mask[None], rel, 2 * NEG)
        m_prev = m_ref[...][:, :, :1]
        m_new = jnp.maximum(m_prev, jnp.max(s, axis=2, keepdims=True))
        alpha = jnp.exp(m_prev - m_new)
        p = jnp.exp(s - m_new)
        m_ref[...] = jnp.broadcast_to(m_new, m_ref.shape)
        probs.append(p.reshape(hpg * tq, width).astype(BF16))
        alphas.append(alpha)
    pvs = [_dot(p, jnp.concatenate([t[1], jnp.ones_like(t[1])], axis=1)) for p, t in zip(probs, tiles)]
    for pv, alpha, t in zip(pvs, alphas, tiles):
        pv = pv.reshape(hpg, tq, 2 * HEAD_B)
        t[5][...] = alpha * t[5][...] + pv[:, :, HEAD_B:]
        t[6][...] = alpha * t[6][...] + pv[:, :, :HEAD_B]


def _cmp_attn_kernel(q_ref, kc_ref, vc_ref, ovt_ref, tab_ref, o_ref, imp_ref, *, tq, q0, lanes):
    g = pl.program_id(1)
    qt = pl.program_id(2)
    hpg = q_ref.shape[2] // HEAD_B
    rr = kc_ref.shape[1]
    kc = kc_ref[0].astype(BF16)
    vc = vc_ref[0].astype(BF16)
    qbase = q0 + qt * tq
    qrow = qbase + lax.broadcasted_iota(jnp.int32, (tq, LANE), 0)
    dists = []
    for cix in range(rr // LANE):
        c_end = S_CMP * (cix * LANE + lax.broadcasted_iota(jnp.int32, (tq, LANE), 1)) + (L_CMP - 1)
        dists.append(qrow - c_end)
    mask = (jnp.concatenate(dists, axis=1) >= 0)[None]
    bias = jnp.stack([jnp.concatenate([_bias_gather(tab_ref[pl.ds(g * hpg + h, 1), :], d) for d in dists], axis=1)
                      for h in range(hpg)])
    s = _dot_nt(_stack_heads(q_ref.at[0]), kc).reshape(hpg, tq, rr) * (HEAD_B ** -0.5) + bias
    s = jnp.where(mask, s, NEG)
    e = jnp.where(mask, jnp.exp(s - jnp.max(s, axis=2, keepdims=True)), 0.0)
    p = e / jnp.maximum(jnp.sum(e, axis=2, keepdims=True), 1e-30)
    o = _dot(p.reshape(hpg * tq, rr).astype(BF16), vc).reshape(hpg, tq, HEAD_B)
    o_ref[0] = jnp.concatenate([o[h] for h in range(hpg)], axis=1)
    psum = jnp.sum(p, axis=0)
    if tq < lanes:
        psum = jnp.concatenate([psum, jnp.zeros((lanes - tq, rr), F32)], axis=0)
    ph, plo = _split(psum)
    ovt = ovt_ref[...]
    imp_ref[0, 0] = _dot_nt(ovt, ph) + _dot_nt(ovt, plo)


def _select_kernel(imp_ref, qpos_ref, selt_ref, *, n_top):
    nsr, lanes = imp_ref.shape[2:]
    j = lax.broadcasted_iota(jnp.int32, (nsr, lanes), 0)
    cur = qpos_ref[...] // L_SEL
    valid = j <= cur
    forced = (j == 0) | (j == cur) | (j == cur - 1)
    score = jnp.where(valid, imp_ref[0, 0] + jnp.where(forced, FORCE_BONUS, 0.0), NEG)

    def body(_, carry):
        score, sel = carry
        top = jnp.max(score, axis=0, keepdims=True)
        first = jnp.min(jnp.where(score == top, j, nsr), axis=0, keepdims=True)
        pick = j == first
        sel = jnp.where(pick & (top > NEG / 2), 1.0, sel)
        return jnp.where(pick, 2 * NEG, score), sel

    _, sel = lax.fori_loop(0, n_top, body, (score, jnp.zeros((nsr, lanes), F32)))
    selt_ref[0, 0] = sel


def _select(imp, qpos, ns):
    x, gg, nsr, l = imp.shape
    lt = next(c for c in (256, 128) if l % c == 0)
    blk = pl.BlockSpec((1, 1, nsr, lt), lambda i, g, t: (i, g, 0, t))
    return pl.pallas_call(
        functools.partial(_select_kernel, n_top=min(TOP_N, ns)), grid=(x, gg, l // lt),
        in_specs=[blk, pl.BlockSpec((1, lt), lambda i, g, t: (0, t))], out_specs=blk,
        out_shape=jax.ShapeDtypeStruct(imp.shape, F32), compiler_params=_cp(3), name="nsa_select",
    )(imp, qpos)


def _cmp_attn(q, kcv, tab, *, t_keys, q0, tq):
    b, tqa, d_b = q.shape
    rr = kcv.shape[1]
    gw = d_b // N_KV
    ns = -(-t_keys // L_SEL)
    nsr = _round_up(ns, SUBLANE)
    lanes = max(tq, LANE)
    nqt = tqa // tq
    ci = np.arange(rr)
    sj = np.arange(nsr)
    ov = ((S_CMP * ci) // L_SEL)[None, :] == sj[:, None]
    ov |= ((S_CMP * ci + L_CMP - 1) // L_SEL)[None, :] == sj[:, None]
    ovt = jnp.asarray(ov.astype(np.float32), BF16)
    kern = functools.partial(_cmp_attn_kernel, tq=tq, q0=q0, lanes=lanes)
    return pl.pallas_call(
        kern, grid=(b, N_KV, nqt),
        in_specs=[pl.BlockSpec((1, tq, gw), lambda i, g, t: (i, t, g)),
                  pl.BlockSpec((1, rr, LANE), lambda i, g, t: (i * 4 + g, 0, 0)),
                  pl.BlockSpec((1, rr, LANE), lambda i, g, t: (i * 4 + N_KV + g, 0, 0)),
                  pl.BlockSpec(ovt.shape, lambda i, g, t: (0, 0)),
                  pl.BlockSpec(tab.shape, lambda i, g, t: (0, 0))],
        out_specs=[pl.BlockSpec((1, tq, gw), lambda i, g, t: (i, t, g)),
                   pl.BlockSpec((1, 1, nsr, lanes), lambda i, g, t: (i, g, 0, t))],
        out_shape=[jax.ShapeDtypeStruct((b, tqa, d_b), F32),
                   jax.ShapeDtypeStruct((b, N_KV, nsr, nqt * lanes), F32)],
        compiler_params=_cp(3), name="nsa_cmp_attn",
    )(q, kcv, kcv, ovt, tab)


def _flash_kernel(*refs, mode, band, tq, tk, nj, q0, kbase, n_tail, tail_pos0):
    refs = list(refs)
    q_ref, k_ref, v_ref = refs[:3]
    pos = 3
    if n_tail:
        kt_ref, vt_ref = refs[pos:pos + 2]
        pos += 2
    if mode == "sel":
        sel_ref = refs[pos]
        pos += 1
    tab_ref, o_ref, m_ref, l_ref, acc_ref = refs[pos:pos + 5]
    g = pl.program_id(1)
    qt = pl.program_id(2)
    j = pl.program_id(3)
    hpg = q_ref.shape[1] // HEAD_B
    qbase = q0 + qt * tq
    qs = _stack_heads(q_ref)

    @pl.when(j == 0)
    def _():
        m_ref[...] = jnp.full_like(m_ref, NEG)
        l_ref[...] = jnp.zeros_like(l_ref)
        acc_ref[...] = jnp.zeros_like(acc_ref)

    def update(k_r, v_r, kpos0, width, n_valid, near):
        qrow = qbase + lax.broadcasted_iota(jnp.int32, (tq, width), 0)
        kcol = kpos0 + lax.broadcasted_iota(jnp.int32, (tq, width), 1)
        dist = qrow - kcol
        mask = dist >= 0
        if n_valid is not None:
            mask &= lax.broadcasted_iota(jnp.int32, (tq, width), 1) < n_valid
        if mode == "win":
            mask &= dist < WINDOW
        else:
            nsp = sel_ref.shape[1]
            blk = (kpos0 + lax.broadcasted_iota(jnp.int32, (nsp, width), 1)) // L_SEL
            expand = (blk == lax.broadcasted_iota(jnp.int32, (nsp, width), 0)).astype(BF16)
            mask &= _dot(sel_ref[...].astype(BF16), expand) > 0.5
        kb = k_r[...].astype(BF16)
        vb = v_r[...].astype(BF16)
        rel = _rel_bias(dist, g, hpg, tab_ref) if near else None
        _online_softmax_update(qs, [(kb, vb, mask, rel, m_ref, l_ref, acc_ref)])

    if band:
        kt_abs = qt - (nj - 1) + j
        active = kt_abs >= 0
    else:
        kt_abs = j
        active = kbase + j * tk <= qbase + tq - 1
    kpos0 = kbase + kt_abs * tk
    is_far = qbase - (kpos0 + tk - 1) >= FAR

    @pl.when(active & is_far)
    def _():
        update(k_ref, v_ref, kpos0, tk, None, False)

    @pl.when(active & jnp.logical_not(is_far))
    def _():
        update(k_ref, v_ref, kpos0, tk, None, True)

    @pl.when(j == nj - 1)
    def _():
        if n_tail:
            update(kt_ref, vt_ref, tail_pos0, kt_ref.shape[0], n_tail, True)
        outs = [acc_ref[h] / jnp.maximum(l_ref[h][:, :1], 1e-30) for h in range(hpg)]
        o_ref[...] = jnp.concatenate(outs, axis=1)


def _sel_paged_kernel(pt_ref, *refs, ns, n_tail, p_len):
    npp = len(refs) - 8
    q_ref, x_refs = refs[0], refs[1:1 + npp]
    tail_ref, sel_ref, tab_ref, o_ref, m_ref, l_ref, acc_ref = refs[1 + npp:]
    j = pl.program_id(1)
    nj = pl.num_programs(1)
    tq = q_ref.shape[0]
    gw = q_ref.shape[1] // N_KV
    hpg = gw // HEAD_B
    page = x_refs[0].shape[0] // ns
    width = npp * page

    @pl.when(j == 0)
    def _():
        m_ref[...] = jnp.full_like(m_ref, NEG)
        l_ref[...] = jnp.zeros_like(l_ref)
        acc_ref[...] = jnp.zeros_like(acc_ref)

    def fold(g, kb, vb, kpos0, n_valid, near):
        w = kb.shape[0]
        col = lax.broadcasted_iota(jnp.int32, (tq, w), 1)
        dist = p_len + lax.broadcasted_iota(jnp.int32, (tq, w), 0) - (kpos0 + col)
        mask = dist >= 0
        if n_valid is not None:
            mask &= col < n_valid
        nsp = sel_ref.shape[2]
        blk = (kpos0 + lax.broadcasted_iota(jnp.int32, (nsp, w), 1)) // L_SEL
        expand = (blk == lax.broadcasted_iota(jnp.int32, (nsp, w), 0)).astype(BF16)
        mask &= _dot(sel_ref[g].astype(BF16), expand) > 0.5
        rel = _rel_bias(dist, g, hpg, tab_ref) if near else None
        _online_softmax_update(_stack_heads(q_ref, g * gw, hpg),
                               [(kb, vb, mask, rel, m_ref.at[g], l_ref.at[g], acc_ref.at[g])])

    def pages(stream):
        return jnp.concatenate([x[pl.ds(stream, page, stride=ns), :] for x in x_refs], axis=0).astype(BF16)

    def main(near):
        for g in range(N_KV):
            fold(g, pages(2 * N_KV + g), pages(3 * N_KV + g), j * width, None, near)

    @pl.when(j < nj - 1)
    def _():
        main(False)

    @pl.when(j == nj - 1)
    def _():
        main(True)
        for g in range(N_KV):
            kt = tail_ref[:, (2 * N_KV + g) * LANE:(2 * N_KV + g + 1) * LANE].astype(BF16)
            vt = tail_ref[:, (3 * N_KV + g) * LANE:(3 * N_KV + g + 1) * LANE].astype(BF16)
            fold(g, kt, vt, p_len, n_tail, True)
        out = acc_ref[...] / jnp.maximum(l_ref[...][:, :, :, :1], 1e-30)
        o_ref[...] = jnp.concatenate([out[g, h] for g in range(N_KV) for h in range(hpg)], axis=1)


def _sel_paged(q, cache, lidx, page_table, ns, rows_t, n_tail, sel, tab):
    b, tq, d_b = q.shape
    n_pages = page_table.shape[1]
    page = cache.shape[2] // ns
    npp = next(k for k in (PAGES_PER_STEP, 4, 2, 1) if n_pages % k == 0)
    assert npp * page > FAR
    hpg = d_b // N_KV // HEAD_B
    whole = lambda a: pl.BlockSpec((None,) + a.shape[1:], lambda i, p, pt: (i,) + (0,) * (a.ndim - 1))
    grid_spec = pltpu.PrefetchScalarGridSpec(
        num_scalar_prefetch=1, grid=(b, n_pages // npp),
        in_specs=[whole(q)] + _page_specs(cache, lidx, npp) + [whole(rows_t), whole(sel),
                  pl.BlockSpec(tab.shape, lambda i, p, pt: (0, 0))],
        out_specs=pl.BlockSpec((None, tq, d_b), lambda i, p, pt: (i, 0, 0)),
        scratch_shapes=[pltpu.VMEM((N_KV, hpg, tq, LANE), F32), pltpu.VMEM((N_KV, hpg, tq, LANE), F32),
                        pltpu.VMEM((N_KV, hpg, tq, HEAD_B), F32)])
    return pl.pallas_call(
        functools.partial(_sel_paged_kernel, ns=ns, n_tail=n_tail, p_len=n_pages * page), grid_spec=grid_spec,
        out_shape=jax.ShapeDtypeStruct((b, tq, d_b), F32), compiler_params=_cp(2), name="nsa_sel_attn_paged",
    )(page_table, q, *([cache] * npp), rows_t, sel, tab)


def _flash(q, k_arr, k_spec, v_arr, v_spec, tab, *, mode, band, tq, tk, nj, q0, kbase,
           tail=None, sel=None, name):
    b, tqa, d_b = q.shape
    gw = d_b // N_KV
    hpg = gw // HEAD_B
    nqt = tqa // tq
    args = [q, k_arr, v_arr]
    in_specs = [pl.BlockSpec((None, tq, gw), lambda i, g, t, j: (i, t, g)), k_spec, v_spec]
    n_tail, tail_pos0 = 0, 0
    if tail is not None:
        tk_arr, tk_spec, tv_arr, tv_spec, n_tail, tail_pos0 = tail
        args += [tk_arr, tv_arr]
        in_specs += [tk_spec, tv_spec]
    if mode == "sel":
        args.append(sel)
        in_specs.append(pl.BlockSpec((None, None, tq, sel.shape[3]), lambda i, g, t, j: (i, g, t, 0)))
    args.append(tab)
    in_specs.append(pl.BlockSpec(tab.shape, lambda i, g, t, j: (0, 0)))
    kern = functools.partial(_flash_kernel, mode=mode, band=band, tq=tq, tk=tk, nj=nj, q0=q0, kbase=kbase,
                             n_tail=n_tail, tail_pos0=tail_pos0)
    return pl.pallas_call(
        kern, grid=(b, N_KV, nqt, nj), in_specs=in_specs,
        out_specs=pl.BlockSpec((None, tq, gw), lambda i, g, t, j: (i, t, g)),
        out_shape=jax.ShapeDtypeStruct((b, tqa, d_b), F32),
        scratch_shapes=[pltpu.VMEM((hpg, tq, LANE), F32), pltpu.VMEM((hpg, tq, LANE), F32),
                        pltpu.VMEM((hpg, tq, HEAD_B), F32)],
        compiler_params=_cp(4), name=name,
    )(*args)


def _attn_res_kernel(*refs, mode, tq, ch, far_w):
    if mode == "sel":
        q_ref, k_ref, v_ref, sel_ref, tab_ref, o_ref, m_ref, l_ref, acc_ref = refs
    else:
        q_ref, k_ref, v_ref, tab_ref, o_ref, m_ref, l_ref, acc_ref = refs
    g = pl.program_id(1)
    qt = pl.program_id(2)
    hpg = q_ref.shape[1] // HEAD_B
    qbase = qt * tq
    near0 = qbase - LANE
    qs = _stack_heads(q_ref)
    m_ref[...] = jnp.full_like(m_ref, NEG)
    l_ref[...] = jnp.zeros_like(l_ref)
    acc_ref[...] = jnp.zeros_like(acc_ref)

    def tile(stream, kpos0, width, near):
        qrow = qbase + lax.broadcasted_iota(jnp.int32, (tq, width), 0)
        kcol = kpos0 + lax.broadcasted_iota(jnp.int32, (tq, width), 1)
        dist = qrow - kcol
        mask = (dist >= 0) if near else (kcol < near0)
        if mode == "win":
            mask &= dist < WINDOW
        else:
            nsp = sel_ref.shape[1]
            blk = (kpos0 + lax.broadcasted_iota(jnp.int32, (nsp, width), 1)) // L_SEL
            expand = (blk == lax.broadcasted_iota(jnp.int32, (nsp, width), 0)).astype(BF16)
            mask &= _dot(sel_ref[...].astype(BF16), expand) > 0.5
        kb = k_ref[pl.ds(kpos0, width), :].astype(BF16)
        vb = v_ref[pl.ds(kpos0, width), :].astype(BF16)
        rel = _rel_bias(dist, g, hpg, tab_ref) if near else None
        return (kb, vb, mask, rel, m_ref.at[stream], l_ref.at[stream], acc_ref.at[stream])

    near = lambda: tile(1, pl.multiple_of(jnp.maximum(near0, 0), LANE), 2 * LANE, True)
    if mode == "sel":
        def body(c, carry):
            _online_softmax_update(qs, [tile(0, pl.multiple_of(c * ch, ch), ch, False)])
            return carry

        lax.fori_loop(0, (jnp.maximum(near0, 0) + ch - 1) // ch, body, 0)
        _online_softmax_update(qs, [near()])
    elif far_w:
        far = tile(0, pl.multiple_of(jnp.maximum(qbase - WINDOW, 0), LANE), far_w, False)
        _online_softmax_update(qs, [far, near()])
    else:
        _online_softmax_update(qs, [near()])
    m0, m1 = m_ref[0][:, :, :1], m_ref[1][:, :, :1]
    m = jnp.maximum(m0, m1)
    w0, w1 = jnp.exp(m0 - m), jnp.exp(m1 - m)
    l = w0 * l_ref[0][:, :, :1] + w1 * l_ref[1][:, :, :1]
    out = (w0 * acc_ref[0] + w1 * acc_ref[1]) / jnp.maximum(l, 1e-30)
    o_ref[...] = jnp.concatenate([out[h] for h in range(hpg)], axis=1)


def _attn_res(q, kv, kcol, vcol, tab, *, mode, sel=None, name):
    b, t, d_b = q.shape
    gw = d_b // N_KV
    hpg = gw // HEAD_B
    tq = LANE
    assert t % tq == 0 and t >= 2 * LANE
    ch = next(c for c in (1024, 512, 256, 128) if t % c == 0)
    far_w = min(WINDOW - LANE, t - 2 * LANE)
    args = [q, kv, kv]
    in_specs = [pl.BlockSpec((None, tq, gw), lambda i, g, qt: (i, qt, g)),
                pl.BlockSpec((None, t, LANE), lambda i, g, qt: (i, 0, kcol + g)),
                pl.BlockSpec((None, t, LANE), lambda i, g, qt: (i, 0, vcol + g))]
    if mode == "sel":
        args.append(sel)
        in_specs.append(pl.BlockSpec((None, None, tq, sel.shape[3]), lambda i, g, qt: (i, g, qt, 0)))
    args.append(tab)
    in_specs.append(pl.BlockSpec(tab.shape, lambda i, g, qt: (0, 0)))
    return pl.pallas_call(
        functools.partial(_attn_res_kernel, mode=mode, tq=tq, ch=ch, far_w=far_w),
        grid=(b, N_KV, t // tq), in_specs=in_specs,
        out_specs=pl.BlockSpec((None, tq, gw), lambda i, g, qt: (i, qt, g)),
        out_shape=jax.ShapeDtypeStruct((b, t, d_b), F32),
        scratch_shapes=[pltpu.VMEM((2, hpg, tq, LANE), F32), pltpu.VMEM((2, hpg, tq, LANE), F32),
                        pltpu.VMEM((2, hpg, tq, HEAD_B), F32)],
        compiler_params=_cp(3), name=name,
    )(*args)


def _nsa_out_kernel(oc_ref, os_ref, ow_ref, gl_ref, e_ref, w_ref, y_ref):
    gate = _sigmoid(gl_ref[...])
    o = (_dot_x(gate, e_ref[0]) * oc_ref[...] + _dot_x(gate, e_ref[1]) * os_ref[...]
         + _dot_x(gate, e_ref[2]) * ow_ref[...])
    y_ref[...] = _dot(o.astype(BF16), w_ref[...]).astype(y_ref.dtype)


def _nsa_out(o_c, o_s, o_w, gl, w_ob):
    m, d_b = o_c.shape
    n = w_ob.shape[1]
    h_b = d_b // HEAD_B
    tm = _tile(m, 256)
    e = np.zeros((3, LANE, d_b), np.float32)
    for br in range(3):
        for hh in range(h_b):
            e[br, br * h_b + hh, hh * HEAD_B:(hh + 1) * HEAD_B] = 1.0
    e = jnp.asarray(e, BF16)
    row = pl.BlockSpec((tm, d_b), lambda i: (i, 0))
    return pl.pallas_call(
        _nsa_out_kernel, grid=(m // tm,),
        in_specs=[row, row, row, pl.BlockSpec((tm, LANE), lambda i: (i, 0)),
                  pl.BlockSpec(e.shape, lambda i: (0, 0, 0)), pl.BlockSpec(w_ob.shape, lambda i: (0, 0))],
        out_specs=pl.BlockSpec((tm, n), lambda i: (i, 0)), out_shape=jax.ShapeDtypeStruct((m, n), BF16),
        compiler_params=_cp(1), name="nsa_out",
    )(o_c, o_s, o_w, gl, e, w_ob)


def _nsa(q, rows, win, gl, tab, lw, *, t_real, paged, lidx, cwin):
    b, tqa, d_b = q.shape
    t = t_real
    ncmp = 2 * N_KV
    nstream = 4 * N_KV
    cw = S_CMP * LANE
    p_len = 0 if paged is None else paged[1].shape[1] * (paged[0].shape[2] // nstream)
    l_tot = p_len + t
    assert p_len % (S_CMP * SUBLANE) == 0 and (p_len == 0 or t <= S_CMP) and (p_len > 0 or t % LANE == 0)
    pe = jnp.repeat(lw["cmp_pe"].reshape(2, 1, 2, cw), N_KV, axis=1)
    pe = jnp.broadcast_to(pe[None], (b, 2, N_KV, 2, cw)).reshape(b * ncmp, 2, cw)
    pad_rows = jnp.zeros((b * ncmp, SUBLANE - 3, cw), F32)
    if paged is None:
        ex = jnp.concatenate([jnp.zeros((b * ncmp, 1, cw), F32), pe, pad_rows], axis=1)
        x_spec = pl.BlockSpec((None, t, LANE), lambda i: (i // ncmp, 0, i % ncmp))
        kcv = _compress(rows, x_spec, t, ex, lw["w1cat"], lw["w2"])
    else:
        new_cmp = rows[:, :, :ncmp * LANE].reshape(b, t, ncmp, LANE).transpose(0, 2, 1, 3)
        ex0 = jnp.pad(new_cmp.reshape(b * ncmp, 1, t * LANE), ((0, 0), (0, 0), (0, cw - t * LANE)))
        ex = jnp.concatenate([ex0, pe, pad_rows], axis=1)
        kcv = _compress_paged(paged[0], lidx, paged[1], nstream, ex, lw["w1cat"], lw["w2"])
    if kcv.shape[1] % LANE:
        kcv = jnp.pad(kcv, ((0, 0), (0, _round_up(kcv.shape[1], LANE) - kcv.shape[1]), (0, 0)))
    tq = tqa if p_len else LANE
    o_c, imp = _cmp_attn(q, kcv, tab, t_keys=l_tot, q0=p_len, tq=tq)
    ns = -(-l_tot // L_SEL)
    nsp = _round_up(ns, LANE)
    if p_len == 0:
        sel = jnp.swapaxes(_select(imp, jnp.arange(tqa, dtype=jnp.int32)[None], ns), 2, 3)
    else:
        nq = b * tqa
        nql = _round_up(nq, LANE)
        nsr = imp.shape[2]
        impl = jnp.pad(imp[..., :tqa].transpose(1, 2, 0, 3).reshape(1, N_KV, nsr, nq),
                       ((0, 0), (0, 0), (0, 0), (0, nql - nq)))
        qpos = (p_len + jnp.arange(nql, dtype=jnp.int32) % tqa)[None]
        sel = _select(impl, qpos, ns)[0, :, :, :nq].reshape(N_KV, nsr, b, tqa).transpose(2, 0, 3, 1)
    sel = jnp.pad(sel, ((0, 0), (0, 0), (0, 0), (0, nsp - sel.shape[3])))
    if paged is None:
        o_s = _attn_res(q, rows, 2 * N_KV, 3 * N_KV, tab, mode="sel", sel=sel, name="nsa_sel_attn")
        o_w = _attn_res(q, win, 0, N_KV, tab, mode="win", name="nsa_win_attn")
    else:
        rows_t = jnp.pad(rows, ((0, 0), (0, LANE - t), (0, 0)))
        win_t = jnp.pad(win, ((0, 0), (0, LANE - t), (0, 0)))
        tails = lambda col: pl.BlockSpec((None, LANE, LANE), lambda i, g, qt, j, col=col: (i, 0, col + g))
        o_s = _sel_paged(q, paged[0], lidx, paged[1], nstream, rows_t, t, sel, tab)
        wb = cwin.shape[1]
        kw = lambda col: pl.BlockSpec((None, wb, LANE), lambda i, g, qt, j, col=col: (i, 0, col + g))
        o_w = _flash(q, cwin, kw(0), cwin, kw(N_KV), tab, mode="win", band=False, tq=tq, tk=wb,
                     nj=1, q0=p_len, kbase=p_len - wb,
                     tail=(win_t, tails(0), win_t, tails(N_KV), t, p_len), name="nsa_win_attn_cached")
    m = b * tqa
    return _nsa_out(o_c.reshape(m, d_b), o_s.reshape(m, d_b), o_w.reshape(m, d_b), gl, lw["w_ob"])


def _layer_weights(l, w_in, mu, w0, w_up, a0, a_up, g_up, k_k, k_a, r_k, lnx_w, lnx_b, w_oa,
                   cmp_pe, cmp_w1, cmp_w2, w_ob, w_o, w_ff_up, w_ff_down, norm_g):
    d_a = w_oa.shape[1]
    d_b = w_ob.shape[1]
    d = w_o.shape[1]
    rw = 3 * d_a + DECAY_LORA + A_LORA + GATE_LORA
    pw = 3 * d_a + LORA_W
    nrow = 4 * N_KV * HEAD_B
    nwin = 2 * N_KV * HEAD_B
    ngl = 3 * (d_b // HEAD_B)
    wi = w_in[l]
    o_q = rw
    o_rows = o_q + d_b
    o_win = o_rows + nrow
    o_gl = o_win + nwin
    o_pg = o_gl + ngl
    wl = jnp.zeros((LORA_W, 3 * d_a), F32)
    wl = wl.at[:DECAY_LORA, :d_a].set(w_up[l])
    wl = wl.at[DECAY_LORA:DECAY_LORA + A_LORA, d_a:2 * d_a].set(a_up[l])
    wl = wl.at[DECAY_LORA + A_LORA:DECAY_LORA + A_LORA + GATE_LORA, 2 * d_a:].set(g_up[l])
    half = S_CMP * HEAD_B
    return dict(
        w_pa=jnp.pad(wi[:, :rw], ((0, 0), (0, pw - rw))).astype(BF16),
        w_q=wi[:, o_q:o_rows].astype(BF16),
        w_rows=wi[:, o_rows:o_win].astype(BF16),
        w_win=wi[:, o_win:o_gl].astype(BF16),
        w_gl=jnp.pad(wi[:, o_gl:o_pg], ((0, 0), (0, LANE - ngl))).astype(BF16),
        w_pg=wi[:, o_pg:o_pg + 2 * d].astype(BF16),
        mu=jnp.pad(mu[l], (0, pw - rw))[None], wl=wl.astype(BF16),
        w0=w0[l][None], a0=a0[l][None], k_k=k_k[l][None], k_a=k_a[l][None],
        r_k=r_k[l].reshape(1, d_a), lnx_w=lnx_w[l][None], lnx_b=lnx_b[l][None],
        w_oa=w_oa[l].astype(BF16), w_ob=w_ob[l].astype(BF16), w_o=w_o[l].astype(BF16),
        w_up=w_ff_up[l].astype(BF16), w_down=w_ff_down[l].astype(BF16),
        cmp_pe=cmp_pe[l],
        w1cat=jnp.concatenate([cmp_w1[l][:, :half], cmp_w1[l][:, half:]], axis=2).astype(BF16),
        w2=cmp_w2[l].astype(BF16), g=norm_g[l], rw=rw,
    )


def _layer(x, xn, lw, g_next, tab, shift0, s0, paged, lidx, cwin):
    b, t, d = x.shape
    m = b * t
    d_a = lw["w_oa"].shape[0]
    x2 = x.reshape(m, d)
    g = lw["g"]
    pa = _matmul(xn, lw["w_pa"], F32, name="proj_rwkv")
    q = _matmul(xn, lw["w_q"], BF16, name="proj_q")
    rows = _matmul(xn, lw["w_rows"], F32, name="proj_rows")
    win = _matmul(xn, lw["w_win"], F32, name="proj_win")
    gl = _matmul(xn, lw["w_gl"], F32, name="proj_gl")
    pg = _matmul(xn, lw["w_pg"], BF16, name="proj_pg")
    pw = pa.shape[1]
    d_b = q.shape[1]
    tp = t if t % RWKV_CHUNK == 0 else _round_up(t, SUBLANE)
    c = RWKV_CHUNK if t % RWKV_CHUNK == 0 else tp
    pa3 = pa.reshape(b, t, pw)
    pa_p = pa3 if tp == t else jnp.pad(pa3, ((0, 0), (0, tp - t), (0, 0)))
    prev = jnp.pad(shift0, ((0, 0), (0, pw - shift0.shape[1])))[:, None]
    r, lgw, k2, v, kk, a, gg = _rwkv_prep(pa_p, prev, lw["mu"], lw["wl"], lw["w0"], lw["a0"], lw["k_k"], lw["k_a"],
                                          None if tp == t else t)
    o, s_fin = _rwkv_scan(r, lgw, k2, v, kk, a, s0, c)
    flat = lambda u: u.reshape(b * tp, d_a)
    ya = _rwkv_post(flat(o), flat(r), flat(k2), flat(v), flat(gg), lw["lnx_w"], lw["lnx_b"], lw["r_k"], lw["w_oa"])
    if tp != t:
        ya = ya.reshape(b, tp, d)[:, :t].reshape(m, d)
    sh = pa3[:, t - 1, :lw["rw"]]
    tqa = t if t % LANE == 0 else _round_up(t, SUBLANE)
    q3 = q.reshape(b, t, d_b)
    gl_p = gl
    if tqa != t:
        q3 = jnp.pad(q3, ((0, 0), (0, tqa - t), (0, 0)))
        gl_p = jnp.pad(gl.reshape(b, t, LANE), ((0, 0), (0, tqa - t), (0, 0))).reshape(b * tqa, LANE)
    rows3 = rows.reshape(b, t, rows.shape[1])
    win3 = win.reshape(b, t, win.shape[1])
    yb = _nsa(q3, rows3, win3, gl_p, tab, lw, t_real=t, paged=paged, lidx=lidx, cwin=cwin)
    if tqa != t:
        yb = yb.reshape(b, tqa, d)[:, :t].reshape(m, d)
    x1, xn1 = _merge(ya, yb, pg, x2, lw["w_o"], g[1:3])
    h = _matmul(xn1, lw["w_up"], BF16, relu2=True, name="ffn_up")
    x_out, xn_out = _ffn_down(h, lw["w_down"], x1, jnp.concatenate([g[3:4], g_next], axis=0))
    wctx = win3 if cwin is None else jnp.concatenate([cwin, win3], axis=1)
    n_keep = min(WINDOW, wctx.shape[1])
    return x_out.reshape(b, t, d), xn_out, rows3, wctx[:, wctx.shape[1] - n_keep:], sh, s_fin


def _pair_states(s):
    b, h, n, _ = s.shape
    s = s.reshape(b, h // 2, 2, n, n)
    z = jnp.zeros((b, h // 2, n, n), s.dtype)
    top = jnp.concatenate([s[:, :, 0], z], axis=3)
    bot = jnp.concatenate([z, s[:, :, 1]], axis=3)
    return jnp.concatenate([top, bot], axis=2)


def _unpair_states(s):
    n = HEAD_A
    b, hp = s.shape[:2]
    return jnp.stack([s[:, :, :n, :n], s[:, :, n:, n:]], axis=2).reshape(b, 2 * hp, n, n)


def _trunk(x, shift0, wkv0, paged, cache_win, tab, layers):
    rows, wins, shifts, wkvs = [], [], [], []
    xn = _rmsnorm_cast(x.reshape(-1, x.shape[2]), layers[0]["g"][0:1])
    for l, lw in enumerate(layers):
        g_next = layers[l + 1]["g"][0:1] if l + 1 < len(layers) else lw["g"][3:4]
        x, xn, nr, nw, sh, st = _layer(x, xn, lw, g_next, tab, shift0[l], _pair_states(wkv0[l]),
                                       paged, l, None if cache_win is None else cache_win[l])
        b, t = nr.shape[:2]
        rows.append(nr.reshape(b, t, 4, N_KV, HEAD_B))
        wins.append(nw.reshape(b, nw.shape[1], 2, N_KV, HEAD_B))
        shifts.append(sh)
        wkvs.append(_unpair_states(st))
    return x, jnp.stack(rows), jnp.stack(wins), jnp.stack(shifts), jnp.stack(wkvs)


def kernel(x_prompt, x_sample, cache_kv, cache_win, state_shift, state_wkv, page_table, w_in, mu, w0, w_up, a0, a_up, g_up, k_k, k_a, r_k, lnx_w, lnx_b, w_oa, cmp_pe, cmp_w1, cmp_w2, w_ob, w_o, w_ff_up, w_ff_down, norm_g, rel_bias):
    depth = w_in.shape[0]
    layers = [_layer_weights(l, w_in, mu, w0, w_up, a0, a_up, g_up, k_k, k_a, r_k, lnx_w, lnx_b, w_oa,
                             cmp_pe, cmp_w1, cmp_w2, w_ob, w_o, w_ff_up, w_ff_down, norm_g) for l in range(depth)]
    tab = rel_bias[_t5_bucket_table()].T
    bp = x_prompt.shape[0]
    rw = state_shift.shape[2]
    zeros_shift = jnp.zeros((depth, bp, rw), x_prompt.dtype)
    zeros_wkv = jnp.zeros((depth, bp) + state_wkv.shape[2:], state_wkv.dtype)
    y_p, kv_p, win_p, sh_p, wkv_p = _trunk(x_prompt, zeros_shift, zeros_wkv, None, None, tab, layers)
    nl, n_pool, page = cache_kv.shape[:3]
    paged = (cache_kv.reshape(nl, n_pool, page * 4 * N_KV, HEAD_B), page_table)
    cwin = cache_win.reshape(cache_win.shape[:3] + (-1,))
    y_s, kv_s, win_s, sh_s, wkv_s = _trunk(x_sample, state_shift, state_wkv, paged, cwin, tab, layers)
    return (y_p, y_s, kv_p, kv_s, win_p, win_s, sh_p, sh_s, wkv_p, wkv_s)
```

```python
import functools
import math

import numpy as np
import jax
import jax.numpy as jnp
from jax import lax
from jax.experimental import pallas as pl
from jax.experimental.pallas import tpu as pltpu

F32 = jnp.float32
BF16 = jnp.bfloat16

HEAD_A = 64
DECAY_LORA = 64
A_LORA = 64
GATE_LORA = 160
LNX_EPS = 64e-5
N_KV = 2
HEAD_B = 128
L_CMP = 32
S_CMP = 16
L_SEL = 64
TOP_N = 16
WINDOW = 512
FORCE_BONUS = 1e4
NEG = -1e30
NUM_BUCKETS = 32
MAX_DIST = 128
EPS = 1e-6
PAGE_SIZE = 128

LANE = 128
SUBLANE = 8
VMEM_LIMIT = 56 * 1024 * 1024
RWKV_CHUNK = 64
LORA_W = 384
FAR = LANE - 1


def _cp(n_axes):
    return pltpu.CompilerParams(dimension_semantics=("arbitrary",) * n_axes,
                                vmem_limit_bytes=VMEM_LIMIT)


def _round_up(x, m):
    return -(-x // m) * m


def _tile(n, pref, mult=SUBLANE):
    if n <= pref:
        return n
    for t in range(pref - pref % mult, 0, -mult):
        if n % t == 0:
            return t
    return n


def _dot(a, b):
    return jnp.dot(a, b, preferred_element_type=F32)


def _dot_nt(a, b):
    return lax.dot_general(a, b, (((1,), (1,)), ((), ())), preferred_element_type=F32)


def _dot_tn(a, b):
    return lax.dot_general(a, b, (((0,), (0,)), ((), ())), preferred_element_type=F32)


def _split(x):
    hi = x.astype(BF16)
    lo = (x - hi.astype(F32)).astype(BF16)
    return hi, lo


def _dot_x(a, b):
    hi, lo = _split(a)
    return _dot(hi, b) + _dot(lo, b)


def _dot3(a, b):
    ah, al = _split(a)
    bh, bl = _split(b)
    return _dot(ah, bh) + _dot(ah, bl) + _dot(al, bh)


def _dot3_nt(a, b):
    ah, al = _split(a)
    bh, bl = _split(b)
    return _dot_nt(ah, bh) + _dot_nt(ah, bl) + _dot_nt(al, bh)


def _dot3_tn(a, b):
    ah, al = _split(a)
    bh, bl = _split(b)
    return _dot_tn(ah, bh) + _dot_tn(ah, bl) + _dot_tn(al, bh)


def _sigmoid(x):
    return 1.0 / (1.0 + jnp.exp(-x))


def _rms(y, g):
    return y * lax.rsqrt(jnp.mean(y * y, axis=-1, keepdims=True) + EPS) * g


def _rmsnorm_kernel(x_ref, g_ref, o_ref):
    o_ref[...] = _rms(x_ref[...], g_ref[...]).astype(o_ref.dtype)


def _rmsnorm_cast(x, g):
    m, d = x.shape
    tm = _tile(m, 256)
    return pl.pallas_call(
        _rmsnorm_kernel, grid=(m // tm,),
        in_specs=[pl.BlockSpec((tm, d), lambda i: (i, 0)), pl.BlockSpec((1, d), lambda i: (0, 0))],
        out_specs=pl.BlockSpec((tm, d), lambda i: (i, 0)),
        out_shape=jax.ShapeDtypeStruct((m, d), BF16), compiler_params=_cp(1), name="rmsnorm_cast",
    )(x, g)


def _mm_kernel(a_ref, w_ref, o_ref, *, relu2):
    y = _dot(a_ref[...], w_ref[...])
    if relu2:
        y = jnp.square(jnp.maximum(y, 0.0))
    o_ref[...] = y.astype(o_ref.dtype)


def _matmul(a, w, out_dtype, relu2=False, name="matmul"):
    m, k = a.shape
    n = w.shape[1]
    tm = _tile(m, 1024)
    tn = _tile(n, 1280, LANE)
    return pl.pallas_call(
        functools.partial(_mm_kernel, relu2=relu2), grid=(m // tm, n // tn),
        in_specs=[pl.BlockSpec((tm, k), lambda i, j: (i, 0)), pl.BlockSpec((k, tn), lambda i, j: (0, j))],
        out_specs=pl.BlockSpec((tm, tn), lambda i, j: (i, j)),
        out_shape=jax.ShapeDtypeStruct((m, n), out_dtype), compiler_params=_cp(2), name=name,
    )(a, w)


def _mm_rows_kernel(a_ref, w_ref, buf_ref, o_ref, il_ref):
    del buf_ref
    y = _dot(a_ref[...], w_ref[...])
    o_ref[...] = y
    ns = y.shape[1] // LANE
    for s in range(ns):
        il_ref[pl.ds(s, y.shape[0], stride=ns), :] = y[:, s * LANE:(s + 1) * LANE]


def _matmul_rows(a, w, buf, lidx):
    m, k = a.shape
    n = w.shape[1]
    ns = n // LANE
    tm = _tile(m, 1024)
    return pl.pallas_call(
        _mm_rows_kernel, grid=(m // tm,),
        in_specs=[pl.BlockSpec((tm, k), lambda i: (i, 0)), pl.BlockSpec((k, n), lambda i: (0, 0)),
                  pl.BlockSpec(memory_space=pl.ANY)],
        out_specs=[pl.BlockSpec((tm, n), lambda i: (i, 0)), pl.BlockSpec((None, tm * ns, LANE), lambda i: (lidx, i, 0))],
        out_shape=[jax.ShapeDtypeStruct((m, n), F32), jax.ShapeDtypeStruct(buf.shape, buf.dtype)],
        input_output_aliases={2: 1}, compiler_params=_cp(1), name="proj_rows",
    )(a, w, buf)


def _ffn_down_kernel(h_ref, w_ref, x_ref, g_ref, o_ref, n_ref, acc_ref):
    k = pl.program_id(1)

    @pl.when(k == 0)
    def _():
        acc_ref[...] = jnp.zeros_like(acc_ref)

    acc_ref[...] += _dot(h_ref[...], w_ref[...])

    @pl.when(k == pl.num_programs(1) - 1)
    def _():
        o = x_ref[...] + _rms(acc_ref[...], g_ref[0:1])
        o_ref[...] = o
        n_ref[...] = _rms(o, g_ref[1:2]).astype(n_ref.dtype)


def _ffn_down(h, w, x, g):
    m, kdim = h.shape
    n = w.shape[1]
    tm = _tile(m, 512)
    tk = _tile(kdim, 2048, LANE)
    row = pl.BlockSpec((tm, n), lambda i, k: (i, 0))
    return pl.pallas_call(
        _ffn_down_kernel, grid=(m // tm, kdim // tk),
        in_specs=[pl.BlockSpec((tm, tk), lambda i, k: (i, k)), pl.BlockSpec((tk, n), lambda i, k: (k, 0)),
                  row, pl.BlockSpec((2, n), lambda i, k: (0, 0))],
        out_specs=[row, row],
        out_shape=[jax.ShapeDtypeStruct((m, n), F32), jax.ShapeDtypeStruct((m, n), BF16)],
        scratch_shapes=[pltpu.VMEM((tm, n), F32)], compiler_params=_cp(2), name="ffn_down",
    )(h, w, x, g)


def _merge_kernel(ya_ref, yb_ref, pg_ref, x_ref, w_ref, g_ref, o_ref, n_ref):
    d = ya_ref.shape[1]
    pg = pg_ref[...].astype(F32)
    mix = _sigmoid(pg[:, :d]) * ya_ref[...].astype(F32) + _sigmoid(pg[:, d:]) * yb_ref[...].astype(F32)
    y = _dot(mix.astype(BF16), w_ref[...])
    o = x_ref[...] + _rms(y, g_ref[0:1])
    o_ref[...] = o
    n_ref[...] = _rms(o, g_ref[1:2]).astype(n_ref.dtype)


def _merge(ya, yb, pg, x, w_o, g):
    m, d = x.shape
    tm = _tile(m, 256)
    row = lambda c: pl.BlockSpec((tm, c), lambda i: (i, 0))
    return pl.pallas_call(
        _merge_kernel, grid=(m // tm,),
        in_specs=[row(d), row(d), row(2 * d), row(d),
                  pl.BlockSpec((d, d), lambda i: (0, 0)), pl.BlockSpec((2, d), lambda i: (0, 0))],
        out_specs=[row(d), row(d)],
        out_shape=[jax.ShapeDtypeStruct((m, d), F32), jax.ShapeDtypeStruct((m, d), BF16)],
        compiler_params=_cp(1), name="merge",
    )(ya, yb, pg, x, w_o, g)


def _head_indicator(d_a):
    h = np.zeros((d_a, LANE), np.float32)
    h[np.arange(d_a), np.arange(d_a) // HEAD_A] = 1.0
    return jnp.asarray(h, BF16), jnp.asarray(h.T, BF16)


def _rwkv_prep_kernel(pa_ref, prev_ref, mu_ref, wl_ref, w0_ref, a0_ref, kk_ref, ka_ref, hd_ref, hdt_ref,
                      r_ref, lw_ref, k2_ref, v_ref, kkn_ref, a_ref, g_ref, carry_ref, *, d_a, t_real):
    t = pl.program_id(1)
    tt = pa_ref.shape[1]

    @pl.when(t == 0)
    def _():
        carry_ref[...] = prev_ref[0]

    x = pa_ref[0]
    row = lax.broadcasted_iota(jnp.int32, x.shape, 0)
    xprev = jnp.where(row == 0, carry_ref[...], pltpu.roll(x, 1, 0))
    carry_ref[...] = pa_ref[0, pl.ds(tt - 1, 1), :]
    xs = x + (xprev - x) * mu_ref[...]
    r = xs[:, :d_a]
    k = xs[:, d_a:2 * d_a]
    v = xs[:, 2 * d_a:3 * d_a]
    lo = xs[:, 3 * d_a:]
    lane = lax.broadcasted_iota(jnp.int32, lo.shape, 1)
    act = jnp.where(lane < DECAY_LORA, jnp.tanh(lo),
                    jnp.where(lane < DECAY_LORA + A_LORA, lo, _sigmoid(lo)))
    lin = _dot(act.astype(BF16), wl_ref[...])
    z = -(w0_ref[...] + lin[:, :d_a])
    w = -(jnp.maximum(z, 0.0) + jnp.log(1.0 + jnp.exp(-jnp.abs(z)))) - 0.5
    logw = -jnp.exp(w)
    a = _sigmoid(a0_ref[...] + lin[:, d_a:2 * d_a])
    g = lin[:, 2 * d_a:]
    kkr = k * kk_ref[...]
    ss = _dot_x(kkr * kkr, hd_ref[...])
    inv = 1.0 / jnp.maximum(jnp.sqrt(ss), 1e-12)
    kkn = kkr * _dot_x(inv, hdt_ref[...])
    k2 = k * (1.0 + (a - 1.0) * ka_ref[...])
    if t_real is not None:
        live = (t * tt + lax.broadcasted_iota(jnp.int32, r.shape, 0)) < t_real
        zero = jnp.zeros_like(r)
        r, logw, k2, v, kkn, a = (jnp.where(live, u, zero) for u in (r, logw, k2, v, kkn, a))
    r_ref[0] = r.astype(r_ref.dtype)
    lw_ref[0] = logw
    k2_ref[0] = k2.astype(k2_ref.dtype)
    v_ref[0] = v.astype(v_ref.dtype)
    kkn_ref[0] = kkn.astype(kkn_ref.dtype)
    a_ref[0] = a.astype(a_ref.dtype)
    g_ref[0] = g.astype(g_ref.dtype)


def _rwkv_prep(pa, prev, mu, wl, w0, a0, k_k, k_a, t_real):
    b, t, p = pa.shape
    d_a = w0.shape[1]
    tt = _tile(t, 256)
    hd, hdt = _head_indicator(d_a)
    full = lambda a: pl.BlockSpec(a.shape, lambda i, j: (0,) * a.ndim)
    out = [jax.ShapeDtypeStruct((b, t, d_a), F32 if i == 1 else BF16) for i in range(7)]
    ospec = pl.BlockSpec((1, tt, d_a), lambda i, j: (i, j, 0))
    return pl.pallas_call(
        functools.partial(_rwkv_prep_kernel, d_a=d_a, t_real=t_real), grid=(b, t // tt),
        in_specs=[pl.BlockSpec((1, tt, p), lambda i, j: (i, j, 0)), pl.BlockSpec((1, 1, p), lambda i, j: (i, 0, 0)),
                  full(mu), full(wl), full(w0), full(a0), full(k_k), full(k_a), full(hd), full(hdt)],
        out_specs=[ospec] * 7, out_shape=out,
        scratch_shapes=[pltpu.VMEM((1, p), F32)], compiler_params=_cp(2), name="rwkv_prep",
    )(pa, prev, mu, wl, w0, a0, k_k, k_a, hd, hdt)


def _mmb(a, b, form="nn"):
    dot = {"nn": _dot, "nt": _dot_nt, "tn": _dot_tn}[form]
    return dot(a.astype(BF16), b.astype(BF16))


def _rwkv_scan_kernel(r_ref, lw_ref, k2_ref, v_ref, kk_ref, a_ref, s0_ref, o_ref, sT_ref, s_ref, *, c):
    @pl.when(pl.program_id(1) == 0)
    def _():
        s_ref[...] = s0_ref[0]

    hk = HEAD_A
    pairs = range(s_ref.shape[0])
    sls = [slice(p * LANE, (p + 1) * LANE) for p in pairs]
    lane = lax.broadcasted_iota(jnp.int32, (c, LANE), 1)
    m0 = lane < hk
    rowi = lax.broadcasted_iota(jnp.int32, (c, c), 0)
    coli = lax.broadcasted_iota(jnp.int32, (c, c), 1)
    ltri = (coli <= rowi).astype(BF16)
    lw = [lw_ref[0, :, sl] for sl in sls]
    cum = [_dot_x_rhs(ltri, x) for x in lw]
    e_wi = [jnp.exp(-x) for x in cum]
    kk = [kk_ref[0, :, sl].astype(F32) for sl in sls]
    at = [-k * jnp.exp(x - y) for k, x, y in zip(kk, cum, lw)]
    bt = [k * a_ref[0, :, sl].astype(F32) * e for k, sl, e in zip(kk, sls, e_wi)]
    kt = [k2_ref[0, :, sl].astype(F32) * e for sl, e in zip(sls, e_wi)]
    rt = [r_ref[0, :, sl].astype(F32) * jnp.exp(x) for sl, x in zip(sls, cum)]
    v = [v_ref[0, :, sl] for sl in sls]
    w_c = [jnp.exp(x[c - 1:c, :]) for x in cum]

    def bd(z):
        zero = jnp.zeros_like(z)
        return jnp.concatenate([jnp.where(m0, z, zero), jnp.where(m0, zero, z)], axis=0)

    prow = lax.broadcasted_iota(jnp.int32, (c, 2 * c), 0)
    pcol = lax.broadcasted_iota(jnp.int32, (c, 2 * c), 1)
    pcol = jnp.where(pcol >= c, pcol - c, pcol)
    strict = pcol < prow
    incl = pcol <= prow
    zero_cc = jnp.zeros((c, 2 * c), F32)
    x2 = [jnp.concatenate([x, y], axis=0) for x, y in zip(at, rt)]
    xb = [_mmb(x, bd(y), "nt") for x, y in zip(x2, bt)]
    xk = [_mmb(x, bd(y), "nt") for x, y in zip(x2, kt)]
    a_ab = [jnp.where(strict, x[:c], zero_cc) for x in xb]
    a_rb = [jnp.where(incl, x[c:], zero_cc) for x in xb]
    a_ak = [jnp.where(strict, x[:c], zero_cc) for x in xk]
    a_rk = [jnp.where(incl, x[c:], zero_cc) for x in xk]

    if 2 * c == LANE:
        bdc = bd
    else:
        mc = lax.broadcasted_iota(jnp.int32, (c, 2 * c), 1) < c

        def bdc(z):
            zero = jnp.zeros_like(z)
            return jnp.concatenate([jnp.where(mc, z, zero), jnp.where(mc, zero, z)], axis=0)

    eye = (pcol == prow).astype(F32)
    tmat = [eye + x for x in a_ab]
    npow = a_ab
    steps = 1
    while 2 * steps < c:
        npow = [_dot3(x, bdc(x)) for x in npow]
        tmat = [x + _dot3(x, bdc(y)) for x, y in zip(tmat, npow)]
        steps *= 2
    ta = [_mmb(x, bd(y)) for x, y in zip(tmat, at)]
    xv = [_mmb(x, bd(y)) for x, y in zip(a_ak, v)]
    tx = [_mmb(x, bd(y)) for x, y in zip(tmat, xv)]
    p_c = [x + _mmb(y, bd(z)) for x, y, z in zip(rt, a_rb, ta)]
    q_c = [_mmb(x, bd(y)) + _mmb(z, bd(u)) for x, y, z, u in zip(a_rb, tx, a_rk, v)]
    lr = lax.broadcasted_iota(jnp.int32, (LANE, LANE), 0)
    lc = lax.broadcasted_iota(jnp.int32, (LANE, LANE), 1)
    same = (lr < hk) == (lc < hk)
    eye_l = (lr == lc).astype(F32)
    zero_l = jnp.zeros((LANE, LANE), F32)
    m_c = [(eye_l + jnp.where(same, _mmb(x, y, "tn"), zero_l)) * w for x, y, w in zip(ta, bt, w_c)]
    n_c = [jnp.where(same, _mmb(x, y, "tn") + _mmb(z, u, "tn"), zero_l) * w
           for x, y, z, u, w in zip(tx, bt, v, kt, w_c)]
    s = [s_ref[p] for p in pairs]
    o = [_mmb(x, y, "nt") + z for x, y, z in zip(p_c, s, q_c)]
    s_new = [_mmb(x, y) + z for x, y, z in zip(s, m_c, n_c)]
    for p in pairs:
        o_ref[0, :, sls[p]] = o[p]
        s_ref[p] = s_new[p]
        sT_ref[0, p] = s_new[p]


def _dot_x_rhs(a, b):
    hi, lo = _split(b)
    return _dot(a, hi) + _dot(a, lo)


def _rwkv_scan(r, lw, k2, v, kk, a, s0, c):
    b, t, d_a = r.shape
    npair = d_a // LANE
    seq = pl.BlockSpec((1, c, d_a), lambda i, j: (i, j, 0))
    st = pl.BlockSpec((1, npair, LANE, LANE), lambda i, j: (i, 0, 0, 0))
    return pl.pallas_call(
        functools.partial(_rwkv_scan_kernel, c=c), grid=(b, t // c),
        in_specs=[seq] * 6 + [st], out_specs=[seq, st],
        out_shape=[jax.ShapeDtypeStruct((b, t, d_a), F32), jax.ShapeDtypeStruct((b, npair, LANE, LANE), F32)],
        scratch_shapes=[pltpu.VMEM((npair, LANE, LANE), F32)], compiler_params=_cp(2), name="rwkv_scan",
    )(r, lw, k2, v, kk, a, s0)


def _rwkv_post_kernel(o_ref, r_ref, k2_ref, v_ref, g_ref, lw_ref, lb_ref, rk_ref, hd_ref, hdt_ref, w_ref, y_ref):
    o = o_ref[...]
    hd = hd_ref[...]
    hdt = hdt_ref[...]
    inv_n = 1.0 / HEAD_A
    mean = _dot_x(_dot_x(o, hd) * inv_n, hdt)
    d = o - mean
    var = _dot_x(d * d, hd) * inv_n
    xo = d * _dot_x(lax.rsqrt(var + LNX_EPS), hdt) * lw_ref[...] + lb_ref[...]
    rk2 = r_ref[...].astype(F32) * k2_ref[...].astype(F32) * rk_ref[...]
    bonus = _dot_x(_dot_x(rk2, hd), hdt) * v_ref[...].astype(F32)
    y_ref[...] = _dot(((xo + bonus) * g_ref[...].astype(F32)).astype(BF16), w_ref[...]).astype(y_ref.dtype)


def _rwkv_post(o, r, k2, v, g, lnx_w, lnx_b, r_k, w_oa):
    m, d_a = o.shape
    n = w_oa.shape[1]
    tm = _tile(m, 256)
    hd, hdt = _head_indicator(d_a)
    row = pl.BlockSpec((tm, d_a), lambda i: (i, 0))
    full = lambda a: pl.BlockSpec(a.shape, lambda i: (0,) * a.ndim)
    return pl.pallas_call(
        _rwkv_post_kernel, grid=(m // tm,),
        in_specs=[row] * 5 + [full(lnx_w), full(lnx_b), full(r_k), full(hd), full(hdt), full(w_oa)],
        out_specs=pl.BlockSpec((tm, n), lambda i: (i, 0)), out_shape=jax.ShapeDtypeStruct((m, n), BF16),
        compiler_params=_cp(1), name="rwkv_post",
    )(o, r, k2, v, g, lnx_w, lnx_b, r_k, hd, hdt, w_oa)


def _t5_bucket_table():
    d = np.arange(LANE)
    max_exact = NUM_BUCKETS // 2
    df = np.maximum(d, 1).astype(np.float32)
    large = max_exact + (np.log(df / np.float32(max_exact)) / np.float32(math.log(MAX_DIST / max_exact))
                         * np.float32(NUM_BUCKETS - max_exact)).astype(np.int32)
    large = np.minimum(large, NUM_BUCKETS - 1)
    tab = np.where(d < max_exact, d, large)
    assert tab[FAR] == NUM_BUCKETS - 1
    return tab


PAGES_PER_STEP = 8


def _page_specs(cache, lidx, npp, half=None):
    if half is None:
        blk = (None, None) + cache.shape[2:]
        return [pl.BlockSpec(blk, lambda i, p, pt, k=k: (lidx, pt[i, p * npp + k], 0, 0)) for k in range(npp)]
    blk = (None, None, cache.shape[2], None) + cache.shape[4:]
    return [pl.BlockSpec(blk, lambda i, p, pt, k=k: (lidx, pt[i, p * npp + k], 0, half, 0, 0)) for k in range(npp)]


def _gelu_tanh(x):
    return 0.5 * x * (1.0 + jnp.tanh(math.sqrt(2.0 / math.pi) * (x + 0.044715 * (x * x * x))))


def _compress_finish(y, ex_ref, w1, w2_ref, o_ref):
    rr = y.shape[0]
    e = _dot(ex_ref[0].astype(BF16), w1)
    nxt = pltpu.roll(y[:, LANE:], rr - 1, 0)
    row = lax.broadcasted_iota(jnp.int32, nxt.shape, 0)
    nxt = jnp.where(row == rr - 1, e[0:1, LANE:], nxt)
    pre = y[:, :LANE] + nxt + (e[1:2, :LANE] + e[2:3, LANE:])
    o_ref[0] = _dot(_gelu_tanh(pre).astype(BF16), w2_ref[0])


def _compress_kernel(x_ref, ex_ref, w1_ref, w2_ref, o_ref):
    rr = x_ref.shape[0] // S_CMP
    w1 = w1_ref[0]
    y = jnp.zeros((rr, 2 * LANE), F32)
    for j in range(S_CMP):
        xj = x_ref[pl.ds(j, rr, stride=S_CMP), :].astype(BF16)
        y = y + _dot(xj, w1[j * LANE:(j + 1) * LANE, :])
    _compress_finish(y, ex_ref, w1, w2_ref, o_ref)


def _compress_finish_kernel(y_ref, ex_ref, w1_ref, w2_ref, o_ref):
    _compress_finish(y_ref[0], ex_ref, w1_ref[0], w2_ref, o_ref)


def _compress_paged_kernel(pt_ref, *refs):
    x_refs, (w1_ref, y_ref) = refs[:-2], refs[-2:]
    page, ncmp, _ = x_refs[0].shape
    cpp = page // S_CMP
    for s in range(ncmp):
        y = jnp.zeros((len(x_refs) * cpp, 2 * LANE), F32)
        for j in range(S_CMP):
            xj = jnp.concatenate([x[pl.ds(j, cpp, stride=S_CMP), s, :] for x in x_refs], axis=0)
            y = y + _dot(xj.astype(BF16), w1_ref[s // N_KV, j * LANE:(j + 1) * LANE, :])
        y_ref[0, s] = y


def _compress(x, x_spec, n_pos, ex, w1cat, w2):
    nb, _, cw = ex.shape
    rr = n_pos // S_CMP
    c_of = lambda i: (i // N_KV) % 2
    return pl.pallas_call(
        _compress_kernel, grid=(nb,),
        in_specs=[x_spec, pl.BlockSpec((1, SUBLANE, cw), lambda i: (i, 0, 0)),
                  pl.BlockSpec((1, cw, 2 * LANE), lambda i: (c_of(i), 0, 0)),
                  pl.BlockSpec((1, LANE, LANE), lambda i: (c_of(i), 0, 0))],
        out_specs=pl.BlockSpec((1, rr, LANE), lambda i: (i, 0, 0)),
        out_shape=jax.ShapeDtypeStruct((nb, rr, LANE), F32), compiler_params=_cp(1), name="nsa_compress",
    )(x, ex, w1cat, w2)


def _compress_paged(cache, lidx, page_table, ex, w1cat, w2):
    b, n_pages = page_table.shape
    ncmp = cache.shape[4]
    cw = ex.shape[2]
    page = cache.shape[2]
    npp = next(k for k in (PAGES_PER_STEP, 4, 2, 1) if n_pages % k == 0)
    rr = n_pages * page // S_CMP
    rt = npp * page // S_CMP
    grid_spec = pltpu.PrefetchScalarGridSpec(
        num_scalar_prefetch=1, grid=(b, n_pages // npp),
        in_specs=_page_specs(cache, lidx, npp, 0) + [pl.BlockSpec(w1cat.shape, lambda i, p, pt: (0, 0, 0))],
        out_specs=pl.BlockSpec((1, ncmp, rt, 2 * LANE), lambda i, p, pt: (i, 0, p, 0)))
    y = pl.pallas_call(
        _compress_paged_kernel, grid_spec=grid_spec,
        out_shape=jax.ShapeDtypeStruct((b, ncmp, rr, 2 * LANE), F32),
        compiler_params=_cp(2), name="nsa_compress_paged",
    )(page_table, *([cache] * npp), w1cat)
    c_of = lambda i: (i // N_KV) % 2
    return pl.pallas_call(
        _compress_finish_kernel, grid=(b * ncmp,),
        in_specs=[pl.BlockSpec((1, rr, 2 * LANE), lambda i: (i, 0, 0)), pl.BlockSpec((1, SUBLANE, cw), lambda i: (i, 0, 0)),
                  pl.BlockSpec((1, cw, 2 * LANE), lambda i: (c_of(i), 0, 0)),
                  pl.BlockSpec((1, LANE, LANE), lambda i: (c_of(i), 0, 0))],
        out_specs=pl.BlockSpec((1, rr, LANE), lambda i: (i, 0, 0)),
        out_shape=jax.ShapeDtypeStruct((b * ncmp, rr, LANE), F32), compiler_params=_cp(1), name="nsa_compress_finish",
    )(y.reshape(b * ncmp, rr, 2 * LANE), ex, w1cat, w2)


def _bias_gather(tab_row, dist):
    idx = jnp.clip(dist, 0, FAR)
    return jnp.take_along_axis(jnp.broadcast_to(tab_row, idx.shape), idx, axis=1)


def _stack_heads(q_ref, col0=0, hpg=None):
    hpg = q_ref.shape[1] // HEAD_B if hpg is None else hpg
    return jnp.concatenate([q_ref[:, col0 + h * HEAD_B:col0 + (h + 1) * HEAD_B] for h in range(hpg)], axis=0)


def _rel_bias(dist, g, hpg, tab_ref):
    rel = []
    for h in range(hpg):
        tab_row = tab_ref[pl.ds(g * hpg + h, 1), :]
        far = tab_row[:, FAR:FAR + 1]
        rel.append(jnp.concatenate([_bias_gather(tab_row, dist[:, c * LANE:(c + 1) * LANE]) - far
                                    for c in range(dist.shape[1] // LANE)], axis=1))
    return jnp.stack(rel)


def _online_softmax_update(qs, tiles):
    hpg, tq, _ = tiles[0][4].shape
    scores = [_dot_nt(qs, kb) for kb, *_ in tiles]
    probs, alphas = [], []
    for s, (kb, vb, mask, rel, m_ref, l_ref, acc_ref) in zip(scores, tiles):
        width = kb.shape[0]
        s = s.reshape(hpg, tq, width) * (HEAD_B ** -0.5)
        if rel is None:
            s = s + jnp.where(mask, 0.0, 2 * NEG)[None]
        else:
            s = s + jnp.where(mask[None], rel, 2 * NEG)
        m_prev = m_ref[...][:, :, :1]
        m_new = jnp.maximum(m_prev, jnp.max(s, axis=2, keepdims=True))
        alpha = jnp.exp(m_prev - m_new)
        p = jnp.exp(s - m_new)
        m_ref[...] = jnp.broadcast_to(m_new, m_ref.shape)
        probs.append(p.reshape(hpg * tq, width).astype(BF16))
        alphas.append(alpha)
    pvs = [_dot(p, jnp.concatenate([t[1], jnp.ones_like(t[1])], axis=1)) for p, t in zip(probs, tiles)]
    for pv, alpha, t in zip(pvs, alphas, tiles):
        pv = pv.reshape(hpg, tq, 2 * HEAD_B)
        t[5][...] = alpha * t[5][...] + pv[:, :, HEAD_B:]
        t[6][...] = alpha * t[6][...] + pv[:, :, :HEAD_B]


def _cmp_attn_kernel(q_ref, kc_ref, vc_ref, ovt_ref, tab_ref, o_ref, imp_ref, *, tq, q0, lanes):
    g = pl.program_id(1)
    qt = pl.program_id(2)
    hpg = q_ref.shape[2] // HEAD_B
    rr = kc_ref.shape[1]
    kc = kc_ref[0].astype(BF16)
    vc = vc_ref[0].astype(BF16)
    qbase = q0 + qt * tq
    qrow = qbase + lax.broadcasted_iota(jnp.int32, (tq, LANE), 0)
    dists = []
    for cix in range(rr // LANE):
        c_end = S_CMP * (cix * LANE + lax.broadcasted_iota(jnp.int32, (tq, LANE), 1)) + (L_CMP - 1)
        dists.append(qrow - c_end)
    mask = (jnp.concatenate(dists, axis=1) >= 0)[None]
    bias = jnp.stack([jnp.concatenate([_bias_gather(tab_ref[pl.ds(g * hpg + h, 1), :], d) for d in dists], axis=1)
                      for h in range(hpg)])
    s = _dot_nt(_stack_heads(q_ref.at[0]), kc).reshape(hpg, tq, rr) * (HEAD_B ** -0.5) + bias
    s = jnp.where(mask, s, NEG)
    e = jnp.where(mask, jnp.exp(s - jnp.max(s, axis=2, keepdims=True)), 0.0)
    p = e / jnp.maximum(jnp.sum(e, axis=2, keepdims=True), 1e-30)
    o = _dot(p.reshape(hpg * tq, rr).astype(BF16), vc).reshape(hpg, tq, HEAD_B)
    o_ref[0] = jnp.concatenate([o[h] for h in range(hpg)], axis=1)
    psum = jnp.sum(p, axis=0)
    if tq < lanes:
        psum = jnp.concatenate([psum, jnp.zeros((lanes - tq, rr), F32)], axis=0)
    ph, plo = _split(psum)
    ovt = ovt_ref[...]
    imp_ref[0, 0] = _dot_nt(ovt, ph) + _dot_nt(ovt, plo)


def _select_kernel(imp_ref, qpos_ref, selt_ref, *, n_top):
    nsr, lanes = imp_ref.shape[2:]
    j = lax.broadcasted_iota(jnp.int32, (nsr, lanes), 0)
    cur = qpos_ref[...] // L_SEL
    valid = j <= cur
    forced = (j == 0) | (j == cur) | (j == cur - 1)
    score = jnp.where(valid, imp_ref[0, 0] + jnp.where(forced, FORCE_BONUS, 0.0), NEG)

    def body(_, carry):
        score, sel = carry
        top = jnp.max(score, axis=0, keepdims=True)
        first = jnp.min(jnp.where(score == top, j, nsr), axis=0, keepdims=True)
        pick = j == first
        sel = jnp.where(pick & (top > NEG / 2), 1.0, sel)
        return jnp.where(pick, 2 * NEG, score), sel

    _, sel = lax.fori_loop(0, n_top, body, (score, jnp.zeros((nsr, lanes), F32)))
    selt_ref[0, 0] = sel


def _select(imp, qpos, ns):
    x, gg, nsr, l = imp.shape
    lt = next(c for c in (256, 128) if l % c == 0)
    blk = pl.BlockSpec((1, 1, nsr, lt), lambda i, g, t: (i, g, 0, t))
    return pl.pallas_call(
        functools.partial(_select_kernel, n_top=min(TOP_N, ns)), grid=(x, gg, l // lt),
        in_specs=[blk, pl.BlockSpec((1, lt), lambda i, g, t: (0, t))], out_specs=blk,
        out_shape=jax.ShapeDtypeStruct(imp.shape, F32), compiler_params=_cp(3), name="nsa_select",
    )(imp, qpos)


def _cmp_attn(q, kcv, tab, *, t_keys, q0, tq):
    b, tqa, d_b = q.shape
    rr = kcv.shape[1]
    gw = d_b // N_KV
    ns = -(-t_keys // L_SEL)
    nsr = _round_up(ns, SUBLANE)
    lanes = max(tq, LANE)
    nqt = tqa // tq
    ci = np.arange(rr)
    sj = np.arange(nsr)
    ov = ((S_CMP * ci) // L_SEL)[None, :] == sj[:, None]
    ov |= ((S_CMP * ci + L_CMP - 1) // L_SEL)[None, :] == sj[:, None]
    ovt = jnp.asarray(ov.astype(np.float32), BF16)
    kern = functools.partial(_cmp_attn_kernel, tq=tq, q0=q0, lanes=lanes)
    return pl.pallas_call(
        kern, grid=(b, N_KV, nqt),
        in_specs=[pl.BlockSpec((1, tq, gw), lambda i, g, t: (i, t, g)),
                  pl.BlockSpec((1, rr, LANE), lambda i, g, t: (i * 4 + g, 0, 0)),
                  pl.BlockSpec((1, rr, LANE), lambda i, g, t: (i * 4 + N_KV + g, 0, 0)),
                  pl.BlockSpec(ovt.shape, lambda i, g, t: (0, 0)),
                  pl.BlockSpec(tab.shape, lambda i, g, t: (0, 0))],
        out_specs=[pl.BlockSpec((1, tq, gw), lambda i, g, t: (i, t, g)),
                   pl.BlockSpec((1, 1, nsr, lanes), lambda i, g, t: (i, g, 0, t))],
        out_shape=[jax.ShapeDtypeStruct((b, tqa, d_b), F32),
                   jax.ShapeDtypeStruct((b, N_KV, nsr, nqt * lanes), F32)],
        compiler_params=_cp(3), name="nsa_cmp_attn",
    )(q, kcv, kcv, ovt, tab)


def _flash_kernel(*refs, mode, band, tq, tk, nj, q0, kbase, n_tail, tail_pos0):
    refs = list(refs)
    q_ref, k_ref, v_ref = refs[:3]
    pos = 3
    if n_tail:
        kt_ref, vt_ref = refs[pos:pos + 2]
        pos += 2
    if mode == "sel":
        sel_ref = refs[pos]
        pos += 1
    tab_ref, o_ref, m_ref, l_ref, acc_ref = refs[pos:pos + 5]
    g = pl.program_id(1)
    qt = pl.program_id(2)
    j = pl.program_id(3)
    hpg = q_ref.shape[1] // HEAD_B
    qbase = q0 + qt * tq
    qs = _stack_heads(q_ref)

    @pl.when(j == 0)
    def _():
        m_ref[...] = jnp.full_like(m_ref, NEG)
        l_ref[...] = jnp.zeros_like(l_ref)
        acc_ref[...] = jnp.zeros_like(acc_ref)

    def update(k_r, v_r, kpos0, width, n_valid, near):
        qrow = qbase + lax.broadcasted_iota(jnp.int32, (tq, width), 0)
        kcol = kpos0 + lax.broadcasted_iota(jnp.int32, (tq, width), 1)
        dist = qrow - kcol
        mask = dist >= 0
        if n_valid is not None:
            mask &= lax.broadcasted_iota(jnp.int32, (tq, width), 1) < n_valid
        if mode == "win":
            mask &= dist < WINDOW
        else:
            nsp = sel_ref.shape[1]
            blk = (kpos0 + lax.broadcasted_iota(jnp.int32, (nsp, width), 1)) // L_SEL
            expand = (blk == lax.broadcasted_iota(jnp.int32, (nsp, width), 0)).astype(BF16)
            mask &= _dot(sel_ref[...].astype(BF16), expand) > 0.5
        kb = k_r[...].astype(BF16)
        vb = v_r[...].astype(BF16)
        rel = _rel_bias(dist, g, hpg, tab_ref) if near else None
        _online_softmax_update(qs, [(kb, vb, mask, rel, m_ref, l_ref, acc_ref)])

    if band:
        kt_abs = qt - (nj - 1) + j
        active = kt_abs >= 0
    else:
        kt_abs = j
        active = kbase + j * tk <= qbase + tq - 1
    kpos0 = kbase + kt_abs * tk
    is_far = qbase - (kpos0 + tk - 1) >= FAR

    @pl.when(active & is_far)
    def _():
        update(k_ref, v_ref, kpos0, tk, None, False)

    @pl.when(active & jnp.logical_not(is_far))
    def _():
        update(k_ref, v_ref, kpos0, tk, None, True)

    @pl.when(j == nj - 1)
    def _():
        if n_tail:
            update(kt_ref, vt_ref, tail_pos0, kt_ref.shape[0], n_tail, True)
        outs = [acc_ref[h] / jnp.maximum(l_ref[h][:, :1], 1e-30) for h in range(hpg)]
        o_ref[...] = jnp.concatenate(outs, axis=1)


def _sel_paged_kernel(pt_ref, *refs, ns, n_tail, p_len):
    npp = len(refs) - 8
    q_ref, x_refs = refs[0], refs[1:1 + npp]
    tail_ref, sel_ref, tab_ref, o_ref, m_ref, l_ref, acc_ref = refs[1 + npp:]
    j = pl.program_id(1)
    nj = pl.num_programs(1)
    tq = q_ref.shape[0]
    gw = q_ref.shape[1] // N_KV
    hpg = gw // HEAD_B
    page = x_refs[0].shape[0] // ns
    width = npp * page

    @pl.when(j == 0)
    def _():
        m_ref[...] = jnp.full_like(m_ref, NEG)
        l_ref[...] = jnp.zeros_like(l_ref)
        acc_ref[...] = jnp.zeros_like(acc_ref)

    def fold(g, kb, vb, kpos0, n_valid, near):
        w = kb.shape[0]
        col = lax.broadcasted_iota(jnp.int32, (tq, w), 1)
        dist = p_len + lax.broadcasted_iota(jnp.int32, (tq, w), 0) - (kpos0 + col)
        mask = dist >= 0
        if n_valid is not None:
            mask &= col < n_valid
        nsp = sel_ref.shape[2]
        blk = (kpos0 + lax.broadcasted_iota(jnp.int32, (nsp, w), 1)) // L_SEL
        expand = (blk == lax.broadcasted_iota(jnp.int32, (nsp, w), 0)).astype(BF16)
        mask &= _dot(sel_ref[g].astype(BF16), expand) > 0.5
        rel = _rel_bias(dist, g, hpg, tab_ref) if near else None
        _online_softmax_update(_stack_heads(q_ref, g * gw, hpg),
                               [(kb, vb, mask, rel, m_ref.at[g], l_ref.at[g], acc_ref.at[g])])

    def pages(stream):
        return jnp.concatenate([x[pl.ds(stream, page, stride=ns), :] for x in x_refs], axis=0).astype(BF16)

    def main(near):
        for g in range(N_KV):
            fold(g, pages(2 * N_KV + g), pages(3 * N_KV + g), j * width, None, near)

    @pl.when(j < nj - 1)
    def _():
        main(False)

    @pl.when(j == nj - 1)
    def _():
        main(True)
        for g in range(N_KV):
            kt = tail_ref[:, (2 * N_KV + g) * LANE:(2 * N_KV + g + 1) * LANE].astype(BF16)
            vt = tail_ref[:, (3 * N_KV + g) * LANE:(3 * N_KV + g + 1) * LANE].astype(BF16)
            fold(g, kt, vt, p_len, n_tail, True)
        out = acc_ref[...] / jnp.maximum(l_ref[...][:, :, :, :1], 1e-30)
        o_ref[...] = jnp.concatenate([out[g, h] for g in range(N_KV) for h in range(hpg)], axis=1)


def _sel_paged(q, cache, lidx, page_table, rows_t, n_tail, sel, tab):
    b, tq, d_b = q.shape
    n_pages = page_table.shape[1]
    ns = 4 * N_KV
    page = cache.shape[2] // ns
    npp = next(k for k in (PAGES_PER_STEP, 4, 2, 1) if n_pages % k == 0)
    assert npp * page > FAR
    hpg = d_b // N_KV // HEAD_B
    whole = lambda a: pl.BlockSpec((None,) + a.shape[1:], lambda i, p, pt: (i,) + (0,) * (a.ndim - 1))
    grid_spec = pltpu.PrefetchScalarGridSpec(
        num_scalar_prefetch=1, grid=(b, n_pages // npp),
        in_specs=[whole(q)] + _page_specs(cache, lidx, npp) + [whole(rows_t), whole(sel),
                  pl.BlockSpec(tab.shape, lambda i, p, pt: (0, 0))],
        out_specs=pl.BlockSpec((None, tq, d_b), lambda i, p, pt: (i, 0, 0)),
        scratch_shapes=[pltpu.VMEM((N_KV, hpg, tq, LANE), F32), pltpu.VMEM((N_KV, hpg, tq, LANE), F32),
                        pltpu.VMEM((N_KV, hpg, tq, HEAD_B), F32)])
    return pl.pallas_call(
        functools.partial(_sel_paged_kernel, ns=ns, n_tail=n_tail, p_len=n_pages * page), grid_spec=grid_spec,
        out_shape=jax.ShapeDtypeStruct((b, tq, d_b), F32), compiler_params=_cp(2), name="nsa_sel_attn_paged",
    )(page_table, q, *([cache] * npp), rows_t, sel, tab)


def _flash(q, k_arr, k_spec, v_arr, v_spec, tab, *, mode, band, tq, tk, nj, q0, kbase,
           tail=None, sel=None, name):
    b, tqa, d_b = q.shape
    gw = d_b // N_KV
    hpg = gw // HEAD_B
    nqt = tqa // tq
    args = [q, k_arr, v_arr]
    in_specs = [pl.BlockSpec((None, tq, gw), lambda i, g, t, j: (i, t, g)), k_spec, v_spec]
    n_tail, tail_pos0 = 0, 0
    if tail is not None:
        tk_arr, tk_spec, tv_arr, tv_spec, n_tail, tail_pos0 = tail
        args += [tk_arr, tv_arr]
        in_specs += [tk_spec, tv_spec]
    if mode == "sel":
        args.append(sel)
        in_specs.append(pl.BlockSpec((None, None, tq, sel.shape[3]), lambda i, g, t, j: (i, g, t, 0)))
    args.append(tab)
    in_specs.append(pl.BlockSpec(tab.shape, lambda i, g, t, j: (0, 0)))
    kern = functools.partial(_flash_kernel, mode=mode, band=band, tq=tq, tk=tk, nj=nj, q0=q0, kbase=kbase,
                             n_tail=n_tail, tail_pos0=tail_pos0)
    return pl.pallas_call(
        kern, grid=(b, N_KV, nqt, nj), in_specs=in_specs,
        out_specs=pl.BlockSpec((None, tq, gw), lambda i, g, t, j: (i, t, g)),
        out_shape=jax.ShapeDtypeStruct((b, tqa, d_b), F32),
        scratch_shapes=[pltpu.VMEM((hpg, tq, LANE), F32), pltpu.VMEM((hpg, tq, LANE), F32),
                        pltpu.VMEM((hpg, tq, HEAD_B), F32)],
        compiler_params=_cp(4), name=name,
    )(*args)


def _attn_res_kernel(*refs, mode, tq, ch, far_w):
    if mode == "sel":
        q_ref, k_ref, v_ref, sel_ref, tab_ref, o_ref, m_ref, l_ref, acc_ref = refs
    else:
        q_ref, k_ref, v_ref, tab_ref, o_ref, m_ref, l_ref, acc_ref = refs
    g = pl.program_id(1)
    qt = pl.program_id(2)
    hpg = q_ref.shape[1] // HEAD_B
    qbase = qt * tq
    near0 = qbase - LANE
    qs = _stack_heads(q_ref)
    m_ref[...] = jnp.full_like(m_ref, NEG)
    l_ref[...] = jnp.zeros_like(l_ref)
    acc_ref[...] = jnp.zeros_like(acc_ref)

    def tile(stream, kpos0, width, near):
        qrow = qbase + lax.broadcasted_iota(jnp.int32, (tq, width), 0)
        kcol = kpos0 + lax.broadcasted_iota(jnp.int32, (tq, width), 1)
        dist = qrow - kcol
        mask = (dist >= 0) if near else (kcol < near0)
        if mode == "win":
            mask &= dist < WINDOW
        else:
            nsp = sel_ref.shape[1]
            blk = (kpos0 + lax.broadcasted_iota(jnp.int32, (nsp, width), 1)) // L_SEL
            expand = (blk == lax.broadcasted_iota(jnp.int32, (nsp, width), 0)).astype(BF16)
            mask &= _dot(sel_ref[...].astype(BF16), expand) > 0.5
        kb = k_ref[pl.ds(kpos0, width), :].astype(BF16)
        vb = v_ref[pl.ds(kpos0, width), :].astype(BF16)
        rel = _rel_bias(dist, g, hpg, tab_ref) if near else None
        return (kb, vb, mask, rel, m_ref.at[stream], l_ref.at[stream], acc_ref.at[stream])

    near = lambda: tile(1, pl.multiple_of(jnp.maximum(near0, 0), LANE), 2 * LANE, True)
    if mode == "sel":
        def body(c, carry):
            _online_softmax_update(qs, [tile(0, pl.multiple_of(c * ch, ch), ch, False)])
            return carry

        lax.fori_loop(0, (jnp.maximum(near0, 0) + ch - 1) // ch, body, 0)
        _online_softmax_update(qs, [near()])
    elif far_w:
        far = tile(0, pl.multiple_of(jnp.maximum(qbase - WINDOW, 0), LANE), far_w, False)
        _online_softmax_update(qs, [far, near()])
    else:
        _online_softmax_update(qs, [near()])
    m0, m1 = m_ref[0][:, :, :1], m_ref[1][:, :, :1]
    m = jnp.maximum(m0, m1)
    w0, w1 = jnp.exp(m0 - m), jnp.exp(m1 - m)
    l = w0 * l_ref[0][:, :, :1] + w1 * l_ref[1][:, :, :1]
    out = (w0 * acc_ref[0] + w1 * acc_ref[1]) / jnp.maximum(l, 1e-30)
    o_ref[...] = jnp.concatenate([out[h] for h in range(hpg)], axis=1)


def _attn_res(q, kv, kcol, vcol, tab, *, mode, sel=None, name):
    b, t, d_b = q.shape
    gw = d_b // N_KV
    hpg = gw // HEAD_B
    tq = LANE
    assert t % tq == 0 and t >= 2 * LANE
    ch = next(c for c in (1024, 512, 256, 128) if t % c == 0)
    far_w = min(WINDOW - LANE, t - 2 * LANE)
    args = [q, kv, kv]
    in_specs = [pl.BlockSpec((None, tq, gw), lambda i, g, qt: (i, qt, g)),
                pl.BlockSpec((None, t, LANE), lambda i, g, qt: (i, 0, kcol + g)),
                pl.BlockSpec((None, t, LANE), lambda i, g, qt: (i, 0, vcol + g))]
    if mode == "sel":
        args.append(sel)
        in_specs.append(pl.BlockSpec((None, None, tq, sel.shape[3]), lambda i, g, qt: (i, g, qt, 0)))
    args.append(tab)
    in_specs.append(pl.BlockSpec(tab.shape, lambda i, g, qt: (0, 0)))
    return pl.pallas_call(
        functools.partial(_attn_res_kernel, mode=mode, tq=tq, ch=ch, far_w=far_w),
        grid=(b, N_KV, t // tq), in_specs=in_specs,
        out_specs=pl.BlockSpec((None, tq, gw), lambda i, g, qt: (i, qt, g)),
        out_shape=jax.ShapeDtypeStruct((b, t, d_b), F32),
        scratch_shapes=[pltpu.VMEM((2, hpg, tq, LANE), F32), pltpu.VMEM((2, hpg, tq, LANE), F32),
                        pltpu.VMEM((2, hpg, tq, HEAD_B), F32)],
        compiler_params=_cp(3), name=name,
    )(*args)


def _nsa_out_kernel(oc_ref, os_ref, ow_ref, gl_ref, e_ref, w_ref, y_ref):
    gate = _sigmoid(gl_ref[...])
    o = (_dot_x(gate, e_ref[0]) * oc_ref[...] + _dot_x(gate, e_ref[1]) * os_ref[...]
         + _dot_x(gate, e_ref[2]) * ow_ref[...])
    y_ref[...] = _dot(o.astype(BF16), w_ref[...]).astype(y_ref.dtype)


def _nsa_out(o_c, o_s, o_w, gl, w_ob):
    m, d_b = o_c.shape
    n = w_ob.shape[1]
    h_b = d_b // HEAD_B
    tm = _tile(m, 256)
    e = np.zeros((3, LANE, d_b), np.float32)
    for br in range(3):
        for hh in range(h_b):
            e[br, br * h_b + hh, hh * HEAD_B:(hh + 1) * HEAD_B] = 1.0
    e = jnp.asarray(e, BF16)
    row = pl.BlockSpec((tm, d_b), lambda i: (i, 0))
    return pl.pallas_call(
        _nsa_out_kernel, grid=(m // tm,),
        in_specs=[row, row, row, pl.BlockSpec((tm, LANE), lambda i: (i, 0)),
                  pl.BlockSpec(e.shape, lambda i: (0, 0, 0)), pl.BlockSpec(w_ob.shape, lambda i: (0, 0))],
        out_specs=pl.BlockSpec((tm, n), lambda i: (i, 0)), out_shape=jax.ShapeDtypeStruct((m, n), BF16),
        compiler_params=_cp(1), name="nsa_out",
    )(o_c, o_s, o_w, gl, e, w_ob)


def _nsa(q, rows, win, gl, tab, lw, *, t_real, paged, lidx, cwin):
    b, tqa, d_b = q.shape
    t = t_real
    ncmp = 2 * N_KV
    cw = S_CMP * LANE
    p_len = 0 if paged is None else paged[1].shape[1] * paged[0].shape[2]
    l_tot = p_len + t
    assert p_len % (S_CMP * SUBLANE) == 0 and (p_len == 0 or t <= S_CMP) and (p_len > 0 or t % LANE == 0)
    pe = jnp.repeat(lw["cmp_pe"].reshape(2, 1, 2, cw), N_KV, axis=1)
    pe = jnp.broadcast_to(pe[None], (b, 2, N_KV, 2, cw)).reshape(b * ncmp, 2, cw)
    pad_rows = jnp.zeros((b * ncmp, SUBLANE - 3, cw), F32)
    if paged is None:
        ex = jnp.concatenate([jnp.zeros((b * ncmp, 1, cw), F32), pe, pad_rows], axis=1)
        x_spec = pl.BlockSpec((None, t, LANE), lambda i: (i // ncmp, 0, i % ncmp))
        kcv = _compress(rows, x_spec, t, ex, lw["w1cat"], lw["w2"])
    else:
        new_cmp = rows[:, :, :ncmp * LANE].reshape(b, t, ncmp, LANE).transpose(0, 2, 1, 3)
        ex0 = jnp.pad(new_cmp.reshape(b * ncmp, 1, t * LANE), ((0, 0), (0, 0), (0, cw - t * LANE)))
        ex = jnp.concatenate([ex0, pe, pad_rows], axis=1)
        kcv = _compress_paged(paged[0], lidx, paged[1], ex, lw["w1cat"], lw["w2"])
    if kcv.shape[1] % LANE:
        kcv = jnp.pad(kcv, ((0, 0), (0, _round_up(kcv.shape[1], LANE) - kcv.shape[1]), (0, 0)))
    tq = tqa if p_len else LANE
    o_c, imp = _cmp_attn(q, kcv, tab, t_keys=l_tot, q0=p_len, tq=tq)
    ns = -(-l_tot // L_SEL)
    nsp = _round_up(ns, LANE)
    if p_len == 0:
        sel = jnp.swapaxes(_select(imp, jnp.arange(tqa, dtype=jnp.int32)[None], ns), 2, 3)
    else:
        nq = b * tqa
        nql = _round_up(nq, LANE)
        nsr = imp.shape[2]
        impl = jnp.pad(imp[..., :tqa].transpose(1, 2, 0, 3).reshape(1, N_KV, nsr, nq),
                       ((0, 0), (0, 0), (0, 0), (0, nql - nq)))
        qpos = (p_len + jnp.arange(nql, dtype=jnp.int32) % tqa)[None]
        sel = _select(impl, qpos, ns)[0, :, :, :nq].reshape(N_KV, nsr, b, tqa).transpose(2, 0, 3, 1)
    sel = jnp.pad(sel, ((0, 0), (0, 0), (0, 0), (0, nsp - sel.shape[3])))
    if paged is None:
        o_s = _attn_res(q, rows, 2 * N_KV, 3 * N_KV, tab, mode="sel", sel=sel, name="nsa_sel_attn")
        o_w = _attn_res(q, win, 0, N_KV, tab, mode="win", name="nsa_win_attn")
    else:
        rows_t = jnp.pad(rows, ((0, 0), (0, LANE - t), (0, 0)))
        win_t = jnp.pad(win, ((0, 0), (0, LANE - t), (0, 0)))
        tails = lambda col: pl.BlockSpec((None, LANE, LANE), lambda i, g, qt, j, col=col: (i, 0, col + g))
        o_s = _sel_paged(q, paged[2], lidx, paged[1], rows_t, t, sel, tab)
        wb = cwin.shape[1]
        kw = lambda col: pl.BlockSpec((None, wb, LANE), lambda i, g, qt, j, col=col: (i, 0, col + g))
        o_w = _flash(q, cwin, kw(0), cwin, kw(N_KV), tab, mode="win", band=False, tq=tq, tk=wb,
                     nj=1, q0=p_len, kbase=p_len - wb,
                     tail=(win_t, tails(0), win_t, tails(N_KV), t, p_len), name="nsa_win_attn_cached")
    m = b * tqa
    return _nsa_out(o_c.reshape(m, d_b), o_s.reshape(m, d_b), o_w.reshape(m, d_b), gl, lw["w_ob"])


def _layer_weights(l, w_in, mu, w0, w_up, a0, a_up, g_up, k_k, k_a, r_k, lnx_w, lnx_b, w_oa,
                   cmp_pe, cmp_w1, cmp_w2, w_ob, w_o, w_ff_up, w_ff_down, norm_g):
    d_a = w_oa.shape[1]
    d_b = w_ob.shape[1]
    d = w_o.shape[1]
    rw = 3 * d_a + DECAY_LORA + A_LORA + GATE_LORA
    pw = 3 * d_a + LORA_W
    nrow = 4 * N_KV * HEAD_B
    nwin = 2 * N_KV * HEAD_B
    ngl = 3 * (d_b // HEAD_B)
    wi = w_in[l]
    o_q = rw
    o_rows = o_q + d_b
    o_win = o_rows + nrow
    o_gl = o_win + nwin
    o_pg = o_gl + ngl
    wl = jnp.zeros((LORA_W, 3 * d_a), F32)
    wl = wl.at[:DECAY_LORA, :d_a].set(w_up[l])
    wl = wl.at[DECAY_LORA:DECAY_LORA + A_LORA, d_a:2 * d_a].set(a_up[l])
    wl = wl.at[DECAY_LORA + A_LORA:DECAY_LORA + A_LORA + GATE_LORA, 2 * d_a:].set(g_up[l])
    half = S_CMP * HEAD_B
    return dict(
        w_pa=jnp.pad(wi[:, :rw], ((0, 0), (0, pw - rw))).astype(BF16),
        w_q=wi[:, o_q:o_rows].astype(BF16),
        w_rows=wi[:, o_rows:o_win].astype(BF16),
        w_win=wi[:, o_win:o_gl].astype(BF16),
        w_gl=jnp.pad(wi[:, o_gl:o_pg], ((0, 0), (0, LANE - ngl))).astype(BF16),
        w_pg=wi[:, o_pg:o_pg + 2 * d].astype(BF16),
        mu=jnp.pad(mu[l], (0, pw - rw))[None], wl=wl.astype(BF16),
        w0=w0[l][None], a0=a0[l][None], k_k=k_k[l][None], k_a=k_a[l][None],
        r_k=r_k[l].reshape(1, d_a), lnx_w=lnx_w[l][None], lnx_b=lnx_b[l][None],
        w_oa=w_oa[l].astype(BF16), w_ob=w_ob[l].astype(BF16), w_o=w_o[l].astype(BF16),
        w_up=w_ff_up[l].astype(BF16), w_down=w_ff_down[l].astype(BF16),
        cmp_pe=cmp_pe[l],
        w1cat=jnp.concatenate([cmp_w1[l][:, :half], cmp_w1[l][:, half:]], axis=2).astype(BF16),
        w2=cmp_w2[l].astype(BF16), g=norm_g[l], rw=rw,
    )


def _layer(x, xn, lw, g_next, tab, shift0, s0, paged, lidx, cwin, rows_buf):
    b, t, d = x.shape
    m = b * t
    d_a = lw["w_oa"].shape[0]
    x2 = x.reshape(m, d)
    g = lw["g"]
    pa = _matmul(xn, lw["w_pa"], F32, name="proj_rwkv")
    q = _matmul(xn, lw["w_q"], BF16, name="proj_q")
    if rows_buf is None:
        rows = _matmul(xn, lw["w_rows"], F32, name="proj_rows")
    else:
        rows, rows_buf = _matmul_rows(xn, lw["w_rows"], rows_buf, lidx)
    win = _matmul(xn, lw["w_win"], F32, name="proj_win")
    gl = _matmul(xn, lw["w_gl"], F32, name="proj_gl")
    pg = _matmul(xn, lw["w_pg"], BF16, name="proj_pg")
    pw = pa.shape[1]
    d_b = q.shape[1]
    tp = t if t % RWKV_CHUNK == 0 else _round_up(t, SUBLANE)
    c = RWKV_CHUNK if t % RWKV_CHUNK == 0 else tp
    pa3 = pa.reshape(b, t, pw)
    pa_p = pa3 if tp == t else jnp.pad(pa3, ((0, 0), (0, tp - t), (0, 0)))
    prev = jnp.pad(shift0, ((0, 0), (0, pw - shift0.shape[1])))[:, None]
    r, lgw, k2, v, kk, a, gg = _rwkv_prep(pa_p, prev, lw["mu"], lw["wl"], lw["w0"], lw["a0"], lw["k_k"], lw["k_a"],
                                          None if tp == t else t)
    o, s_fin = _rwkv_scan(r, lgw, k2, v, kk, a, s0, c)
    flat = lambda u: u.reshape(b * tp, d_a)
    ya = _rwkv_post(flat(o), flat(r), flat(k2), flat(v), flat(gg), lw["lnx_w"], lw["lnx_b"], lw["r_k"], lw["w_oa"])
    if tp != t:
        ya = ya.reshape(b, tp, d)[:, :t].reshape(m, d)
    sh = pa3[:, t - 1, :lw["rw"]]
    tqa = t if t % LANE == 0 else _round_up(t, SUBLANE)
    q3 = q.reshape(b, t, d_b)
    gl_p = gl
    if tqa != t:
        q3 = jnp.pad(q3, ((0, 0), (0, tqa - t), (0, 0)))
        gl_p = jnp.pad(gl.reshape(b, t, LANE), ((0, 0), (0, tqa - t), (0, 0))).reshape(b * tqa, LANE)
    rows3 = rows.reshape(b, t, rows.shape[1])
    win3 = win.reshape(b, t, win.shape[1])
    yb = _nsa(q3, rows3, win3, gl_p, tab, lw, t_real=t, paged=paged, lidx=lidx, cwin=cwin)
    if tqa != t:
        yb = yb.reshape(b, tqa, d)[:, :t].reshape(m, d)
    x1, xn1 = _merge(ya, yb, pg, x2, lw["w_o"], g[1:3])
    h = _matmul(xn1, lw["w_up"], BF16, relu2=True, name="ffn_up")
    x_out, xn_out = _ffn_down(h, lw["w_down"], x1, jnp.concatenate([g[3:4], g_next], axis=0))
    wctx = win3 if cwin is None else jnp.concatenate([cwin, win3], axis=1)
    n_keep = min(WINDOW, wctx.shape[1])
    new_rows = rows3 if rows_buf is None else rows_buf
    return x_out.reshape(b, t, d), xn_out, new_rows, wctx[:, wctx.shape[1] - n_keep:], sh, s_fin


def _pair_states(s):
    b, h, n, _ = s.shape
    s = s.reshape(b, h // 2, 2, n, n)
    z = jnp.zeros((b, h // 2, n, n), s.dtype)
    top = jnp.concatenate([s[:, :, 0], z], axis=3)
    bot = jnp.concatenate([z, s[:, :, 1]], axis=3)
    return jnp.concatenate([top, bot], axis=2)


def _unpair_states(s):
    n = HEAD_A
    b, hp = s.shape[:2]
    return jnp.stack([s[:, :, :n, :n], s[:, :, n:, n:]], axis=2).reshape(b, 2 * hp, n, n)


def _trunk(x, shift0, wkv0, paged, cache_win, tab, layers):
    rows, wins, shifts, wkvs = [], [], [], []
    b, t = x.shape[:2]
    ns = 4 * N_KV
    xn = _rmsnorm_cast(x.reshape(-1, x.shape[2]), layers[0]["g"][0:1])
    rows_buf = jnp.zeros((len(layers), b * t * ns, HEAD_B), F32) if (b * t) % LANE == 0 else None
    for l, lw in enumerate(layers):
        g_next = layers[l + 1]["g"][0:1] if l + 1 < len(layers) else lw["g"][3:4]
        x, xn, nr, nw, sh, st = _layer(x, xn, lw, g_next, tab, shift0[l], _pair_states(wkv0[l]),
                                       paged, l, None if cache_win is None else cache_win[l], rows_buf)
        if rows_buf is None:
            rows.append(nr.reshape(b, t, 4, N_KV, HEAD_B))
        else:
            rows_buf = nr
        wins.append(nw.reshape(b, nw.shape[1], 2, N_KV, HEAD_B))
        shifts.append(sh)
        wkvs.append(_unpair_states(st))
    rows = jnp.stack(rows) if rows_buf is None else rows_buf.reshape(len(layers), b, t, 4, N_KV, HEAD_B)
    return x, rows, jnp.stack(wins), jnp.stack(shifts), jnp.stack(wkvs)


def kernel(x_prompt, x_sample, cache_kv, cache_win, state_shift, state_wkv, page_table, w_in, mu, w0, w_up, a0, a_up, g_up, k_k, k_a, r_k, lnx_w, lnx_b, w_oa, cmp_pe, cmp_w1, cmp_w2, w_ob, w_o, w_ff_up, w_ff_down, norm_g, rel_bias):
    depth = w_in.shape[0]
    layers = [_layer_weights(l, w_in, mu, w0, w_up, a0, a_up, g_up, k_k, k_a, r_k, lnx_w, lnx_b, w_oa,
                             cmp_pe, cmp_w1, cmp_w2, w_ob, w_o, w_ff_up, w_ff_down, norm_g) for l in range(depth)]
    tab = rel_bias[_t5_bucket_table()].T
    bp = x_prompt.shape[0]
    rw = state_shift.shape[2]
    zeros_shift = jnp.zeros((depth, bp, rw), x_prompt.dtype)
    zeros_wkv = jnp.zeros((depth, bp) + state_wkv.shape[2:], state_wkv.dtype)
    y_p, kv_p, win_p, sh_p, wkv_p = _trunk(x_prompt, zeros_shift, zeros_wkv, None, None, tab, layers)
    nl, n_pool, page = cache_kv.shape[:3]
    paged = (cache_kv.reshape(nl, n_pool, page, 2, 2 * N_KV, HEAD_B), page_table,
             cache_kv.reshape(nl, n_pool, page * 4 * N_KV, HEAD_B))
    cwin = cache_win.reshape(cache_win.shape[:3] + (-1,))
    y_s, kv_s, win_s, sh_s, wkv_s = _trunk(x_sample, state_shift, state_wkv, paged, cwin, tab, layers)
    return (y_p, y_s, kv_p, kv_s, win_p, win_s, sh_p, sh_s, wkv_p, wkv_s)
```

```python
import functools
import math

import numpy as np
import jax
import jax.numpy as jnp
from jax import lax
from jax.experimental import pallas as pl
from jax.experimental.pallas import tpu as pltpu

F32 = jnp.float32
BF16 = jnp.bfloat16

HEAD_A = 64
DECAY_LORA = 64
A_LORA = 64
GATE_LORA = 160
LNX_EPS = 64e-5
N_KV = 2
HEAD_B = 128
L_CMP = 32
S_CMP = 16
L_SEL = 64
TOP_N = 16
WINDOW = 512
FORCE_BONUS = 1e4
NEG = -1e30
NUM_BUCKETS = 32
MAX_DIST = 128
EPS = 1e-6
PAGE_SIZE = 128

LANE = 128
SUBLANE = 8
VMEM_LIMIT = 56 * 1024 * 1024
RWKV_CHUNK = 64
LORA_W = 384
FAR = LANE - 1


def _cp(n_axes):
    return pltpu.CompilerParams(dimension_semantics=("arbitrary",) * n_axes,
                                vmem_limit_bytes=VMEM_LIMIT)


def _round_up(x, m):
    return -(-x // m) * m


def _tile(n, pref, mult=SUBLANE):
    if n <= pref:
        return n
    for t in range(pref - pref % mult, 0, -mult):
        if n % t == 0:
            return t
    return n


def _dot(a, b):
    return jnp.dot(a, b, preferred_element_type=F32)


def _dot_nt(a, b):
    return lax.dot_general(a, b, (((1,), (1,)), ((), ())), preferred_element_type=F32)


def _dot_tn(a, b):
    return lax.dot_general(a, b, (((0,), (0,)), ((), ())), preferred_element_type=F32)


def _split(x):
    hi = x.astype(BF16)
    lo = (x - hi.astype(F32)).astype(BF16)
    return hi, lo


def _dot_x(a, b):
    hi, lo = _split(a)
    return _dot(hi, b) + _dot(lo, b)


def _dot3(a, b):
    ah, al = _split(a)
    bh, bl = _split(b)
    return _dot(ah, bh) + _dot(ah, bl) + _dot(al, bh)


def _dot3_nt(a, b):
    ah, al = _split(a)
    bh, bl = _split(b)
    return _dot_nt(ah, bh) + _dot_nt(ah, bl) + _dot_nt(al, bh)


def _dot3_tn(a, b):
    ah, al = _split(a)
    bh, bl = _split(b)
    return _dot_tn(ah, bh) + _dot_tn(ah, bl) + _dot_tn(al, bh)


def _sigmoid(x):
    return 1.0 / (1.0 + jnp.exp(-x))


def _rms(y, g):
    return y * lax.rsqrt(jnp.mean(y * y, axis=-1, keepdims=True) + EPS) * g


def _rmsnorm_kernel(x_ref, g_ref, o_ref):
    o_ref[...] = _rms(x_ref[...], g_ref[...]).astype(o_ref.dtype)


def _rmsnorm_cast(x, g):
    m, d = x.shape
    tm = _tile(m, 256)
    return pl.pallas_call(
        _rmsnorm_kernel, grid=(m // tm,),
        in_specs=[pl.BlockSpec((tm, d), lambda i: (i, 0)), pl.BlockSpec((1, d), lambda i: (0, 0))],
        out_specs=pl.BlockSpec((tm, d), lambda i: (i, 0)),
        out_shape=jax.ShapeDtypeStruct((m, d), BF16), compiler_params=_cp(1), name="rmsnorm_cast",
    )(x, g)


def _mm_kernel(a_ref, w_ref, o_ref, *, relu2):
    y = _dot(a_ref[...], w_ref[...])
    if relu2:
        y = jnp.square(jnp.maximum(y, 0.0))
    o_ref[...] = y.astype(o_ref.dtype)


def _matmul(a, w, out_dtype, relu2=False, name="matmul"):
    m, k = a.shape
    n = w.shape[1]
    tm = _tile(m, 1024)
    tn = _tile(n, 1280, LANE)
    return pl.pallas_call(
        functools.partial(_mm_kernel, relu2=relu2), grid=(m // tm, n // tn),
        in_specs=[pl.BlockSpec((tm, k), lambda i, j: (i, 0)), pl.BlockSpec((k, tn), lambda i, j: (0, j))],
        out_specs=pl.BlockSpec((tm, tn), lambda i, j: (i, j)),
        out_shape=jax.ShapeDtypeStruct((m, n), out_dtype), compiler_params=_cp(2), name=name,
    )(a, w)


def _mm_rows_kernel(a_ref, w_ref, buf_ref, o_ref, il_ref):
    del buf_ref
    y = _dot(a_ref[...], w_ref[...])
    o_ref[...] = y
    ns = y.shape[1] // LANE
    for s in range(ns):
        il_ref[pl.ds(s, y.shape[0], stride=ns), :] = y[:, s * LANE:(s + 1) * LANE]


def _matmul_rows(a, w, buf, lidx):
    m, k = a.shape
    n = w.shape[1]
    ns = n // LANE
    tm = _tile(m, 1024)
    return pl.pallas_call(
        _mm_rows_kernel, grid=(m // tm,),
        in_specs=[pl.BlockSpec((tm, k), lambda i: (i, 0)), pl.BlockSpec((k, n), lambda i: (0, 0)),
                  pl.BlockSpec(memory_space=pl.ANY)],
        out_specs=[pl.BlockSpec((tm, n), lambda i: (i, 0)), pl.BlockSpec((None, tm * ns, LANE), lambda i: (lidx, i, 0))],
        out_shape=[jax.ShapeDtypeStruct((m, n), F32), jax.ShapeDtypeStruct(buf.shape, buf.dtype)],
        input_output_aliases={2: 1}, compiler_params=_cp(1), name="proj_rows",
    )(a, w, buf)


def _ffn_down_kernel(h_ref, w_ref, x_ref, g_ref, o_ref, n_ref, acc_ref):
    k = pl.program_id(1)

    @pl.when(k == 0)
    def _():
        acc_ref[...] = jnp.zeros_like(acc_ref)

    acc_ref[...] += _dot(h_ref[...], w_ref[...])

    @pl.when(k == pl.num_programs(1) - 1)
    def _():
        o = x_ref[...] + _rms(acc_ref[...], g_ref[0:1])
        o_ref[...] = o
        n_ref[...] = _rms(o, g_ref[1:2]).astype(n_ref.dtype)


def _ffn_down(h, w, x, g):
    m, kdim = h.shape
    n = w.shape[1]
    tm = _tile(m, 512)
    tk = _tile(kdim, 2048, LANE)
    row = pl.BlockSpec((tm, n), lambda i, k: (i, 0))
    return pl.pallas_call(
        _ffn_down_kernel, grid=(m // tm, kdim // tk),
        in_specs=[pl.BlockSpec((tm, tk), lambda i, k: (i, k)), pl.BlockSpec((tk, n), lambda i, k: (k, 0)),
                  row, pl.BlockSpec((2, n), lambda i, k: (0, 0))],
        out_specs=[row, row],
        out_shape=[jax.ShapeDtypeStruct((m, n), F32), jax.ShapeDtypeStruct((m, n), BF16)],
        scratch_shapes=[pltpu.VMEM((tm, n), F32)], compiler_params=_cp(2), name="ffn_down",
    )(h, w, x, g)


def _merge_kernel(ya_ref, yb_ref, pg_ref, x_ref, w_ref, g_ref, o_ref, n_ref):
    d = ya_ref.shape[1]
    pg = pg_ref[...].astype(F32)
    mix = _sigmoid(pg[:, :d]) * ya_ref[...].astype(F32) + _sigmoid(pg[:, d:]) * yb_ref[...].astype(F32)
    y = _dot(mix.astype(BF16), w_ref[...])
    o = x_ref[...] + _rms(y, g_ref[0:1])
    o_ref[...] = o
    n_ref[...] = _rms(o, g_ref[1:2]).astype(n_ref.dtype)


def _merge(ya, yb, pg, x, w_o, g):
    m, d = x.shape
    tm = _tile(m, 256)
    row = lambda c: pl.BlockSpec((tm, c), lambda i: (i, 0))
    return pl.pallas_call(
        _merge_kernel, grid=(m // tm,),
        in_specs=[row(d), row(d), row(2 * d), row(d),
                  pl.BlockSpec((d, d), lambda i: (0, 0)), pl.BlockSpec((2, d), lambda i: (0, 0))],
        out_specs=[row(d), row(d)],
        out_shape=[jax.ShapeDtypeStruct((m, d), F32), jax.ShapeDtypeStruct((m, d), BF16)],
        compiler_params=_cp(1), name="merge",
    )(ya, yb, pg, x, w_o, g)


def _head_indicator(d_a):
    h = np.zeros((d_a, LANE), np.float32)
    h[np.arange(d_a), np.arange(d_a) // HEAD_A] = 1.0
    return jnp.asarray(h, BF16), jnp.asarray(h.T, BF16)


def _rwkv_prep_kernel(pa_ref, prev_ref, mu_ref, wl_ref, w0_ref, a0_ref, kk_ref, ka_ref, hd_ref, hdt_ref,
                      r_ref, lw_ref, k2_ref, v_ref, kkn_ref, a_ref, g_ref, carry_ref, *, d_a, t_real):
    t = pl.program_id(1)
    tt = pa_ref.shape[1]

    @pl.when(t == 0)
    def _():
        carry_ref[...] = prev_ref[0]

    x = pa_ref[0]
    row = lax.broadcasted_iota(jnp.int32, x.shape, 0)
    xprev = jnp.where(row == 0, carry_ref[...], pltpu.roll(x, 1, 0))
    carry_ref[...] = pa_ref[0, pl.ds(tt - 1, 1), :]
    xs = x + (xprev - x) * mu_ref[...]
    r = xs[:, :d_a]
    k = xs[:, d_a:2 * d_a]
    v = xs[:, 2 * d_a:3 * d_a]
    lo = xs[:, 3 * d_a:]
    lane = lax.broadcasted_iota(jnp.int32, lo.shape, 1)
    act = jnp.where(lane < DECAY_LORA, jnp.tanh(lo),
                    jnp.where(lane < DECAY_LORA + A_LORA, lo, _sigmoid(lo)))
    lin = _dot(act.astype(BF16), wl_ref[...])
    z = -(w0_ref[...] + lin[:, :d_a])
    w = -(jnp.maximum(z, 0.0) + jnp.log(1.0 + jnp.exp(-jnp.abs(z)))) - 0.5
    logw = -jnp.exp(w)
    a = _sigmoid(a0_ref[...] + lin[:, d_a:2 * d_a])
    g = lin[:, 2 * d_a:]
    kkr = k * kk_ref[...]
    ss = _dot_x(kkr * kkr, hd_ref[...])
    inv = 1.0 / jnp.maximum(jnp.sqrt(ss), 1e-12)
    kkn = kkr * _dot_x(inv, hdt_ref[...])
    k2 = k * (1.0 + (a - 1.0) * ka_ref[...])
    if t_real is not None:
        live = (t * tt + lax.broadcasted_iota(jnp.int32, r.shape, 0)) < t_real
        zero = jnp.zeros_like(r)
        r, logw, k2, v, kkn, a = (jnp.where(live, u, zero) for u in (r, logw, k2, v, kkn, a))
    r_ref[0] = r.astype(r_ref.dtype)
    lw_ref[0] = logw
    k2_ref[0] = k2.astype(k2_ref.dtype)
    v_ref[0] = v.astype(v_ref.dtype)
    kkn_ref[0] = kkn.astype(kkn_ref.dtype)
    a_ref[0] = a.astype(a_ref.dtype)
    g_ref[0] = g.astype(g_ref.dtype)


def _rwkv_prep(pa, prev, mu, wl, w0, a0, k_k, k_a, t_real):
    b, t, p = pa.shape
    d_a = w0.shape[1]
    tt = _tile(t, 256)
    hd, hdt = _head_indicator(d_a)
    full = lambda a: pl.BlockSpec(a.shape, lambda i, j: (0,) * a.ndim)
    out = [jax.ShapeDtypeStruct((b, t, d_a), F32 if i == 1 else BF16) for i in range(7)]
    ospec = pl.BlockSpec((1, tt, d_a), lambda i, j: (i, j, 0))
    return pl.pallas_call(
        functools.partial(_rwkv_prep_kernel, d_a=d_a, t_real=t_real), grid=(b, t // tt),
        in_specs=[pl.BlockSpec((1, tt, p), lambda i, j: (i, j, 0)), pl.BlockSpec((1, 1, p), lambda i, j: (i, 0, 0)),
                  full(mu), full(wl), full(w0), full(a0), full(k_k), full(k_a), full(hd), full(hdt)],
        out_specs=[ospec] * 7, out_shape=out,
        scratch_shapes=[pltpu.VMEM((1, p), F32)], compiler_params=_cp(2), name="rwkv_prep",
    )(pa, prev, mu, wl, w0, a0, k_k, k_a, hd, hdt)


def _mmb(a, b, form="nn"):
    dot = {"nn": _dot, "nt": _dot_nt, "tn": _dot_tn}[form]
    return dot(a.astype(BF16), b.astype(BF16))


def _rwkv_scan_kernel(r_ref, lw_ref, k2_ref, v_ref, kk_ref, a_ref, s0_ref, o_ref, sT_ref, s_ref, *, c):
    @pl.when(pl.program_id(1) == 0)
    def _():
        s_ref[...] = s0_ref[0]

    hk = HEAD_A
    pairs = range(s_ref.shape[0])
    sls = [slice(p * LANE, (p + 1) * LANE) for p in pairs]
    lane = lax.broadcasted_iota(jnp.int32, (c, LANE), 1)
    m0 = lane < hk
    rowi = lax.broadcasted_iota(jnp.int32, (c, c), 0)
    coli = lax.broadcasted_iota(jnp.int32, (c, c), 1)
    ltri = (coli <= rowi).astype(BF16)
    lw = [lw_ref[0, :, sl] for sl in sls]
    cum = [_dot_x_rhs(ltri, x) for x in lw]
    e_wi = [jnp.exp(-x) for x in cum]
    kk = [kk_ref[0, :, sl].astype(F32) for sl in sls]
    at = [-k * jnp.exp(x - y) for k, x, y in zip(kk, cum, lw)]
    bt = [k * a_ref[0, :, sl].astype(F32) * e for k, sl, e in zip(kk, sls, e_wi)]
    kt = [k2_ref[0, :, sl].astype(F32) * e for sl, e in zip(sls, e_wi)]
    rt = [r_ref[0, :, sl].astype(F32) * jnp.exp(x) for sl, x in zip(sls, cum)]
    v = [v_ref[0, :, sl] for sl in sls]
    w_c = [jnp.exp(x[c - 1:c, :]) for x in cum]

    def bd(z):
        zero = jnp.zeros_like(z)
        return jnp.concatenate([jnp.where(m0, z, zero), jnp.where(m0, zero, z)], axis=0)

    prow = lax.broadcasted_iota(jnp.int32, (c, 2 * c), 0)
    pcol = lax.broadcasted_iota(jnp.int32, (c, 2 * c), 1)
    pcol = jnp.where(pcol >= c, pcol - c, pcol)
    strict = pcol < prow
    incl = pcol <= prow
    zero_cc = jnp.zeros((c, 2 * c), F32)
    x2 = [jnp.concatenate([x, y], axis=0) for x, y in zip(at, rt)]
    xb = [_mmb(x, bd(y), "nt") for x, y in zip(x2, bt)]
    xk = [_mmb(x, bd(y), "nt") for x, y in zip(x2, kt)]
    a_ab = [jnp.where(strict, x[:c], zero_cc) for x in xb]
    a_rb = [jnp.where(incl, x[c:], zero_cc) for x in xb]
    a_ak = [jnp.where(strict, x[:c], zero_cc) for x in xk]
    a_rk = [jnp.where(incl, x[c:], zero_cc) for x in xk]

    if 2 * c == LANE:
        bdc = bd
    else:
        mc = lax.broadcasted_iota(jnp.int32, (c, 2 * c), 1) < c

        def bdc(z):
            zero = jnp.zeros_like(z)
            return jnp.concatenate([jnp.where(mc, z, zero), jnp.where(mc, zero, z)], axis=0)

    eye = (pcol == prow).astype(F32)
    tmat = [eye + x for x in a_ab]
    npow = a_ab
    steps = 1
    while 2 * steps < c:
        npow = [_dot3(x, bdc(x)) for x in npow]
        tmat = [x + _dot3(x, bdc(y)) for x, y in zip(tmat, npow)]
        steps *= 2
    ta = [_mmb(x, bd(y)) for x, y in zip(tmat, at)]
    xv = [_mmb(x, bd(y)) for x, y in zip(a_ak, v)]
    tx = [_mmb(x, bd(y)) for x, y in zip(tmat, xv)]
    p_c = [x + _mmb(y, bd(z)) for x, y, z in zip(rt, a_rb, ta)]
    q_c = [_mmb(x, bd(y)) + _mmb(z, bd(u)) for x, y, z, u in zip(a_rb, tx, a_rk, v)]
    lr = lax.broadcasted_iota(jnp.int32, (LANE, LANE), 0)
    lc = lax.broadcasted_iota(jnp.int32, (LANE, LANE), 1)
    same = (lr < hk) == (lc < hk)
    eye_l = (lr == lc).astype(F32)
    zero_l = jnp.zeros((LANE, LANE), F32)
    m_c = [(eye_l + jnp.where(same, _mmb(x, y, "tn"), zero_l)) * w for x, y, w in zip(ta, bt, w_c)]
    n_c = [jnp.where(same, _mmb(x, y, "tn") + _mmb(z, u, "tn"), zero_l) * w
           for x, y, z, u, w in zip(tx, bt, v, kt, w_c)]
    s = [s_ref[p] for p in pairs]
    o = [_mmb(x, y, "nt") + z for x, y, z in zip(p_c, s, q_c)]
    s_new = [_mmb(x, y) + z for x, y, z in zip(s, m_c, n_c)]
    for p in pairs:
        o_ref[0, :, sls[p]] = o[p]
        s_ref[p] = s_new[p]
        sT_ref[0, p] = s_new[p]


def _dot_x_rhs(a, b):
    hi, lo = _split(b)
    return _dot(a, hi) + _dot(a, lo)


def _rwkv_scan(r, lw, k2, v, kk, a, s0, c):
    b, t, d_a = r.shape
    npair = d_a // LANE
    seq = pl.BlockSpec((1, c, d_a), lambda i, j: (i, j, 0))
    st = pl.BlockSpec((1, npair, LANE, LANE), lambda i, j: (i, 0, 0, 0))
    return pl.pallas_call(
        functools.partial(_rwkv_scan_kernel, c=c), grid=(b, t // c),
        in_specs=[seq] * 6 + [st], out_specs=[seq, st],
        out_shape=[jax.ShapeDtypeStruct((b, t, d_a), F32), jax.ShapeDtypeStruct((b, npair, LANE, LANE), F32)],
        scratch_shapes=[pltpu.VMEM((npair, LANE, LANE), F32)], compiler_params=_cp(2), name="rwkv_scan",
    )(r, lw, k2, v, kk, a, s0)


def _rwkv_post_kernel(o_ref, r_ref, k2_ref, v_ref, g_ref, lw_ref, lb_ref, rk_ref, hd_ref, hdt_ref, w_ref, y_ref):
    o = o_ref[...]
    hd = hd_ref[...]
    hdt = hdt_ref[...]
    inv_n = 1.0 / HEAD_A
    mean = _dot_x(_dot_x(o, hd) * inv_n, hdt)
    d = o - mean
    var = _dot_x(d * d, hd) * inv_n
    xo = d * _dot_x(lax.rsqrt(var + LNX_EPS), hdt) * lw_ref[...] + lb_ref[...]
    rk2 = r_ref[...].astype(F32) * k2_ref[...].astype(F32) * rk_ref[...]
    bonus = _dot_x(_dot_x(rk2, hd), hdt) * v_ref[...].astype(F32)
    y_ref[...] = _dot(((xo + bonus) * g_ref[...].astype(F32)).astype(BF16), w_ref[...]).astype(y_ref.dtype)


def _rwkv_post(o, r, k2, v, g, lnx_w, lnx_b, r_k, w_oa):
    m, d_a = o.shape
    n = w_oa.shape[1]
    tm = _tile(m, 256)
    hd, hdt = _head_indicator(d_a)
    row = pl.BlockSpec((tm, d_a), lambda i: (i, 0))
    full = lambda a: pl.BlockSpec(a.shape, lambda i: (0,) * a.ndim)
    return pl.pallas_call(
        _rwkv_post_kernel, grid=(m // tm,),
        in_specs=[row] * 5 + [full(lnx_w), full(lnx_b), full(r_k), full(hd), full(hdt), full(w_oa)],
        out_specs=pl.BlockSpec((tm, n), lambda i: (i, 0)), out_shape=jax.ShapeDtypeStruct((m, n), BF16),
        compiler_params=_cp(1), name="rwkv_post",
    )(o, r, k2, v, g, lnx_w, lnx_b, r_k, hd, hdt, w_oa)


def _t5_bucket_table():
    d = np.arange(LANE)
    max_exact = NUM_BUCKETS // 2
    df = np.maximum(d, 1).astype(np.float32)
    large = max_exact + (np.log(df / np.float32(max_exact)) / np.float32(math.log(MAX_DIST / max_exact))
                         * np.float32(NUM_BUCKETS - max_exact)).astype(np.int32)
    large = np.minimum(large, NUM_BUCKETS - 1)
    tab = np.where(d < max_exact, d, large)
    assert tab[FAR] == NUM_BUCKETS - 1
    return tab


PAGES_PER_STEP = 8


def _page_specs(cache, lidx, npp, half=None):
    if half is None:
        blk = (None, None) + cache.shape[2:]
        return [pl.BlockSpec(blk, lambda i, p, pt, k=k: (lidx, pt[i, p * npp + k], 0, 0)) for k in range(npp)]
    blk = (None, None, cache.shape[2], None) + cache.shape[4:]
    return [pl.BlockSpec(blk, lambda i, p, pt, k=k: (lidx, pt[i, p * npp + k], 0, half, 0, 0)) for k in range(npp)]


def _gelu_tanh(x):
    return 0.5 * x * (1.0 + jnp.tanh(math.sqrt(2.0 / math.pi) * (x + 0.044715 * (x * x * x))))


def _compress_finish(y, ex_ref, w1, w2_ref, o_ref):
    rr = y.shape[0]
    e = _dot(ex_ref[0].astype(BF16), w1)
    nxt = pltpu.roll(y[:, LANE:], rr - 1, 0)
    row = lax.broadcasted_iota(jnp.int32, nxt.shape, 0)
    nxt = jnp.where(row == rr - 1, e[0:1, LANE:], nxt)
    pre = y[:, :LANE] + nxt + (e[1:2, :LANE] + e[2:3, LANE:])
    o_ref[0] = _dot(_gelu_tanh(pre).astype(BF16), w2_ref[0])


def _compress_kernel(x_ref, ex_ref, w1_ref, w2_ref, o_ref):
    rr = x_ref.shape[0] // S_CMP
    w1 = w1_ref[0]
    y = jnp.zeros((rr, 2 * LANE), F32)
    for j in range(S_CMP):
        xj = x_ref[pl.ds(j, rr, stride=S_CMP), :].astype(BF16)
        y = y + _dot(xj, w1[j * LANE:(j + 1) * LANE, :])
    _compress_finish(y, ex_ref, w1, w2_ref, o_ref)


def _compress_finish_kernel(y_ref, ex_ref, w1_ref, w2_ref, o_ref):
    _compress_finish(y_ref[0], ex_ref, w1_ref[0], w2_ref, o_ref)


def _compress_paged_kernel(pt_ref, *refs):
    x_refs, (w1_ref, y_ref) = refs[:-2], refs[-2:]
    page, ncmp, _ = x_refs[0].shape
    cpp = page // S_CMP
    for s in range(ncmp):
        y = jnp.zeros((len(x_refs) * cpp, 2 * LANE), F32)
        for j in range(S_CMP):
            xj = jnp.concatenate([x[pl.ds(j, cpp, stride=S_CMP), s, :] for x in x_refs], axis=0)
            y = y + _dot(xj.astype(BF16), w1_ref[s // N_KV, j * LANE:(j + 1) * LANE, :])
        y_ref[0, s] = y


def _compress(x, x_spec, n_pos, ex, w1cat, w2):
    nb, _, cw = ex.shape
    rr = n_pos // S_CMP
    c_of = lambda i: (i // N_KV) % 2
    return pl.pallas_call(
        _compress_kernel, grid=(nb,),
        in_specs=[x_spec, pl.BlockSpec((1, SUBLANE, cw), lambda i: (i, 0, 0)),
                  pl.BlockSpec((1, cw, 2 * LANE), lambda i: (c_of(i), 0, 0)),
                  pl.BlockSpec((1, LANE, LANE), lambda i: (c_of(i), 0, 0))],
        out_specs=pl.BlockSpec((1, rr, LANE), lambda i: (i, 0, 0)),
        out_shape=jax.ShapeDtypeStruct((nb, rr, LANE), F32), compiler_params=_cp(1), name="nsa_compress",
    )(x, ex, w1cat, w2)


def _compress_paged(cache, lidx, page_table, ex, w1cat, w2):
    b, n_pages = page_table.shape
    ncmp = cache.shape[4]
    cw = ex.shape[2]
    page = cache.shape[2]
    npp = next(k for k in (PAGES_PER_STEP, 4, 2, 1) if n_pages % k == 0)
    rr = n_pages * page // S_CMP
    rt = npp * page // S_CMP
    grid_spec = pltpu.PrefetchScalarGridSpec(
        num_scalar_prefetch=1, grid=(b, n_pages // npp),
        in_specs=_page_specs(cache, lidx, npp, 0) + [pl.BlockSpec(w1cat.shape, lambda i, p, pt: (0, 0, 0))],
        out_specs=pl.BlockSpec((1, ncmp, rt, 2 * LANE), lambda i, p, pt: (i, 0, p, 0)))
    y = pl.pallas_call(
        _compress_paged_kernel, grid_spec=grid_spec,
        out_shape=jax.ShapeDtypeStruct((b, ncmp, rr, 2 * LANE), F32),
        compiler_params=_cp(2), name="nsa_compress_paged",
    )(page_table, *([cache] * npp), w1cat)
    c_of = lambda i: (i // N_KV) % 2
    return pl.pallas_call(
        _compress_finish_kernel, grid=(b * ncmp,),
        in_specs=[pl.BlockSpec((1, rr, 2 * LANE), lambda i: (i, 0, 0)), pl.BlockSpec((1, SUBLANE, cw), lambda i: (i, 0, 0)),
                  pl.BlockSpec((1, cw, 2 * LANE), lambda i: (c_of(i), 0, 0)),
                  pl.BlockSpec((1, LANE, LANE), lambda i: (c_of(i), 0, 0))],
        out_specs=pl.BlockSpec((1, rr, LANE), lambda i: (i, 0, 0)),
        out_shape=jax.ShapeDtypeStruct((b * ncmp, rr, LANE), F32), compiler_params=_cp(1), name="nsa_compress_finish",
    )(y.reshape(b * ncmp, rr, 2 * LANE), ex, w1cat, w2)


def _bias_gather(tab_row, dist):
    idx = jnp.clip(dist, 0, FAR)
    return jnp.take_along_axis(jnp.broadcast_to(tab_row, idx.shape), idx, axis=1)


def _stack_heads(q_ref, col0=0, hpg=None):
    hpg = q_ref.shape[1] // HEAD_B if hpg is None else hpg
    return jnp.concatenate([q_ref[:, col0 + h * HEAD_B:col0 + (h + 1) * HEAD_B] for h in range(hpg)], axis=0)


def _rel_bias(dist, g, hpg, tab_ref):
    rel = []
    for h in range(hpg):
        tab_row = tab_ref[pl.ds(g * hpg + h, 1), :]
        far = tab_row[:, FAR:FAR + 1]
        rel.append(jnp.concatenate([_bias_gather(tab_row, dist[:, c * LANE:(c + 1) * LANE]) - far
                                    for c in range(dist.shape[1] // LANE)], axis=1))
    return jnp.stack(rel)


def _online_softmax_update(qs, tiles):
    hpg, tq, _ = tiles[0][4].shape
    scores = [_dot_nt(qs, kb) for kb, *_ in tiles]
    probs, alphas = [], []
    for s, (kb, vb, mask, rel, m_ref, l_ref, acc_ref) in zip(scores, tiles):
        width = kb.shape[0]
        s = s.reshape(hpg, tq, width) * (HEAD_B ** -0.5)
        if rel is None:
            s = s + jnp.where(mask, 0.0, 2 * NEG)[None]
        else:
            s = s + jnp.where(mask[None], rel, 2 * NEG)
        m_prev = m_ref[...][:, :, :1]
        m_new = jnp.maximum(m_prev, jnp.max(s, axis=2, keepdims=True))
        alpha = jnp.exp(m_prev - m_new)
        p = jnp.exp(s - m_new)
        m_ref[...] = jnp.broadcast_to(m_new, m_ref.shape)
        probs.append(p.reshape(hpg * tq, width).astype(BF16))
        alphas.append(alpha)
    pvs = [_dot(p, jnp.concatenate([t[1], jnp.ones_like(t[1])], axis=1)) for p, t in zip(probs, tiles)]
    for pv, alpha, t in zip(pvs, alphas, tiles):
        pv = pv.reshape(hpg, tq, 2 * HEAD_B)
        t[5][...] = alpha * t[5][...] + pv[:, :, HEAD_B:]
        t[6][...] = alpha * t[6][...] + pv[:, :, :HEAD_B]


def _cmp_attn_kernel(q_ref, kc_ref, vc_ref, ovt_ref, tab_ref, o_ref, imp_ref, *, tq, q0, lanes):
    g = pl.program_id(1)
    qt = pl.program_id(2)
    hpg = q_ref.shape[2] // HEAD_B
    rr = kc_ref.shape[1]
    kc = kc_ref[0].astype(BF16)
    vc = vc_ref[0].astype(BF16)
    qbase = q0 + qt * tq
    qrow = qbase + lax.broadcasted_iota(jnp.int32, (tq, LANE), 0)
    dists = []
    for cix in range(rr // LANE):
        c_end = S_CMP * (cix * LANE + lax.broadcasted_iota(jnp.int32, (tq, LANE), 1)) + (L_CMP - 1)
        dists.append(qrow - c_end)
    mask = (jnp.concatenate(dists, axis=1) >= 0)[None]
    bias = jnp.stack([jnp.concatenate([_bias_gather(tab_ref[pl.ds(g * hpg + h, 1), :], d) for d in dists], axis=1)
                      for h in range(hpg)])
    s = _dot_nt(_stack_heads(q_ref.at[0]), kc).reshape(hpg, tq, rr) * (HEAD_B ** -0.5) + bias
    s = jnp.where(mask, s, NEG)
    e = jnp.where(mask, jnp.exp(s - jnp.max(s, axis=2, keepdims=True)), 0.0)
    p = e / jnp.maximum(jnp.sum(e, axis=2, keepdims=True), 1e-30)
    o = _dot(p.reshape(hpg * tq, rr).astype(BF16), vc).reshape(hpg, tq, HEAD_B)
    o_ref[0] = jnp.concatenate([o[h] for h in range(hpg)], axis=1)
    psum = jnp.sum(p, axis=0)
    if tq < lanes:
        psum = jnp.concatenate([psum, jnp.zeros((lanes - tq, rr), F32)], axis=0)
    ph, plo = _split(psum)
    ovt = ovt_ref[...]
    imp_ref[0, 0] = _dot_nt(ovt, ph) + _dot_nt(ovt, plo)


def _select_kernel(imp_ref, qpos_ref, selt_ref, *, n_top):
    nsr, lanes = imp_ref.shape[2:]
    j = lax.broadcasted_iota(jnp.int32, (nsr, lanes), 0)
    cur = qpos_ref[...] // L_SEL
    valid = j <= cur
    forced = (j == 0) | (j == cur) | (j == cur - 1)
    score = jnp.where(valid, imp_ref[0, 0] + jnp.where(forced, FORCE_BONUS, 0.0), NEG)

    def body(_, carry):
        score, sel = carry
        top = jnp.max(score, axis=0, keepdims=True)
        first = jnp.min(jnp.where(score == top, j, nsr), axis=0, keepdims=True)
        pick = j == first
        sel = jnp.where(pick & (top > NEG / 2), 1.0, sel)
        return jnp.where(pick, 2 * NEG, score), sel

    _, sel = lax.fori_loop(0, n_top, body, (score, jnp.zeros((nsr, lanes), F32)))
    selt_ref[0, 0] = sel


def _select(imp, qpos, ns):
    x, gg, nsr, l = imp.shape
    lt = next(c for c in (256, 128) if l % c == 0)
    blk = pl.BlockSpec((1, 1, nsr, lt), lambda i, g, t: (i, g, 0, t))
    return pl.pallas_call(
        functools.partial(_select_kernel, n_top=min(TOP_N, ns)), grid=(x, gg, l // lt),
        in_specs=[blk, pl.BlockSpec((1, lt), lambda i, g, t: (0, t))], out_specs=blk,
        out_shape=jax.ShapeDtypeStruct(imp.shape, F32), compiler_params=_cp(3), name="nsa_select",
    )(imp, qpos)


def _cmp_attn(q, kcv, tab, *, t_keys, q0, tq):
    b, tqa, d_b = q.shape
    rr = kcv.shape[1]
    gw = d_b // N_KV
    ns = -(-t_keys // L_SEL)
    nsr = _round_up(ns, SUBLANE)
    lanes = max(tq, LANE)
    nqt = tqa // tq
    ci = np.arange(rr)
    sj = np.arange(nsr)
    ov = ((S_CMP * ci) // L_SEL)[None, :] == sj[:, None]
    ov |= ((S_CMP * ci + L_CMP - 1) // L_SEL)[None, :] == sj[:, None]
    ovt = jnp.asarray(ov.astype(np.float32), BF16)
    kern = functools.partial(_cmp_attn_kernel, tq=tq, q0=q0, lanes=lanes)
    return pl.pallas_call(
        kern, grid=(b, N_KV, nqt),
        in_specs=[pl.BlockSpec((1, tq, gw), lambda i, g, t: (i, t, g)),
                  pl.BlockSpec((1, rr, LANE), lambda i, g, t: (i * 4 + g, 0, 0)),
                  pl.BlockSpec((1, rr, LANE), lambda i, g, t: (i * 4 + N_KV + g, 0, 0)),
                  pl.BlockSpec(ovt.shape, lambda i, g, t: (0, 0)),
                  pl.BlockSpec(tab.shape, lambda i, g, t: (0, 0))],
        out_specs=[pl.BlockSpec((1, tq, gw), lambda i, g, t: (i, t, g)),
                   pl.BlockSpec((1, 1, nsr, lanes), lambda i, g, t: (i, g, 0, t))],
        out_shape=[jax.ShapeDtypeStruct((b, tqa, d_b), F32),
                   jax.ShapeDtypeStruct((b, N_KV, nsr, nqt * lanes), F32)],
        compiler_params=_cp(3), name="nsa_cmp_attn",
    )(q, kcv, kcv, ovt, tab)


def _flash_kernel(*refs, mode, band, tq, tk, nj, q0, kbase, n_tail, tail_pos0):
    refs = list(refs)
    q_ref, k_ref, v_ref = refs[:3]
    pos = 3
    if n_tail:
        kt_ref, vt_ref = refs[pos:pos + 2]
        pos += 2
    if mode == "sel":
        sel_ref = refs[pos]
        pos += 1
    tab_ref, o_ref, m_ref, l_ref, acc_ref = refs[pos:pos + 5]
    g = pl.program_id(1)
    qt = pl.program_id(2)
    j = pl.program_id(3)
    hpg = q_ref.shape[1] // HEAD_B
    qbase = q0 + qt * tq
    qs = _stack_heads(q_ref)

    @pl.when(j == 0)
    def _():
        m_ref[...] = jnp.full_like(m_ref, NEG)
        l_ref[...] = jnp.zeros_like(l_ref)
        acc_ref[...] = jnp.zeros_like(acc_ref)

    def update(k_r, v_r, kpos0, width, n_valid, near):
        qrow = qbase + lax.broadcasted_iota(jnp.int32, (tq, width), 0)
        kcol = kpos0 + lax.broadcasted_iota(jnp.int32, (tq, width), 1)
        dist = qrow - kcol
        mask = dist >= 0
        if n_valid is not None:
            mask &= lax.broadcasted_iota(jnp.int32, (tq, width), 1) < n_valid
        if mode == "win":
            mask &= dist < WINDOW
        else:
            nsp = sel_ref.shape[1]
            blk = (kpos0 + lax.broadcasted_iota(jnp.int32, (nsp, width), 1)) // L_SEL
            expand = (blk == lax.broadcasted_iota(jnp.int32, (nsp, width), 0)).astype(BF16)
            mask &= _dot(sel_ref[...].astype(BF16), expand) > 0.5
        kb = k_r[...].astype(BF16)
        vb = v_r[...].astype(BF16)
        rel = _rel_bias(dist, g, hpg, tab_ref) if near else None
        _online_softmax_update(qs, [(kb, vb, mask, rel, m_ref, l_ref, acc_ref)])

    if band:
        kt_abs = qt - (nj - 1) + j
        active = kt_abs >= 0
    else:
        kt_abs = j
        active = kbase + j * tk <= qbase + tq - 1
    kpos0 = kbase + kt_abs * tk
    is_far = qbase - (kpos0 + tk - 1) >= FAR

    @pl.when(active & is_far)
    def _():
        update(k_ref, v_ref, kpos0, tk, None, False)

    @pl.when(active & jnp.logical_not(is_far))
    def _():
        update(k_ref, v_ref, kpos0, tk, None, True)

    @pl.when(j == nj - 1)
    def _():
        if n_tail:
            update(kt_ref, vt_ref, tail_pos0, kt_ref.shape[0], n_tail, True)
        outs = [acc_ref[h] / jnp.maximum(l_ref[h][:, :1], 1e-30) for h in range(hpg)]
        o_ref[...] = jnp.concatenate(outs, axis=1)


def _sel_paged_kernel(pt_ref, *refs, ns, n_tail, p_len):
    npp = len(refs) - 8
    q_ref, x_refs = refs[0], refs[1:1 + npp]
    tail_ref, sel_ref, tab_ref, o_ref, m_ref, l_ref, acc_ref = refs[1 + npp:]
    j = pl.program_id(1)
    nj = pl.num_programs(1)
    tq = q_ref.shape[0]
    gw = q_ref.shape[1] // N_KV
    hpg = gw // HEAD_B
    page = x_refs[0].shape[0] // ns
    width = npp * page

    @pl.when(j == 0)
    def _():
        m_ref[...] = jnp.full_like(m_ref, NEG)
        l_ref[...] = jnp.zeros_like(l_ref)
        acc_ref[...] = jnp.zeros_like(acc_ref)

    def fold(g, kb, vb, kpos0, n_valid, near):
        w = kb.shape[0]
        col = lax.broadcasted_iota(jnp.int32, (tq, w), 1)
        dist = p_len + lax.broadcasted_iota(jnp.int32, (tq, w), 0) - (kpos0 + col)
        mask = dist >= 0
        if n_valid is not None:
            mask &= col < n_valid
        nsp = sel_ref.shape[2]
        blk = (kpos0 + lax.broadcasted_iota(jnp.int32, (nsp, w), 1)) // L_SEL
        expand = (blk == lax.broadcasted_iota(jnp.int32, (nsp, w), 0)).astype(BF16)
        mask &= _dot(sel_ref[g].astype(BF16), expand) > 0.5
        rel = _rel_bias(dist, g, hpg, tab_ref) if near else None
        _online_softmax_update(_stack_heads(q_ref, g * gw, hpg),
                               [(kb, vb, mask, rel, m_ref.at[g], l_ref.at[g], acc_ref.at[g])])

    def pages(stream):
        return jnp.concatenate([x[pl.ds(stream, page, stride=ns), :] for x in x_refs], axis=0).astype(BF16)

    def main(near):
        for g in range(N_KV):
            fold(g, pages(2 * N_KV + g), pages(3 * N_KV + g), j * width, None, near)

    @pl.when(j < nj - 1)
    def _():
        main(False)

    @pl.when(j == nj - 1)
    def _():
        main(True)
        for g in range(N_KV):
            kt = tail_ref[:, (2 * N_KV + g) * LANE:(2 * N_KV + g + 1) * LANE].astype(BF16)
            vt = tail_ref[:, (3 * N_KV + g) * LANE:(3 * N_KV + g + 1) * LANE].astype(BF16)
            fold(g, kt, vt, p_len, n_tail, True)
        out = acc_ref[...] / jnp.maximum(l_ref[...][:, :, :, :1], 1e-30)
        o_ref[...] = jnp.concatenate([out[g, h] for g in range(N_KV) for h in range(hpg)], axis=1)


def _sel_paged(q, cache, lidx, page_table, rows_t, n_tail, sel, tab):
    b, tq, d_b = q.shape
    n_pages = page_table.shape[1]
    ns = 4 * N_KV
    page = cache.shape[2] // ns
    npp = next(k for k in (PAGES_PER_STEP, 4, 2, 1) if n_pages % k == 0)
    assert npp * page > FAR
    hpg = d_b // N_KV // HEAD_B
    whole = lambda a: pl.BlockSpec((None,) + a.shape[1:], lambda i, p, pt: (i,) + (0,) * (a.ndim - 1))
    grid_spec = pltpu.PrefetchScalarGridSpec(
        num_scalar_prefetch=1, grid=(b, n_pages // npp),
        in_specs=[whole(q)] + _page_specs(cache, lidx, npp) + [whole(rows_t), whole(sel),
                  pl.BlockSpec(tab.shape, lambda i, p, pt: (0, 0))],
        out_specs=pl.BlockSpec((None, tq, d_b), lambda i, p, pt: (i, 0, 0)),
        scratch_shapes=[pltpu.VMEM((N_KV, hpg, tq, LANE), F32), pltpu.VMEM((N_KV, hpg, tq, LANE), F32),
                        pltpu.VMEM((N_KV, hpg, tq, HEAD_B), F32)])
    return pl.pallas_call(
        functools.partial(_sel_paged_kernel, ns=ns, n_tail=n_tail, p_len=n_pages * page), grid_spec=grid_spec,
        out_shape=jax.ShapeDtypeStruct((b, tq, d_b), F32), compiler_params=_cp(2), name="nsa_sel_attn_paged",
    )(page_table, q, *([cache] * npp), rows_t, sel, tab)


def _flash(q, k_arr, k_spec, v_arr, v_spec, tab, *, mode, band, tq, tk, nj, q0, kbase,
           tail=None, sel=None, name):
    b, tqa, d_b = q.shape
    gw = d_b // N_KV
    hpg = gw // HEAD_B
    nqt = tqa // tq
    args = [q, k_arr, v_arr]
    in_specs = [pl.BlockSpec((None, tq, gw), lambda i, g, t, j: (i, t, g)), k_spec, v_spec]
    n_tail, tail_pos0 = 0, 0
    if tail is not None:
        tk_arr, tk_spec, tv_arr, tv_spec, n_tail, tail_pos0 = tail
        args += [tk_arr, tv_arr]
        in_specs += [tk_spec, tv_spec]
    if mode == "sel":
        args.append(sel)
        in_specs.append(pl.BlockSpec((None, None, tq, sel.shape[3]), lambda i, g, t, j: (i, g, t, 0)))
    args.append(tab)
    in_specs.append(pl.BlockSpec(tab.shape, lambda i, g, t, j: (0, 0)))
    kern = functools.partial(_flash_kernel, mode=mode, band=band, tq=tq, tk=tk, nj=nj, q0=q0, kbase=kbase,
                             n_tail=n_tail, tail_pos0=tail_pos0)
    return pl.pallas_call(
        kern, grid=(b, N_KV, nqt, nj), in_specs=in_specs,
        out_specs=pl.BlockSpec((None, tq, gw), lambda i, g, t, j: (i, t, g)),
        out_shape=jax.ShapeDtypeStruct((b, tqa, d_b), F32),
        scratch_shapes=[pltpu.VMEM((hpg, tq, LANE), F32), pltpu.VMEM((hpg, tq, LANE), F32),
                        pltpu.VMEM((hpg, tq, HEAD_B), F32)],
        compiler_params=_cp(4), name=name,
    )(*args)


def _attn_res_kernel(*refs, mode, tq, ch, far_w):
    if mode == "sel":
        q_ref, k_ref, v_ref, sel_ref, tab_ref, o_ref, m_ref, l_ref, acc_ref = refs
    else:
        q_ref, k_ref, v_ref, tab_ref, o_ref, m_ref, l_ref, acc_ref = refs
    g = pl.program_id(1)
    qt = pl.program_id(2)
    hpg = q_ref.shape[1] // HEAD_B
    qbase = qt * tq
    near0 = qbase - LANE
    qs = _stack_heads(q_ref)
    m_ref[...] = jnp.full_like(m_ref, NEG)
    l_ref[...] = jnp.zeros_like(l_ref)
    acc_ref[...] = jnp.zeros_like(acc_ref)

    def tile(stream, kpos0, width, near):
        qrow = qbase + lax.broadcasted_iota(jnp.int32, (tq, width), 0)
        kcol = kpos0 + lax.broadcasted_iota(jnp.int32, (tq, width), 1)
        dist = qrow - kcol
        mask = (dist >= 0) if near else (kcol < near0)
        if mode == "win":
            mask &= dist < WINDOW
        else:
            nsp = sel_ref.shape[1]
            blk = (kpos0 + lax.broadcasted_iota(jnp.int32, (nsp, width), 1)) // L_SEL
            expand = (blk == lax.broadcasted_iota(jnp.int32, (nsp, width), 0)).astype(BF16)
            mask &= _dot(sel_ref[...].astype(BF16), expand) > 0.5
        kb = k_ref[pl.ds(kpos0, width), :].astype(BF16)
        vb = v_ref[pl.ds(kpos0, width), :].astype(BF16)
        rel = _rel_bias(dist, g, hpg, tab_ref) if near else None
        return (kb, vb, mask, rel, m_ref.at[stream], l_ref.at[stream], acc_ref.at[stream])

    near = lambda: tile(1, pl.multiple_of(jnp.maximum(near0, 0), LANE), 2 * LANE, True)
    if mode == "sel":
        def body(c, carry):
            _online_softmax_update(qs, [tile(0, pl.multiple_of(c * ch, ch), ch, False)])
            return carry

        lax.fori_loop(0, (jnp.maximum(near0, 0) + ch - 1) // ch, body, 0)
        _online_softmax_update(qs, [near()])
    elif far_w:
        far = tile(0, pl.multiple_of(jnp.maximum(qbase - WINDOW, 0), LANE), far_w, False)
        _online_softmax_update(qs, [far, near()])
    else:
        _online_softmax_update(qs, [near()])
    m0, m1 = m_ref[0][:, :, :1], m_ref[1][:, :, :1]
    m = jnp.maximum(m0, m1)
    w0, w1 = jnp.exp(m0 - m), jnp.exp(m1 - m)
    l = w0 * l_ref[0][:, :, :1] + w1 * l_ref[1][:, :, :1]
    out = (w0 * acc_ref[0] + w1 * acc_ref[1]) / jnp.maximum(l, 1e-30)
    o_ref[...] = jnp.concatenate([out[h] for h in range(hpg)], axis=1)


def _attn_res(q, kv, kcol, vcol, tab, *, mode, sel=None, name):
    b, t, d_b = q.shape
    gw = d_b // N_KV
    hpg = gw // HEAD_B
    tq = LANE
    assert t % tq == 0 and t >= 2 * LANE
    ch = next(c for c in (1024, 512, 256, 128) if t % c == 0)
    far_w = min(WINDOW - LANE, t - 2 * LANE)
    args = [q, kv, kv]
    in_specs = [pl.BlockSpec((None, tq, gw), lambda i, g, qt: (i, qt, g)),
                pl.BlockSpec((None, t, LANE), lambda i, g, qt: (i, 0, kcol + g)),
                pl.BlockSpec((None, t, LANE), lambda i, g, qt: (i, 0, vcol + g))]
    if mode == "sel":
        args.append(sel)
        in_specs.append(pl.BlockSpec((None, None, tq, sel.shape[3]), lambda i, g, qt: (i, g, qt, 0)))
    args.append(tab)
    in_specs.append(pl.BlockSpec(tab.shape, lambda i, g, qt: (0, 0)))
    return pl.pallas_call(
        functools.partial(_attn_res_kernel, mode=mode, tq=tq, ch=ch, far_w=far_w),
        grid=(b, N_KV, t // tq), in_specs=in_specs,
        out_specs=pl.BlockSpec((None, tq, gw), lambda i, g, qt: (i, qt, g)),
        out_shape=jax.ShapeDtypeStruct((b, t, d_b), F32),
        scratch_shapes=[pltpu.VMEM((2, hpg, tq, LANE), F32), pltpu.VMEM((2, hpg, tq, LANE), F32),
                        pltpu.VMEM((2, hpg, tq, HEAD_B), F32)],
        compiler_params=_cp(3), name=name,
    )(*args)


def _nsa_out_kernel(oc_ref, os_ref, ow_ref, gl_ref, e_ref, w_ref, y_ref):
    gate = _sigmoid(gl_ref[...])
    o = (_dot_x(gate, e_ref[0]) * oc_ref[...] + _dot_x(gate, e_ref[1]) * os_ref[...]
         + _dot_x(gate, e_ref[2]) * ow_ref[...])
    y_ref[...] = _dot(o.astype(BF16), w_ref[...]).astype(y_ref.dtype)


def _nsa_out(o_c, o_s, o_w, gl, w_ob):
    m, d_b = o_c.shape
    n = w_ob.shape[1]
    h_b = d_b // HEAD_B
    tm = _tile(m, 256)
    e = np.zeros((3, LANE, d_b), np.float32)
    for br in range(3):
        for hh in range(h_b):
            e[br, br * h_b + hh, hh * HEAD_B:(hh + 1) * HEAD_B] = 1.0
    e = jnp.asarray(e, BF16)
    row = pl.BlockSpec((tm, d_b), lambda i: (i, 0))
    return pl.pallas_call(
        _nsa_out_kernel, grid=(m // tm,),
        in_specs=[row, row, row, pl.BlockSpec((tm, LANE), lambda i: (i, 0)),
                  pl.BlockSpec(e.shape, lambda i: (0, 0, 0)), pl.BlockSpec(w_ob.shape, lambda i: (0, 0))],
        out_specs=pl.BlockSpec((tm, n), lambda i: (i, 0)), out_shape=jax.ShapeDtypeStruct((m, n), BF16),
        compiler_params=_cp(1), name="nsa_out",
    )(o_c, o_s, o_w, gl, e, w_ob)


def _nsa(q, rows, win, gl, tab, lw, *, t_real, paged, lidx, cwin):
    b, tqa, d_b = q.shape
    t = t_real
    ncmp = 2 * N_KV
    cw = S_CMP * LANE
    p_len = 0 if paged is None else paged[1].shape[1] * paged[0].shape[2]
    l_tot = p_len + t
    assert p_len % (S_CMP * SUBLANE) == 0 and (p_len == 0 or t <= S_CMP) and (p_len > 0 or t % LANE == 0)
    pe = jnp.repeat(lw["cmp_pe"].reshape(2, 1, 2, cw), N_KV, axis=1)
    pe = jnp.broadcast_to(pe[None], (b, 2, N_KV, 2, cw)).reshape(b * ncmp, 2, cw)
    pad_rows = jnp.zeros((b * ncmp, SUBLANE - 3, cw), F32)
    if paged is None:
        ex = jnp.concatenate([jnp.zeros((b * ncmp, 1, cw), F32), pe, pad_rows], axis=1)
        x_spec = pl.BlockSpec((None, t, LANE), lambda i: (i // ncmp, 0, i % ncmp))
        kcv = _compress(rows, x_spec, t, ex, lw["w1cat"], lw["w2"])
    else:
        new_cmp = rows[:, :, :ncmp * LANE].reshape(b, t, ncmp, LANE).transpose(0, 2, 1, 3)
        ex0 = jnp.pad(new_cmp.reshape(b * ncmp, 1, t * LANE), ((0, 0), (0, 0), (0, cw - t * LANE)))
        ex = jnp.concatenate([ex0, pe, pad_rows], axis=1)
        kcv = _compress_paged(paged[0], lidx, paged[1], ex, lw["w1cat"], lw["w2"])
    if kcv.shape[1] % LANE:
        kcv = jnp.pad(kcv, ((0, 0), (0, _round_up(kcv.shape[1], LANE) - kcv.shape[1]), (0, 0)))
    tq = tqa if p_len else next(c for c in (4 * LANE, 2 * LANE, LANE) if tqa % c == 0)
    o_c, imp = _cmp_attn(q, kcv, tab, t_keys=l_tot, q0=p_len, tq=tq)
    ns = -(-l_tot // L_SEL)
    nsp = _round_up(ns, LANE)
    if p_len == 0:
        sel = jnp.swapaxes(_select(imp, jnp.arange(tqa, dtype=jnp.int32)[None], ns), 2, 3)
    else:
        nq = b * tqa
        nql = _round_up(nq, LANE)
        nsr = imp.shape[2]
        impl = jnp.pad(imp[..., :tqa].transpose(1, 2, 0, 3).reshape(1, N_KV, nsr, nq),
                       ((0, 0), (0, 0), (0, 0), (0, nql - nq)))
        qpos = (p_len + jnp.arange(nql, dtype=jnp.int32) % tqa)[None]
        sel = _select(impl, qpos, ns)[0, :, :, :nq].reshape(N_KV, nsr, b, tqa).transpose(2, 0, 3, 1)
    sel = jnp.pad(sel, ((0, 0), (0, 0), (0, 0), (0, nsp - sel.shape[3])))
    if paged is None:
        o_s = _attn_res(q, rows, 2 * N_KV, 3 * N_KV, tab, mode="sel", sel=sel, name="nsa_sel_attn")
        o_w = _attn_res(q, win, 0, N_KV, tab, mode="win", name="nsa_win_attn")
    else:
        rows_t = jnp.pad(rows, ((0, 0), (0, LANE - t), (0, 0)))
        win_t = jnp.pad(win, ((0, 0), (0, LANE - t), (0, 0)))
        tails = lambda col: pl.BlockSpec((None, LANE, LANE), lambda i, g, qt, j, col=col: (i, 0, col + g))
        o_s = _sel_paged(q, paged[2], lidx, paged[1], rows_t, t, sel, tab)
        wb = cwin.shape[1]
        kw = lambda col: pl.BlockSpec((None, wb, LANE), lambda i, g, qt, j, col=col: (i, 0, col + g))
        o_w = _flash(q, cwin, kw(0), cwin, kw(N_KV), tab, mode="win", band=False, tq=tq, tk=wb,
                     nj=1, q0=p_len, kbase=p_len - wb,
                     tail=(win_t, tails(0), win_t, tails(N_KV), t, p_len), name="nsa_win_attn_cached")
    m = b * tqa
    return _nsa_out(o_c.reshape(m, d_b), o_s.reshape(m, d_b), o_w.reshape(m, d_b), gl, lw["w_ob"])


def _layer_weights(l, w_in, mu, w0, w_up, a0, a_up, g_up, k_k, k_a, r_k, lnx_w, lnx_b, w_oa,
                   cmp_pe, cmp_w1, cmp_w2, w_ob, w_o, w_ff_up, w_ff_down, norm_g):
    d_a = w_oa.shape[1]
    d_b = w_ob.shape[1]
    d = w_o.shape[1]
    rw = 3 * d_a + DECAY_LORA + A_LORA + GATE_LORA
    pw = 3 * d_a + LORA_W
    nrow = 4 * N_KV * HEAD_B
    nwin = 2 * N_KV * HEAD_B
    ngl = 3 * (d_b // HEAD_B)
    wi = w_in[l]
    o_q = rw
    o_rows = o_q + d_b
    o_win = o_rows + nrow
    o_gl = o_win + nwin
    o_pg = o_gl + ngl
    wl = jnp.zeros((LORA_W, 3 * d_a), F32)
    wl = wl.at[:DECAY_LORA, :d_a].set(w_up[l])
    wl = wl.at[DECAY_LORA:DECAY_LORA + A_LORA, d_a:2 * d_a].set(a_up[l])
    wl = wl.at[DECAY_LORA + A_LORA:DECAY_LORA + A_LORA + GATE_LORA, 2 * d_a:].set(g_up[l])
    half = S_CMP * HEAD_B
    return dict(
        w_pa=jnp.pad(wi[:, :rw], ((0, 0), (0, pw - rw))).astype(BF16),
        w_q=wi[:, o_q:o_rows].astype(BF16),
        w_rows=wi[:, o_rows:o_win].astype(BF16),
        w_win=wi[:, o_win:o_gl].astype(BF16),
        w_gl=jnp.pad(wi[:, o_gl:o_pg], ((0, 0), (0, LANE - ngl))).astype(BF16),
        w_pg=wi[:, o_pg:o_pg + 2 * d].astype(BF16),
        mu=jnp.pad(mu[l], (0, pw - rw))[None], wl=wl.astype(BF16),
        w0=w0[l][None], a0=a0[l][None], k_k=k_k[l][None], k_a=k_a[l][None],
        r_k=r_k[l].reshape(1, d_a), lnx_w=lnx_w[l][None], lnx_b=lnx_b[l][None],
        w_oa=w_oa[l].astype(BF16), w_ob=w_ob[l].astype(BF16), w_o=w_o[l].astype(BF16),
        w_up=w_ff_up[l].astype(BF16), w_down=w_ff_down[l].astype(BF16),
        cmp_pe=cmp_pe[l],
        w1cat=jnp.concatenate([cmp_w1[l][:, :half], cmp_w1[l][:, half:]], axis=2).astype(BF16),
        w2=cmp_w2[l].astype(BF16), g=norm_g[l], rw=rw,
    )


def _layer(x, xn, lw, g_next, tab, shift0, s0, paged, lidx, cwin, rows_buf):
    b, t, d = x.shape
    m = b * t
    d_a = lw["w_oa"].shape[0]
    x2 = x.reshape(m, d)
    g = lw["g"]
    pa = _matmul(xn, lw["w_pa"], F32, name="proj_rwkv")
    q = _matmul(xn, lw["w_q"], BF16, name="proj_q")
    if rows_buf is None:
        rows = _matmul(xn, lw["w_rows"], F32, name="proj_rows")
    else:
        rows, rows_buf = _matmul_rows(xn, lw["w_rows"], rows_buf, lidx)
    win = _matmul(xn, lw["w_win"], F32, name="proj_win")
    gl = _matmul(xn, lw["w_gl"], F32, name="proj_gl")
    pg = _matmul(xn, lw["w_pg"], BF16, name="proj_pg")
    pw = pa.shape[1]
    d_b = q.shape[1]
    tp = t if t % RWKV_CHUNK == 0 else _round_up(t, SUBLANE)
    c = RWKV_CHUNK if t % RWKV_CHUNK == 0 else tp
    pa3 = pa.reshape(b, t, pw)
    pa_p = pa3 if tp == t else jnp.pad(pa3, ((0, 0), (0, tp - t), (0, 0)))
    prev = jnp.pad(shift0, ((0, 0), (0, pw - shift0.shape[1])))[:, None]
    r, lgw, k2, v, kk, a, gg = _rwkv_prep(pa_p, prev, lw["mu"], lw["wl"], lw["w0"], lw["a0"], lw["k_k"], lw["k_a"],
                                          None if tp == t else t)
    o, s_fin = _rwkv_scan(r, lgw, k2, v, kk, a, s0, c)
    flat = lambda u: u.reshape(b * tp, d_a)
    ya = _rwkv_post(flat(o), flat(r), flat(k2), flat(v), flat(gg), lw["lnx_w"], lw["lnx_b"], lw["r_k"], lw["w_oa"])
    if tp != t:
        ya = ya.reshape(b, tp, d)[:, :t].reshape(m, d)
    sh = pa3[:, t - 1, :lw["rw"]]
    tqa = t if t % LANE == 0 else _round_up(t, SUBLANE)
    q3 = q.reshape(b, t, d_b)
    gl_p = gl
    if tqa != t:
        q3 = jnp.pad(q3, ((0, 0), (0, tqa - t), (0, 0)))
        gl_p = jnp.pad(gl.reshape(b, t, LANE), ((0, 0), (0, tqa - t), (0, 0))).reshape(b * tqa, LANE)
    rows3 = rows.reshape(b, t, rows.shape[1])
    win3 = win.reshape(b, t, win.shape[1])
    yb = _nsa(q3, rows3, win3, gl_p, tab, lw, t_real=t, paged=paged, lidx=lidx, cwin=cwin)
    if tqa != t:
        yb = yb.reshape(b, tqa, d)[:, :t].reshape(m, d)
    x1, xn1 = _merge(ya, yb, pg, x2, lw["w_o"], g[1:3])
    h = _matmul(xn1, lw["w_up"], BF16, relu2=True, name="ffn_up")
    x_out, xn_out = _ffn_down(h, lw["w_down"], x1, jnp.concatenate([g[3:4], g_next], axis=0))
    wctx = win3 if cwin is None else jnp.concatenate([cwin, win3], axis=1)
    n_keep = min(WINDOW, wctx.shape[1])
    new_rows = rows3 if rows_buf is None else rows_buf
    return x_out.reshape(b, t, d), xn_out, new_rows, wctx[:, wctx.shape[1] - n_keep:], sh, s_fin


def _pair_states(s):
    b, h, n, _ = s.shape
    s = s.reshape(b, h // 2, 2, n, n)
    z = jnp.zeros((b, h // 2, n, n), s.dtype)
    top = jnp.concatenate([s[:, :, 0], z], axis=3)
    bot = jnp.concatenate([z, s[:, :, 1]], axis=3)
    return jnp.concatenate([top, bot], axis=2)


def _unpair_states(s):
    n = HEAD_A
    b, hp = s.shape[:2]
    return jnp.stack([s[:, :, :n, :n], s[:, :, n:, n:]], axis=2).reshape(b, 2 * hp, n, n)


def _trunk(x, shift0, wkv0, paged, cache_win, tab, layers):
    rows, wins, shifts, wkvs = [], [], [], []
    b, t = x.shape[:2]
    ns = 4 * N_KV
    xn = _rmsnorm_cast(x.reshape(-1, x.shape[2]), layers[0]["g"][0:1])
    rows_buf = jnp.zeros((len(layers), b * t * ns, HEAD_B), F32) if (b * t) % LANE == 0 else None
    for l, lw in enumerate(layers):
        g_next = layers[l + 1]["g"][0:1] if l + 1 < len(layers) else lw["g"][3:4]
        x, xn, nr, nw, sh, st = _layer(x, xn, lw, g_next, tab, shift0[l], _pair_states(wkv0[l]),
                                       paged, l, None if cache_win is None else cache_win[l], rows_buf)
        if rows_buf is None:
            rows.append(nr.reshape(b, t, 4, N_KV, HEAD_B))
        else:
            rows_buf = nr
        wins.append(nw.reshape(b, nw.shape[1], 2, N_KV, HEAD_B))
        shifts.append(sh)
        wkvs.append(_unpair_states(st))
    rows = jnp.stack(rows) if rows_buf is None else rows_buf.reshape(len(layers), b, t, 4, N_KV, HEAD_B)
    return x, rows, jnp.stack(wins), jnp.stack(shifts), jnp.stack(wkvs)


def kernel(x_prompt, x_sample, cache_kv, cache_win, state_shift, state_wkv, page_table, w_in, mu, w0, w_up, a0, a_up, g_up, k_k, k_a, r_k, lnx_w, lnx_b, w_oa, cmp_pe, cmp_w1, cmp_w2, w_ob, w_o, w_ff_up, w_ff_down, norm_g, rel_bias):
    depth = w_in.shape[0]
    layers = [_layer_weights(l, w_in, mu, w0, w_up, a0, a_up, g_up, k_k, k_a, r_k, lnx_w, lnx_b, w_oa,
                             cmp_pe, cmp_w1, cmp_w2, w_ob, w_o, w_ff_up, w_ff_down, norm_g) for l in range(depth)]
    tab = rel_bias[_t5_bucket_table()].T
    bp = x_prompt.shape[0]
    rw = state_shift.shape[2]
    zeros_shift = jnp.zeros((depth, bp, rw), x_prompt.dtype)
    zeros_wkv = jnp.zeros((depth, bp) + state_wkv.shape[2:], state_wkv.dtype)
    y_p, kv_p, win_p, sh_p, wkv_p = _trunk(x_prompt, zeros_shift, zeros_wkv, None, None, tab, layers)
    nl, n_pool, page = cache_kv.shape[:3]
    paged = (cache_kv.reshape(nl, n_pool, page, 2, 2 * N_KV, HEAD_B), page_table,
             cache_kv.reshape(nl, n_pool, page * 4 * N_KV, HEAD_B))
    cwin = cache_win.reshape(cache_win.shape[:3] + (-1,))
    y_s, kv_s, win_s, sh_s, wkv_s = _trunk(x_sample, state_shift, state_wkv, paged, cwin, tab, layers)
    return (y_p, y_s, kv_p, kv_s, win_p, win_s, sh_p, sh_s, wkv_p, wkv_s)
```

```python
import functools
import math

import numpy as np
import jax
import jax.numpy as jnp
from jax import lax
from jax.experimental import pallas as pl
from jax.experimental.pallas import tpu as pltpu

F32 = jnp.float32
BF16 = jnp.bfloat16

HEAD_A = 64
DECAY_LORA = 64
A_LORA = 64
GATE_LORA = 160
LNX_EPS = 64e-5
N_KV = 2
HEAD_B = 128
L_CMP = 32
S_CMP = 16
L_SEL = 64
TOP_N = 16
WINDOW = 512
FORCE_BONUS = 1e4
NEG = -1e30
NUM_BUCKETS = 32
MAX_DIST = 128
EPS = 1e-6

LANE = 128
SUBLANE = 8
VMEM_LIMIT = 56 * 1024 * 1024
RWKV_CHUNK = 64
LORA_W = 384
FAR = LANE - 1


def _cp(n_axes):
    return pltpu.CompilerParams(dimension_semantics=("arbitrary",) * n_axes,
                                vmem_limit_bytes=VMEM_LIMIT)


def _round_up(x, m):
    return -(-x // m) * m


def _tile(n, pref, mult=SUBLANE):
    if n <= pref:
        return n
    for t in range(pref - pref % mult, 0, -mult):
        if n % t == 0:
            return t
    return n


def _dot(a, b):
    return jnp.dot(a, b, preferred_element_type=F32)


def _dot_nt(a, b):
    return lax.dot_general(a, b, (((1,), (1,)), ((), ())), preferred_element_type=F32)


def _dot_tn(a, b):
    return lax.dot_general(a, b, (((0,), (0,)), ((), ())), preferred_element_type=F32)


def _split(x):
    hi = x.astype(BF16)
    lo = (x - hi.astype(F32)).astype(BF16)
    return hi, lo


def _dot_x(a, b):
    hi, lo = _split(a)
    return _dot(hi, b) + _dot(lo, b)


def _dot3(a, b):
    ah, al = _split(a)
    bh, bl = _split(b)
    return _dot(ah, bh) + _dot(ah, bl) + _dot(al, bh)


def _sigmoid(x):
    return 1.0 / (1.0 + jnp.exp(-x))


def _rms(y, g):
    return y * lax.rsqrt(jnp.mean(y * y, axis=-1, keepdims=True) + EPS) * g


def _rmsnorm_kernel(x_ref, g_ref, o_ref):
    o_ref[...] = _rms(x_ref[...], g_ref[...]).astype(o_ref.dtype)


def _rmsnorm_cast(x, g):
    m, d = x.shape
    tm = _tile(m, 256)
    return pl.pallas_call(
        _rmsnorm_kernel, grid=(m // tm,),
        in_specs=[pl.BlockSpec((tm, d), lambda i: (i, 0)), pl.BlockSpec((1, d), lambda i: (0, 0))],
        out_specs=pl.BlockSpec((tm, d), lambda i: (i, 0)),
        out_shape=jax.ShapeDtypeStruct((m, d), BF16), compiler_params=_cp(1), name="rmsnorm_cast",
    )(x, g)


def _mm_kernel(a_ref, w_ref, o_ref, *, relu2):
    y = _dot(a_ref[...], w_ref[...])
    if relu2:
        y = jnp.square(jnp.maximum(y, 0.0))
    o_ref[...] = y.astype(o_ref.dtype)


def _matmul(a, w, out_dtype, relu2=False, name="matmul"):
    m, k = a.shape
    n = w.shape[1]
    tm = _tile(m, 1024)
    tn = _tile(n, 1280, LANE)
    return pl.pallas_call(
        functools.partial(_mm_kernel, relu2=relu2), grid=(m // tm, n // tn),
        in_specs=[pl.BlockSpec((tm, k), lambda i, j: (i, 0)), pl.BlockSpec((k, tn), lambda i, j: (0, j))],
        out_specs=pl.BlockSpec((tm, tn), lambda i, j: (i, j)),
        out_shape=jax.ShapeDtypeStruct((m, n), out_dtype), compiler_params=_cp(2), name=name,
    )(a, w)


def _mm_rows_kernel(a_ref, w_ref, buf_ref, o_ref, il_ref):
    del buf_ref
    y = _dot(a_ref[...], w_ref[...])
    o_ref[...] = y
    ns = y.shape[1] // LANE
    for s in range(ns):
        il_ref[pl.ds(s, y.shape[0], stride=ns), :] = y[:, s * LANE:(s + 1) * LANE]


def _matmul_rows(a, w, buf, lidx):
    m, k = a.shape
    n = w.shape[1]
    ns = n // LANE
    tm = _tile(m, 1024)
    return pl.pallas_call(
        _mm_rows_kernel, grid=(m // tm,),
        in_specs=[pl.BlockSpec((tm, k), lambda i: (i, 0)), pl.BlockSpec((k, n), lambda i: (0, 0)),
                  pl.BlockSpec(memory_space=pl.ANY)],
        out_specs=[pl.BlockSpec((tm, n), lambda i: (i, 0)), pl.BlockSpec((None, tm * ns, LANE), lambda i: (lidx, i, 0))],
        out_shape=[jax.ShapeDtypeStruct((m, n), F32), jax.ShapeDtypeStruct(buf.shape, buf.dtype)],
        input_output_aliases={2: 1}, compiler_params=_cp(1), name="proj_rows",
    )(a, w, buf)


def _ffn_down_kernel(h_ref, w_ref, x_ref, g_ref, o_ref, n_ref, acc_ref):
    k = pl.program_id(1)

    @pl.when(k == 0)
    def _():
        acc_ref[...] = jnp.zeros_like(acc_ref)

    acc_ref[...] += _dot(h_ref[...], w_ref[...])

    @pl.when(k == pl.num_programs(1) - 1)
    def _():
        o = x_ref[...] + _rms(acc_ref[...], g_ref[0:1])
        o_ref[...] = o
        n_ref[...] = _rms(o, g_ref[1:2]).astype(n_ref.dtype)


def _ffn_down(h, w, x, g):
    m, kdim = h.shape
    n = w.shape[1]
    tm = _tile(m, 512)
    tk = _tile(kdim, 2048, LANE)
    row = pl.BlockSpec((tm, n), lambda i, k: (i, 0))
    return pl.pallas_call(
        _ffn_down_kernel, grid=(m // tm, kdim // tk),
        in_specs=[pl.BlockSpec((tm, tk), lambda i, k: (i, k)), pl.BlockSpec((tk, n), lambda i, k: (k, 0)),
                  row, pl.BlockSpec((2, n), lambda i, k: (0, 0))],
        out_specs=[row, row],
        out_shape=[jax.ShapeDtypeStruct((m, n), F32), jax.ShapeDtypeStruct((m, n), BF16)],
        scratch_shapes=[pltpu.VMEM((tm, n), F32)], compiler_params=_cp(2), name="ffn_down",
    )(h, w, x, g)


def _merge_kernel(ya_ref, yb_ref, pg_ref, x_ref, w_ref, g_ref, o_ref, n_ref):
    d = ya_ref.shape[1]
    pg = pg_ref[...].astype(F32)
    mix = _sigmoid(pg[:, :d]) * ya_ref[...].astype(F32) + _sigmoid(pg[:, d:]) * yb_ref[...].astype(F32)
    y = _dot(mix.astype(BF16), w_ref[...])
    o = x_ref[...] + _rms(y, g_ref[0:1])
    o_ref[...] = o
    n_ref[...] = _rms(o, g_ref[1:2]).astype(n_ref.dtype)


def _merge(ya, yb, pg, x, w_o, g):
    m, d = x.shape
    tm = _tile(m, 256)
    row = lambda c: pl.BlockSpec((tm, c), lambda i: (i, 0))
    return pl.pallas_call(
        _merge_kernel, grid=(m // tm,),
        in_specs=[row(d), row(d), row(2 * d), row(d),
                  pl.BlockSpec((d, d), lambda i: (0, 0)), pl.BlockSpec((2, d), lambda i: (0, 0))],
        out_specs=[row(d), row(d)],
        out_shape=[jax.ShapeDtypeStruct((m, d), F32), jax.ShapeDtypeStruct((m, d), BF16)],
        compiler_params=_cp(1), name="merge",
    )(ya, yb, pg, x, w_o, g)


def _head_indicator(d_a):
    h = np.zeros((d_a, LANE), np.float32)
    h[np.arange(d_a), np.arange(d_a) // HEAD_A] = 1.0
    return jnp.asarray(h, BF16), jnp.asarray(h.T, BF16)


def _rwkv_prep_kernel(pa_ref, prev_ref, mu_ref, wl_ref, w0_ref, a0_ref, kk_ref, ka_ref, hd_ref, hdt_ref,
                      r_ref, lw_ref, k2_ref, v_ref, kkn_ref, a_ref, g_ref, carry_ref, *, d_a, t_real):
    t = pl.program_id(1)
    tt = pa_ref.shape[1]

    @pl.when(t == 0)
    def _():
        carry_ref[...] = prev_ref[0]

    x = pa_ref[0]
    row = lax.broadcasted_iota(jnp.int32, x.shape, 0)
    xprev = jnp.where(row == 0, carry_ref[...], pltpu.roll(x, 1, 0))
    carry_ref[...] = pa_ref[0, pl.ds(tt - 1, 1), :]
    xs = x + (xprev - x) * mu_ref[...]
    r = xs[:, :d_a]
    k = xs[:, d_a:2 * d_a]
    v = xs[:, 2 * d_a:3 * d_a]
    lo = xs[:, 3 * d_a:]
    lane = lax.broadcasted_iota(jnp.int32, lo.shape, 1)
    act = jnp.where(lane < DECAY_LORA, jnp.tanh(lo),
                    jnp.where(lane < DECAY_LORA + A_LORA, lo, _sigmoid(lo)))
    lin = _dot(act.astype(BF16), wl_ref[...])
    z = -(w0_ref[...] + lin[:, :d_a])
    w = -(jnp.maximum(z, 0.0) + jnp.log(1.0 + jnp.exp(-jnp.abs(z)))) - 0.5
    logw = -jnp.exp(w)
    a = _sigmoid(a0_ref[...] + lin[:, d_a:2 * d_a])
    g = lin[:, 2 * d_a:]
    kkr = k * kk_ref[...]
    ss = _dot_x(kkr * kkr, hd_ref[...])
    inv = 1.0 / jnp.maximum(jnp.sqrt(ss), 1e-12)
    kkn = kkr * _dot_x(inv, hdt_ref[...])
    k2 = k * (1.0 + (a - 1.0) * ka_ref[...])
    if t_real is not None:
        live = (t * tt + lax.broadcasted_iota(jnp.int32, r.shape, 0)) < t_real
        zero = jnp.zeros_like(r)
        r, logw, k2, v, kkn, a = (jnp.where(live, u, zero) for u in (r, logw, k2, v, kkn, a))
    r_ref[0] = r.astype(r_ref.dtype)
    lw_ref[0] = logw
    k2_ref[0] = k2.astype(k2_ref.dtype)
    v_ref[0] = v.astype(v_ref.dtype)
    kkn_ref[0] = kkn.astype(kkn_ref.dtype)
    a_ref[0] = a.astype(a_ref.dtype)
    g_ref[0] = g.astype(g_ref.dtype)


def _rwkv_prep(pa, prev, mu, wl, w0, a0, k_k, k_a, t_real):
    b, t, p = pa.shape
    d_a = w0.shape[1]
    tt = _tile(t, 256)
    hd, hdt = _head_indicator(d_a)
    full = lambda a: pl.BlockSpec(a.shape, lambda i, j: (0,) * a.ndim)
    out = [jax.ShapeDtypeStruct((b, t, d_a), F32 if i == 1 else BF16) for i in range(7)]
    ospec = pl.BlockSpec((1, tt, d_a), lambda i, j: (i, j, 0))
    return pl.pallas_call(
        functools.partial(_rwkv_prep_kernel, d_a=d_a, t_real=t_real), grid=(b, t // tt),
        in_specs=[pl.BlockSpec((1, tt, p), lambda i, j: (i, j, 0)), pl.BlockSpec((1, 1, p), lambda i, j: (i, 0, 0)),
                  full(mu), full(wl), full(w0), full(a0), full(k_k), full(k_a), full(hd), full(hdt)],
        out_specs=[ospec] * 7, out_shape=out,
        scratch_shapes=[pltpu.VMEM((1, p), F32)], compiler_params=_cp(2), name="rwkv_prep",
    )(pa, prev, mu, wl, w0, a0, k_k, k_a, hd, hdt)


def _mmb(a, b, form="nn"):
    dot = {"nn": _dot, "nt": _dot_nt, "tn": _dot_tn}[form]
    return dot(a.astype(BF16), b.astype(BF16))


def _rwkv_scan_kernel(r_ref, lw_ref, k2_ref, v_ref, kk_ref, a_ref, s0_ref, o_ref, sT_ref, s_ref, *, c):
    @pl.when(pl.program_id(1) == 0)
    def _():
        s_ref[...] = s0_ref[0]

    hk = HEAD_A
    pairs = range(s_ref.shape[0])
    sls = [slice(p * LANE, (p + 1) * LANE) for p in pairs]
    lane = lax.broadcasted_iota(jnp.int32, (c, LANE), 1)
    m0 = lane < hk
    rowi = lax.broadcasted_iota(jnp.int32, (c, c), 0)
    coli = lax.broadcasted_iota(jnp.int32, (c, c), 1)
    ltri = (coli <= rowi).astype(BF16)
    lw = [lw_ref[0, :, sl] for sl in sls]
    cum = [_dot_x_rhs(ltri, x) for x in lw]
    e_wi = [jnp.exp(-x) for x in cum]
    kk = [kk_ref[0, :, sl].astype(F32) for sl in sls]
    at = [-k * jnp.exp(x - y) for k, x, y in zip(kk, cum, lw)]
    bt = [k * a_ref[0, :, sl].astype(F32) * e for k, sl, e in zip(kk, sls, e_wi)]
    kt = [k2_ref[0, :, sl].astype(F32) * e for sl, e in zip(sls, e_wi)]
    rt = [r_ref[0, :, sl].astype(F32) * jnp.exp(x) for sl, x in zip(sls, cum)]
    v = [v_ref[0, :, sl] for sl in sls]
    w_c = [jnp.exp(x[c - 1:c, :]) for x in cum]

    def bd(z):
        zero = jnp.zeros_like(z)
        return jnp.concatenate([jnp.where(m0, z, zero), jnp.where(m0, zero, z)], axis=0)

    prow = lax.broadcasted_iota(jnp.int32, (c, 2 * c), 0)
    pcol = lax.broadcasted_iota(jnp.int32, (c, 2 * c), 1)
    pcol = jnp.where(pcol >= c, pcol - c, pcol)
    strict = pcol < prow
    incl = pcol <= prow
    zero_cc = jnp.zeros((c, 2 * c), F32)
    x2 = [jnp.concatenate([x, y], axis=0) for x, y in zip(at, rt)]
    xb = [_mmb(x, bd(y), "nt") for x, y in zip(x2, bt)]
    xk = [_mmb(x, bd(y), "nt") for x, y in zip(x2, kt)]
    a_ab = [jnp.where(strict, x[:c], zero_cc) for x in xb]
    a_rb = [jnp.where(incl, x[c:], zero_cc) for x in xb]
    a_ak = [jnp.where(strict, x[:c], zero_cc) for x in xk]
    a_rk = [jnp.where(incl, x[c:], zero_cc) for x in xk]

    if 2 * c == LANE:
        bdc = bd
    else:
        mc = lax.broadcasted_iota(jnp.int32, (c, 2 * c), 1) < c

        def bdc(z):
            zero = jnp.zeros_like(z)
            return jnp.concatenate([jnp.where(mc, z, zero), jnp.where(mc, zero, z)], axis=0)

    eye = (pcol == prow).astype(F32)
    tmat = [eye + x for x in a_ab]
    npow = a_ab
    steps = 1
    while 2 * steps < c:
        npow = [_dot3(x, bdc(x)) for x in npow]
        tmat = [x + _dot3(x, bdc(y)) for x, y in zip(tmat, npow)]
        steps *= 2
    ta = [_mmb(x, bd(y)) for x, y in zip(tmat, at)]
    xv = [_mmb(x, bd(y)) for x, y in zip(a_ak, v)]
    tx = [_mmb(x, bd(y)) for x, y in zip(tmat, xv)]
    p_c = [x + _mmb(y, bd(z)) for x, y, z in zip(rt, a_rb, ta)]
    q_c = [_mmb(x, bd(y)) + _mmb(z, bd(u)) for x, y, z, u in zip(a_rb, tx, a_rk, v)]
    lr = lax.broadcasted_iota(jnp.int32, (LANE, LANE), 0)
    lc = lax.broadcasted_iota(jnp.int32, (LANE, LANE), 1)
    same = (lr < hk) == (lc < hk)
    eye_l = (lr == lc).astype(F32)
    zero_l = jnp.zeros((LANE, LANE), F32)
    m_c = [(eye_l + jnp.where(same, _mmb(x, y, "tn"), zero_l)) * w for x, y, w in zip(ta, bt, w_c)]
    n_c = [jnp.where(same, _mmb(x, y, "tn") + _mmb(z, u, "tn"), zero_l) * w
           for x, y, z, u, w in zip(tx, bt, v, kt, w_c)]
    s = [s_ref[p] for p in pairs]
    o = [_mmb(x, y, "nt") + z for x, y, z in zip(p_c, s, q_c)]
    s_new = [_mmb(x, y) + z for x, y, z in zip(s, m_c, n_c)]
    for p in pairs:
        o_ref[0, :, sls[p]] = o[p]
        s_ref[p] = s_new[p]
        sT_ref[0, p] = s_new[p]


def _dot_x_rhs(a, b):
    hi, lo = _split(b)
    return _dot(a, hi) + _dot(a, lo)


def _rwkv_scan(r, lw, k2, v, kk, a, s0, c):
    b, t, d_a = r.shape
    npair = d_a // LANE
    seq = pl.BlockSpec((1, c, d_a), lambda i, j: (i, j, 0))
    st = pl.BlockSpec((1, npair, LANE, LANE), lambda i, j: (i, 0, 0, 0))
    return pl.pallas_call(
        functools.partial(_rwkv_scan_kernel, c=c), grid=(b, t // c),
        in_specs=[seq] * 6 + [st], out_specs=[seq, st],
        out_shape=[jax.ShapeDtypeStruct((b, t, d_a), F32), jax.ShapeDtypeStruct((b, npair, LANE, LANE), F32)],
        scratch_shapes=[pltpu.VMEM((npair, LANE, LANE), F32)], compiler_params=_cp(2), name="rwkv_scan",
    )(r, lw, k2, v, kk, a, s0)


def _rwkv_post_kernel(o_ref, r_ref, k2_ref, v_ref, g_ref, lw_ref, lb_ref, rk_ref, hd_ref, hdt_ref, w_ref, y_ref):
    o = o_ref[...]
    hd = hd_ref[...]
    hdt = hdt_ref[...]
    inv_n = 1.0 / HEAD_A
    mean = _dot_x(_dot_x(o, hd) * inv_n, hdt)
    d = o - mean
    var = _dot_x(d * d, hd) * inv_n
    xo = d * _dot_x(lax.rsqrt(var + LNX_EPS), hdt) * lw_ref[...] + lb_ref[...]
    rk2 = r_ref[...].astype(F32) * k2_ref[...].astype(F32) * rk_ref[...]
    bonus = _dot_x(_dot_x(rk2, hd), hdt) * v_ref[...].astype(F32)
    y_ref[...] = _dot(((xo + bonus) * g_ref[...].astype(F32)).astype(BF16), w_ref[...]).astype(y_ref.dtype)


def _rwkv_post(o, r, k2, v, g, lnx_w, lnx_b, r_k, w_oa):
    m, d_a = o.shape
    n = w_oa.shape[1]
    tm = _tile(m, 256)
    hd, hdt = _head_indicator(d_a)
    row = pl.BlockSpec((tm, d_a), lambda i: (i, 0))
    full = lambda a: pl.BlockSpec(a.shape, lambda i: (0,) * a.ndim)
    return pl.pallas_call(
        _rwkv_post_kernel, grid=(m // tm,),
        in_specs=[row] * 5 + [full(lnx_w), full(lnx_b), full(r_k), full(hd), full(hdt), full(w_oa)],
        out_specs=pl.BlockSpec((tm, n), lambda i: (i, 0)), out_shape=jax.ShapeDtypeStruct((m, n), BF16),
        compiler_params=_cp(1), name="rwkv_post",
    )(o, r, k2, v, g, lnx_w, lnx_b, r_k, hd, hdt, w_oa)


def _t5_bucket_table():
    d = np.arange(LANE)
    max_exact = NUM_BUCKETS // 2
    df = np.maximum(d, 1).astype(np.float32)
    large = max_exact + (np.log(df / np.float32(max_exact)) / np.float32(math.log(MAX_DIST / max_exact))
                         * np.float32(NUM_BUCKETS - max_exact)).astype(np.int32)
    large = np.minimum(large, NUM_BUCKETS - 1)
    tab = np.where(d < max_exact, d, large)
    assert tab[FAR] == NUM_BUCKETS - 1
    return tab


PAGES_PER_STEP = 8


def _page_specs(cache, lidx, npp, half=None):
    if half is None:
        blk = (None, None) + cache.shape[2:]
        return [pl.BlockSpec(blk, lambda i, p, pt, k=k: (lidx, pt[i, p * npp + k], 0, 0)) for k in range(npp)]
    blk = (None, None, cache.shape[2], None) + cache.shape[4:]
    return [pl.BlockSpec(blk, lambda i, p, pt, k=k: (lidx, pt[i, p * npp + k], 0, half, 0, 0)) for k in range(npp)]


def _gelu_tanh(x):
    return 0.5 * x * (1.0 + jnp.tanh(math.sqrt(2.0 / math.pi) * (x + 0.044715 * (x * x * x))))


def _compress_finish(y, ex_ref, w1, w2_ref, o_ref):
    rr = y.shape[0]
    e = _dot(ex_ref[0].astype(BF16), w1)
    nxt = pltpu.roll(y[:, LANE:], rr - 1, 0)
    row = lax.broadcasted_iota(jnp.int32, nxt.shape, 0)
    nxt = jnp.where(row == rr - 1, e[0:1, LANE:], nxt)
    pre = y[:, :LANE] + nxt + (e[1:2, :LANE] + e[2:3, LANE:])
    o_ref[0] = _dot(_gelu_tanh(pre).astype(BF16), w2_ref[0])


def _compress_kernel(x_ref, ex_ref, w1_ref, w2_ref, o_ref):
    rr = x_ref.shape[0] // S_CMP
    w1 = w1_ref[0]
    y = jnp.zeros((rr, 2 * LANE), F32)
    for j in range(S_CMP):
        xj = x_ref[pl.ds(j, rr, stride=S_CMP), :].astype(BF16)
        y = y + _dot(xj, w1[j * LANE:(j + 1) * LANE, :])
    _compress_finish(y, ex_ref, w1, w2_ref, o_ref)


def _compress_finish_kernel(y_ref, ex_ref, w1_ref, w2_ref, o_ref):
    _compress_finish(y_ref[0], ex_ref, w1_ref[0], w2_ref, o_ref)


def _compress_paged_kernel(pt_ref, *refs):
    x_refs, (w1_ref, y_ref) = refs[:-2], refs[-2:]
    page, ncmp, _ = x_refs[0].shape
    cpp = page // S_CMP
    for s in range(ncmp):
        y = jnp.zeros((len(x_refs) * cpp, 2 * LANE), F32)
        for j in range(S_CMP):
            xj = jnp.concatenate([x[pl.ds(j, cpp, stride=S_CMP), s, :] for x in x_refs], axis=0)
            y = y + _dot(xj.astype(BF16), w1_ref[s // N_KV, j * LANE:(j + 1) * LANE, :])
        y_ref[0, s] = y


def _compress(x, x_spec, n_pos, ex, w1cat, w2):
    nb, _, cw = ex.shape
    rr = n_pos // S_CMP
    c_of = lambda i: (i // N_KV) % 2
    return pl.pallas_call(
        _compress_kernel, grid=(nb,),
        in_specs=[x_spec, pl.BlockSpec((1, SUBLANE, cw), lambda i: (i, 0, 0)),
                  pl.BlockSpec((1, cw, 2 * LANE), lambda i: (c_of(i), 0, 0)),
                  pl.BlockSpec((1, LANE, LANE), lambda i: (c_of(i), 0, 0))],
        out_specs=pl.BlockSpec((1, rr, LANE), lambda i: (i, 0, 0)),
        out_shape=jax.ShapeDtypeStruct((nb, rr, LANE), F32), compiler_params=_cp(1), name="nsa_compress",
    )(x, ex, w1cat, w2)


def _compress_paged(cache, lidx, page_table, ex, w1cat, w2):
    b, n_pages = page_table.shape
    ncmp = cache.shape[4]
    cw = ex.shape[2]
    page = cache.shape[2]
    npp = next(k for k in (PAGES_PER_STEP, 4, 2, 1) if n_pages % k == 0)
    rr = n_pages * page // S_CMP
    rt = npp * page // S_CMP
    grid_spec = pltpu.PrefetchScalarGridSpec(
        num_scalar_prefetch=1, grid=(b, n_pages // npp),
        in_specs=_page_specs(cache, lidx, npp, 0) + [pl.BlockSpec(w1cat.shape, lambda i, p, pt: (0, 0, 0))],
        out_specs=pl.BlockSpec((1, ncmp, rt, 2 * LANE), lambda i, p, pt: (i, 0, p, 0)))
    y = pl.pallas_call(
        _compress_paged_kernel, grid_spec=grid_spec,
        out_shape=jax.ShapeDtypeStruct((b, ncmp, rr, 2 * LANE), F32),
        compiler_params=_cp(2), name="nsa_compress_paged",
    )(page_table, *([cache] * npp), w1cat)
    c_of = lambda i: (i // N_KV) % 2
    return pl.pallas_call(
        _compress_finish_kernel, grid=(b * ncmp,),
        in_specs=[pl.BlockSpec((1, rr, 2 * LANE), lambda i: (i, 0, 0)), pl.BlockSpec((1, SUBLANE, cw), lambda i: (i, 0, 0)),
                  pl.BlockSpec((1, cw, 2 * LANE), lambda i: (c_of(i), 0, 0)),
                  pl.BlockSpec((1, LANE, LANE), lambda i: (c_of(i), 0, 0))],
        out_specs=pl.BlockSpec((1, rr, LANE), lambda i: (i, 0, 0)),
        out_shape=jax.ShapeDtypeStruct((b * ncmp, rr, LANE), F32), compiler_params=_cp(1), name="nsa_compress_finish",
    )(y.reshape(b * ncmp, rr, 2 * LANE), ex, w1cat, w2)


def _bias_gather(tab_row, dist):
    idx = jnp.clip(dist, 0, FAR)
    return jnp.take_along_axis(jnp.broadcast_to(tab_row, idx.shape), idx, axis=1)


def _stack_heads(q_ref, col0=0, hpg=None):
    hpg = q_ref.shape[1] // HEAD_B if hpg is None else hpg
    return jnp.concatenate([q_ref[:, col0 + h * HEAD_B:col0 + (h + 1) * HEAD_B] for h in range(hpg)], axis=0)


def _rel_bias(dist, g, hpg, tab_ref):
    rel = []
    for h in range(hpg):
        tab_row = tab_ref[pl.ds(g * hpg + h, 1), :]
        far = tab_row[:, FAR:FAR + 1]
        rel.append(jnp.concatenate([_bias_gather(tab_row, dist[:, c * LANE:(c + 1) * LANE]) - far
                                    for c in range(dist.shape[1] // LANE)], axis=1))
    return jnp.stack(rel)


def _online_softmax_update(qs, tiles):
    hpg, tq, _ = tiles[0][4].shape
    scores = [_dot_nt(qs, kb) for kb, *_ in tiles]
    probs, alphas = [], []
    for s, (kb, vb, mask, rel, m_ref, l_ref, acc_ref) in zip(scores, tiles):
        width = kb.shape[0]
        s = s.reshape(hpg, tq, width) * (HEAD_B ** -0.5)
        if rel is None:
            s = s + jnp.where(mask, 0.0, 2 * NEG)[None]
        else:
            s = s + jnp.where(mask[None], rel, 2 * NEG)
        m_prev = m_ref[...][:, :, :1]
        m_new = jnp.maximum(m_prev, jnp.max(s, axis=2, keepdims=True))
        alpha = jnp.exp(m_prev - m_new)
        p = jnp.exp(s - m_new)
        m_ref[...] = jnp.broadcast_to(m_new, m_ref.shape)
        probs.append(p.reshape(hpg * tq, width).astype(BF16))
        alphas.append(alpha)
    pvs = [_dot(p, jnp.concatenate([t[1], jnp.ones_like(t[1])], axis=1)) for p, t in zip(probs, tiles)]
    for pv, alpha, t in zip(pvs, alphas, tiles):
        pv = pv.reshape(hpg, tq, 2 * HEAD_B)
        t[5][...] = alpha * t[5][...] + pv[:, :, HEAD_B:]
        t[6][...] = alpha * t[6][...] + pv[:, :, :HEAD_B]


def _cmp_attn_kernel(q_ref, kc_ref, vc_ref, ovt_ref, tab_ref, o_ref, imp_ref, *, tq, q0, lanes):
    g = pl.program_id(1)
    qt = pl.program_id(2)
    hpg = q_ref.shape[2] // HEAD_B
    rr = kc_ref.shape[1]
    kc = kc_ref[0].astype(BF16)
    vc = vc_ref[0].astype(BF16)
    qbase = q0 + qt * tq
    qrow = qbase + lax.broadcasted_iota(jnp.int32, (tq, LANE), 0)
    dists = []
    for cix in range(rr // LANE):
        c_end = S_CMP * (cix * LANE + lax.broadcasted_iota(jnp.int32, (tq, LANE), 1)) + (L_CMP - 1)
        dists.append(qrow - c_end)
    mask = (jnp.concatenate(dists, axis=1) >= 0)[None]
    bias = jnp.stack([jnp.concatenate([_bias_gather(tab_ref[pl.ds(g * hpg + h, 1), :], d) for d in dists], axis=1)
                      for h in range(hpg)])
    s = _dot_nt(_stack_heads(q_ref.at[0]), kc).reshape(hpg, tq, rr) * (HEAD_B ** -0.5) + bias
    s = jnp.where(mask, s, NEG)
    e = jnp.where(mask, jnp.exp(s - jnp.max(s, axis=2, keepdims=True)), 0.0)
    p = e / jnp.maximum(jnp.sum(e, axis=2, keepdims=True), 1e-30)
    o = _dot(p.reshape(hpg * tq, rr).astype(BF16), vc).reshape(hpg, tq, HEAD_B)
    o_ref[0] = jnp.concatenate([o[h] for h in range(hpg)], axis=1)
    psum = jnp.sum(p, axis=0)
    if tq < lanes:
        psum = jnp.concatenate([psum, jnp.zeros((lanes - tq, rr), F32)], axis=0)
    ph, plo = _split(psum)
    ovt = ovt_ref[...]
    imp_ref[0, 0] = _dot_nt(ovt, ph) + _dot_nt(ovt, plo)


def _select_kernel(imp_ref, qpos_ref, selt_ref, *, n_top):
    nsr, lanes = imp_ref.shape[2:]
    j = lax.broadcasted_iota(jnp.int32, (nsr, lanes), 0)
    cur = qpos_ref[...] // L_SEL
    valid = j <= cur
    forced = (j == 0) | (j == cur) | (j == cur - 1)
    score = jnp.where(valid, imp_ref[0, 0] + jnp.where(forced, FORCE_BONUS, 0.0), NEG)

    def body(_, carry):
        score, sel = carry
        top = jnp.max(score, axis=0, keepdims=True)
        first = jnp.min(jnp.where(score == top, j, nsr), axis=0, keepdims=True)
        pick = j == first
        sel = jnp.where(pick & (top > NEG / 2), 1.0, sel)
        return jnp.where(pick, 2 * NEG, score), sel

    _, sel = lax.fori_loop(0, n_top, body, (score, jnp.zeros((nsr, lanes), F32)))
    selt_ref[0, 0] = sel


def _select(imp, qpos, ns):
    x, gg, nsr, l = imp.shape
    lt = next(c for c in (256, 128) if l % c == 0)
    blk = pl.BlockSpec((1, 1, nsr, lt), lambda i, g, t: (i, g, 0, t))
    return pl.pallas_call(
        functools.partial(_select_kernel, n_top=min(TOP_N, ns)), grid=(x, gg, l // lt),
        in_specs=[blk, pl.BlockSpec((1, lt), lambda i, g, t: (0, t))], out_specs=blk,
        out_shape=jax.ShapeDtypeStruct(imp.shape, F32), compiler_params=_cp(3), name="nsa_select",
    )(imp, qpos)


def _cmp_attn(q, kcv, tab, *, t_keys, q0, tq):
    b, tqa, d_b = q.shape
    rr = kcv.shape[1]
    gw = d_b // N_KV
    ns = -(-t_keys // L_SEL)
    nsr = _round_up(ns, SUBLANE)
    lanes = max(tq, LANE)
    nqt = tqa // tq
    ci = np.arange(rr)
    sj = np.arange(nsr)
    ov = ((S_CMP * ci) // L_SEL)[None, :] == sj[:, None]
    ov |= ((S_CMP * ci + L_CMP - 1) // L_SEL)[None, :] == sj[:, None]
    ovt = jnp.asarray(ov.astype(np.float32), BF16)
    kern = functools.partial(_cmp_attn_kernel, tq=tq, q0=q0, lanes=lanes)
    return pl.pallas_call(
        kern, grid=(b, N_KV, nqt),
        in_specs=[pl.BlockSpec((1, tq, gw), lambda i, g, t: (i, t, g)),
                  pl.BlockSpec((1, rr, LANE), lambda i, g, t: (i * 4 + g, 0, 0)),
                  pl.BlockSpec((1, rr, LANE), lambda i, g, t: (i * 4 + N_KV + g, 0, 0)),
                  pl.BlockSpec(ovt.shape, lambda i, g, t: (0, 0)),
                  pl.BlockSpec(tab.shape, lambda i, g, t: (0, 0))],
        out_specs=[pl.BlockSpec((1, tq, gw), lambda i, g, t: (i, t, g)),
                   pl.BlockSpec((1, 1, nsr, lanes), lambda i, g, t: (i, g, 0, t))],
        out_shape=[jax.ShapeDtypeStruct((b, tqa, d_b), F32),
                   jax.ShapeDtypeStruct((b, N_KV, nsr, nqt * lanes), F32)],
        compiler_params=_cp(3), name="nsa_cmp_attn",
    )(q, kcv, kcv, ovt, tab)


def _win_cached_kernel(q_ref, k_ref, v_ref, kt_ref, vt_ref, tab_ref, o_ref, m_ref, l_ref, acc_ref,
                       *, q0, kbase, n_tail):
    g = pl.program_id(1)
    tq = q_ref.shape[0]
    hpg = q_ref.shape[1] // HEAD_B
    qs = _stack_heads(q_ref)
    m_ref[...] = jnp.full_like(m_ref, NEG)
    l_ref[...] = jnp.zeros_like(l_ref)
    acc_ref[...] = jnp.zeros_like(acc_ref)

    def fold(k_r, v_r, kpos0, n_valid):
        w = k_r.shape[0]
        col = lax.broadcasted_iota(jnp.int32, (tq, w), 1)
        dist = q0 + lax.broadcasted_iota(jnp.int32, (tq, w), 0) - (kpos0 + col)
        mask = (dist >= 0) & (dist < WINDOW)
        if n_valid is not None:
            mask &= col < n_valid
        _online_softmax_update(qs, [(k_r[...].astype(BF16), v_r[...].astype(BF16), mask,
                                     _rel_bias(dist, g, hpg, tab_ref), m_ref, l_ref, acc_ref)])

    fold(k_ref, v_ref, kbase, None)
    fold(kt_ref, vt_ref, q0, n_tail)
    out = acc_ref[...] / jnp.maximum(l_ref[...][:, :, :1], 1e-30)
    o_ref[...] = jnp.concatenate([out[h] for h in range(hpg)], axis=1)


def _sel_paged_kernel(pt_ref, *refs, ns, n_tail, p_len):
    npp = len(refs) - 8
    q_ref, x_refs = refs[0], refs[1:1 + npp]
    tail_ref, sel_ref, tab_ref, o_ref, m_ref, l_ref, acc_ref = refs[1 + npp:]
    j = pl.program_id(1)
    nj = pl.num_programs(1)
    tq = q_ref.shape[0]
    gw = q_ref.shape[1] // N_KV
    hpg = gw // HEAD_B
    page = x_refs[0].shape[0] // ns
    width = npp * page

    @pl.when(j == 0)
    def _():
        m_ref[...] = jnp.full_like(m_ref, NEG)
        l_ref[...] = jnp.zeros_like(l_ref)
        acc_ref[...] = jnp.zeros_like(acc_ref)

    def fold(g, kb, vb, kpos0, n_valid, near):
        w = kb.shape[0]
        col = lax.broadcasted_iota(jnp.int32, (tq, w), 1)
        dist = p_len + lax.broadcasted_iota(jnp.int32, (tq, w), 0) - (kpos0 + col)
        mask = dist >= 0
        if n_valid is not None:
            mask &= col < n_valid
        nsp = sel_ref.shape[2]
        blk = (kpos0 + lax.broadcasted_iota(jnp.int32, (nsp, w), 1)) // L_SEL
        expand = (blk == lax.broadcasted_iota(jnp.int32, (nsp, w), 0)).astype(BF16)
        mask &= _dot(sel_ref[g].astype(BF16), expand) > 0.5
        rel = _rel_bias(dist, g, hpg, tab_ref) if near else None
        _online_softmax_update(_stack_heads(q_ref, g * gw, hpg),
                               [(kb, vb, mask, rel, m_ref.at[g], l_ref.at[g], acc_ref.at[g])])

    def pages(stream):
        return jnp.concatenate([x[pl.ds(stream, page, stride=ns), :] for x in x_refs], axis=0).astype(BF16)

    def main(near):
        for g in range(N_KV):
            fold(g, pages(2 * N_KV + g), pages(3 * N_KV + g), j * width, None, near)

    @pl.when(j < nj - 1)
    def _():
        main(False)

    @pl.when(j == nj - 1)
    def _():
        main(True)
        for g in range(N_KV):
            kt = tail_ref[:, (2 * N_KV + g) * LANE:(2 * N_KV + g + 1) * LANE].astype(BF16)
            vt = tail_ref[:, (3 * N_KV + g) * LANE:(3 * N_KV + g + 1) * LANE].astype(BF16)
            fold(g, kt, vt, p_len, n_tail, True)
        out = acc_ref[...] / jnp.maximum(l_ref[...][:, :, :, :1], 1e-30)
        o_ref[...] = jnp.concatenate([out[g, h] for g in range(N_KV) for h in range(hpg)], axis=1)


def _sel_paged(q, cache, lidx, page_table, rows_t, n_tail, sel, tab):
    b, tq, d_b = q.shape
    n_pages = page_table.shape[1]
    ns = 4 * N_KV
    page = cache.shape[2] // ns
    npp = next(k for k in (PAGES_PER_STEP, 4, 2, 1) if n_pages % k == 0)
    assert npp * page > FAR
    hpg = d_b // N_KV // HEAD_B
    whole = lambda a: pl.BlockSpec((None,) + a.shape[1:], lambda i, p, pt: (i,) + (0,) * (a.ndim - 1))
    grid_spec = pltpu.PrefetchScalarGridSpec(
        num_scalar_prefetch=1, grid=(b, n_pages // npp),
        in_specs=[whole(q)] + _page_specs(cache, lidx, npp) + [whole(rows_t), whole(sel),
                  pl.BlockSpec(tab.shape, lambda i, p, pt: (0, 0))],
        out_specs=pl.BlockSpec((None, tq, d_b), lambda i, p, pt: (i, 0, 0)),
        scratch_shapes=[pltpu.VMEM((N_KV, hpg, tq, LANE), F32), pltpu.VMEM((N_KV, hpg, tq, LANE), F32),
                        pltpu.VMEM((N_KV, hpg, tq, HEAD_B), F32)])
    return pl.pallas_call(
        functools.partial(_sel_paged_kernel, ns=ns, n_tail=n_tail, p_len=n_pages * page), grid_spec=grid_spec,
        out_shape=jax.ShapeDtypeStruct((b, tq, d_b), F32), compiler_params=_cp(2), name="nsa_sel_attn_paged",
    )(page_table, q, *([cache] * npp), rows_t, sel, tab)


def _win_cached(q, cwin, win_t, n_tail, tab, p_len):
    b, tq, d_b = q.shape
    gw = d_b // N_KV
    hpg = gw // HEAD_B
    wb = cwin.shape[1]
    col = lambda rows, c0: pl.BlockSpec((None, rows, LANE), lambda i, g: (i, 0, c0 + g))
    qspec = pl.BlockSpec((None, tq, gw), lambda i, g: (i, 0, g))
    return pl.pallas_call(
        functools.partial(_win_cached_kernel, q0=p_len, kbase=p_len - wb, n_tail=n_tail), grid=(b, N_KV),
        in_specs=[qspec, col(wb, 0), col(wb, N_KV), col(LANE, 0), col(LANE, N_KV),
                  pl.BlockSpec(tab.shape, lambda i, g: (0, 0))],
        out_specs=qspec, out_shape=jax.ShapeDtypeStruct((b, tq, d_b), F32),
        scratch_shapes=[pltpu.VMEM((hpg, tq, LANE), F32), pltpu.VMEM((hpg, tq, LANE), F32),
                        pltpu.VMEM((hpg, tq, HEAD_B), F32)],
        compiler_params=_cp(2), name="nsa_win_attn_cached",
    )(q, cwin, cwin, win_t, win_t, tab)


def _attn_res_kernel(*refs, mode, tq, ch, far_w):
    if mode == "sel":
        q_ref, k_ref, v_ref, sel_ref, tab_ref, o_ref, m_ref, l_ref, acc_ref = refs
    else:
        q_ref, k_ref, v_ref, tab_ref, o_ref, m_ref, l_ref, acc_ref = refs
    g = pl.program_id(1)
    qt = pl.program_id(2)
    hpg = q_ref.shape[1] // HEAD_B
    qbase = qt * tq
    near0 = qbase - LANE
    qs = _stack_heads(q_ref)
    m_ref[...] = jnp.full_like(m_ref, NEG)
    l_ref[...] = jnp.zeros_like(l_ref)
    acc_ref[...] = jnp.zeros_like(acc_ref)

    def tile(stream, kpos0, width, near):
        qrow = qbase + lax.broadcasted_iota(jnp.int32, (tq, width), 0)
        kcol = kpos0 + lax.broadcasted_iota(jnp.int32, (tq, width), 1)
        dist = qrow - kcol
        mask = (dist >= 0) if near else (kcol < near0)
        if mode == "win":
            mask &= dist < WINDOW
        else:
            nsp = sel_ref.shape[1]
            blk = (kpos0 + lax.broadcasted_iota(jnp.int32, (nsp, width), 1)) // L_SEL
            expand = (blk == lax.broadcasted_iota(jnp.int32, (nsp, width), 0)).astype(BF16)
            mask &= _dot(sel_ref[...].astype(BF16), expand) > 0.5
        kb = k_ref[pl.ds(kpos0, width), :].astype(BF16)
        vb = v_ref[pl.ds(kpos0, width), :].astype(BF16)
        rel = _rel_bias(dist, g, hpg, tab_ref) if near else None
        return (kb, vb, mask, rel, m_ref.at[stream], l_ref.at[stream], acc_ref.at[stream])

    near = lambda: tile(1, pl.multiple_of(jnp.maximum(near0, 0), LANE), 2 * LANE, True)
    if mode == "sel":
        def body(c, carry):
            _online_softmax_update(qs, [tile(0, pl.multiple_of(c * ch, ch), ch, False)])
            return carry

        lax.fori_loop(0, (jnp.maximum(near0, 0) + ch - 1) // ch, body, 0)
        _online_softmax_update(qs, [near()])
    elif far_w:
        far = tile(0, pl.multiple_of(jnp.maximum(qbase - WINDOW, 0), LANE), far_w, False)
        _online_softmax_update(qs, [far, near()])
    else:
        _online_softmax_update(qs, [near()])
    m0, m1 = m_ref[0][:, :, :1], m_ref[1][:, :, :1]
    m = jnp.maximum(m0, m1)
    w0, w1 = jnp.exp(m0 - m), jnp.exp(m1 - m)
    l = w0 * l_ref[0][:, :, :1] + w1 * l_ref[1][:, :, :1]
    out = (w0 * acc_ref[0] + w1 * acc_ref[1]) / jnp.maximum(l, 1e-30)
    o_ref[...] = jnp.concatenate([out[h] for h in range(hpg)], axis=1)


def _attn_res(q, kv, kcol, vcol, tab, *, mode, sel=None, name):
    b, t, d_b = q.shape
    gw = d_b // N_KV
    hpg = gw // HEAD_B
    tq = LANE
    assert t % tq == 0 and t >= 2 * LANE
    ch = next(c for c in (1024, 512, 256, 128) if t % c == 0)
    far_w = min(WINDOW - LANE, t - 2 * LANE)
    args = [q, kv, kv]
    in_specs = [pl.BlockSpec((None, tq, gw), lambda i, g, qt: (i, qt, g)),
                pl.BlockSpec((None, t, LANE), lambda i, g, qt: (i, 0, kcol + g)),
                pl.BlockSpec((None, t, LANE), lambda i, g, qt: (i, 0, vcol + g))]
    if mode == "sel":
        args.append(sel)
        in_specs.append(pl.BlockSpec((None, None, tq, sel.shape[3]), lambda i, g, qt: (i, g, qt, 0)))
    args.append(tab)
    in_specs.append(pl.BlockSpec(tab.shape, lambda i, g, qt: (0, 0)))
    return pl.pallas_call(
        functools.partial(_attn_res_kernel, mode=mode, tq=tq, ch=ch, far_w=far_w),
        grid=(b, N_KV, t // tq), in_specs=in_specs,
        out_specs=pl.BlockSpec((None, tq, gw), lambda i, g, qt: (i, qt, g)),
        out_shape=jax.ShapeDtypeStruct((b, t, d_b), F32),
        scratch_shapes=[pltpu.VMEM((2, hpg, tq, LANE), F32), pltpu.VMEM((2, hpg, tq, LANE), F32),
                        pltpu.VMEM((2, hpg, tq, HEAD_B), F32)],
        compiler_params=_cp(3), name=name,
    )(*args)


def _nsa_out_kernel(oc_ref, os_ref, ow_ref, gl_ref, e_ref, w_ref, y_ref):
    gate = _sigmoid(gl_ref[...])
    o = (_dot_x(gate, e_ref[0]) * oc_ref[...] + _dot_x(gate, e_ref[1]) * os_ref[...]
         + _dot_x(gate, e_ref[2]) * ow_ref[...])
    y_ref[...] = _dot(o.astype(BF16), w_ref[...]).astype(y_ref.dtype)


def _nsa_out(o_c, o_s, o_w, gl, w_ob):
    m, d_b = o_c.shape
    n = w_ob.shape[1]
    h_b = d_b // HEAD_B
    tm = _tile(m, 256)
    e = np.zeros((3, LANE, d_b), np.float32)
    for br in range(3):
        for hh in range(h_b):
            e[br, br * h_b + hh, hh * HEAD_B:(hh + 1) * HEAD_B] = 1.0
    e = jnp.asarray(e, BF16)
    row = pl.BlockSpec((tm, d_b), lambda i: (i, 0))
    return pl.pallas_call(
        _nsa_out_kernel, grid=(m // tm,),
        in_specs=[row, row, row, pl.BlockSpec((tm, LANE), lambda i: (i, 0)),
                  pl.BlockSpec(e.shape, lambda i: (0, 0, 0)), pl.BlockSpec(w_ob.shape, lambda i: (0, 0))],
        out_specs=pl.BlockSpec((tm, n), lambda i: (i, 0)), out_shape=jax.ShapeDtypeStruct((m, n), BF16),
        compiler_params=_cp(1), name="nsa_out",
    )(o_c, o_s, o_w, gl, e, w_ob)


def _nsa(q, rows, win, gl, tab, lw, *, t_real, paged, lidx, cwin):
    b, tqa, d_b = q.shape
    t = t_real
    ncmp = 2 * N_KV
    cw = S_CMP * LANE
    p_len = 0 if paged is None else paged[1].shape[1] * paged[0].shape[2]
    l_tot = p_len + t
    assert p_len % (S_CMP * SUBLANE) == 0 and (p_len == 0 or t <= S_CMP) and (p_len > 0 or t % LANE == 0)
    pe = jnp.repeat(lw["cmp_pe"].reshape(2, 1, 2, cw), N_KV, axis=1)
    pe = jnp.broadcast_to(pe[None], (b, 2, N_KV, 2, cw)).reshape(b * ncmp, 2, cw)
    pad_rows = jnp.zeros((b * ncmp, SUBLANE - 3, cw), F32)
    if paged is None:
        ex = jnp.concatenate([jnp.zeros((b * ncmp, 1, cw), F32), pe, pad_rows], axis=1)
        x_spec = pl.BlockSpec((None, t, LANE), lambda i: (i // ncmp, 0, i % ncmp))
        kcv = _compress(rows, x_spec, t, ex, lw["w1cat"], lw["w2"])
    else:
        new_cmp = rows[:, :, :ncmp * LANE].reshape(b, t, ncmp, LANE).transpose(0, 2, 1, 3)
        ex0 = jnp.pad(new_cmp.reshape(b * ncmp, 1, t * LANE), ((0, 0), (0, 0), (0, cw - t * LANE)))
        ex = jnp.concatenate([ex0, pe, pad_rows], axis=1)
        kcv = _compress_paged(paged[0], lidx, paged[1], ex, lw["w1cat"], lw["w2"])
    if kcv.shape[1] % LANE:
        kcv = jnp.pad(kcv, ((0, 0), (0, _round_up(kcv.shape[1], LANE) - kcv.shape[1]), (0, 0)))
    tq = tqa if p_len else next(c for c in (4 * LANE, 2 * LANE, LANE) if tqa % c == 0)
    o_c, imp = _cmp_attn(q, kcv, tab, t_keys=l_tot, q0=p_len, tq=tq)
    ns = -(-l_tot // L_SEL)
    nsp = _round_up(ns, LANE)
    if p_len == 0:
        sel = jnp.swapaxes(_select(imp, jnp.arange(tqa, dtype=jnp.int32)[None], ns), 2, 3)
    else:
        nq = b * tqa
        nql = _round_up(nq, LANE)
        nsr = imp.shape[2]
        impl = jnp.pad(imp[..., :tqa].transpose(1, 2, 0, 3).reshape(1, N_KV, nsr, nq),
                       ((0, 0), (0, 0), (0, 0), (0, nql - nq)))
        qpos = (p_len + jnp.arange(nql, dtype=jnp.int32) % tqa)[None]
        sel = _select(impl, qpos, ns)[0, :, :, :nq].reshape(N_KV, nsr, b, tqa).transpose(2, 0, 3, 1)
    sel = jnp.pad(sel, ((0, 0), (0, 0), (0, 0), (0, nsp - sel.shape[3])))
    if paged is None:
        o_s = _attn_res(q, rows, 2 * N_KV, 3 * N_KV, tab, mode="sel", sel=sel, name="nsa_sel_attn")
        o_w = _attn_res(q, win, 0, N_KV, tab, mode="win", name="nsa_win_attn")
    else:
        rows_t = jnp.pad(rows, ((0, 0), (0, LANE - t), (0, 0)))
        win_t = jnp.pad(win, ((0, 0), (0, LANE - t), (0, 0)))
        o_s = _sel_paged(q, paged[2], lidx, paged[1], rows_t, t, sel, tab)
        o_w = _win_cached(q, cwin, win_t, t, tab, p_len)
    m = b * tqa
    return _nsa_out(o_c.reshape(m, d_b), o_s.reshape(m, d_b), o_w.reshape(m, d_b), gl, lw["w_ob"])


def _layer_weights(l, w_in, mu, w0, w_up, a0, a_up, g_up, k_k, k_a, r_k, lnx_w, lnx_b, w_oa,
                   cmp_pe, cmp_w1, cmp_w2, w_ob, w_o, w_ff_up, w_ff_down, norm_g):
    d_a = w_oa.shape[1]
    d_b = w_ob.shape[1]
    d = w_o.shape[1]
    rw = 3 * d_a + DECAY_LORA + A_LORA + GATE_LORA
    pw = 3 * d_a + LORA_W
    nrow = 4 * N_KV * HEAD_B
    nwin = 2 * N_KV * HEAD_B
    ngl = 3 * (d_b // HEAD_B)
    wi = w_in[l]
    o_q = rw
    o_rows = o_q + d_b
    o_win = o_rows + nrow
    o_gl = o_win + nwin
    o_pg = o_gl + ngl
    wl = jnp.zeros((LORA_W, 3 * d_a), F32)
    wl = wl.at[:DECAY_LORA, :d_a].set(w_up[l])
    wl = wl.at[DECAY_LORA:DECAY_LORA + A_LORA, d_a:2 * d_a].set(a_up[l])
    wl = wl.at[DECAY_LORA + A_LORA:DECAY_LORA + A_LORA + GATE_LORA, 2 * d_a:].set(g_up[l])
    half = S_CMP * HEAD_B
    return dict(
        w_pa=jnp.pad(wi[:, :rw], ((0, 0), (0, pw - rw))).astype(BF16),
        w_q=wi[:, o_q:o_rows].astype(BF16),
        w_rows=wi[:, o_rows:o_win].astype(BF16),
        w_win=wi[:, o_win:o_gl].astype(BF16),
        w_gl=jnp.pad(wi[:, o_gl:o_pg], ((0, 0), (0, LANE - ngl))).astype(BF16),
        w_pg=wi[:, o_pg:o_pg + 2 * d].astype(BF16),
        mu=jnp.pad(mu[l], (0, pw - rw))[None], wl=wl.astype(BF16),
        w0=w0[l][None], a0=a0[l][None], k_k=k_k[l][None], k_a=k_a[l][None],
        r_k=r_k[l].reshape(1, d_a), lnx_w=lnx_w[l][None], lnx_b=lnx_b[l][None],
        w_oa=w_oa[l].astype(BF16), w_ob=w_ob[l].astype(BF16), w_o=w_o[l].astype(BF16),
        w_up=w_ff_up[l].astype(BF16), w_down=w_ff_down[l].astype(BF16),
        cmp_pe=cmp_pe[l],
        w1cat=jnp.concatenate([cmp_w1[l][:, :half], cmp_w1[l][:, half:]], axis=2).astype(BF16),
        w2=cmp_w2[l].astype(BF16), g=norm_g[l], rw=rw,
    )


def _layer(x, xn, lw, g_next, tab, shift0, s0, paged, lidx, cwin, rows_buf):
    b, t, d = x.shape
    m = b * t
    d_a = lw["w_oa"].shape[0]
    x2 = x.reshape(m, d)
    g = lw["g"]
    pa = _matmul(xn, lw["w_pa"], F32, name="proj_rwkv")
    q = _matmul(xn, lw["w_q"], BF16, name="proj_q")
    if rows_buf is None:
        rows = _matmul(xn, lw["w_rows"], F32, name="proj_rows")
    else:
        rows, rows_buf = _matmul_rows(xn, lw["w_rows"], rows_buf, lidx)
    win = _matmul(xn, lw["w_win"], F32, name="proj_win")
    gl = _matmul(xn, lw["w_gl"], F32, name="proj_gl")
    pg = _matmul(xn, lw["w_pg"], BF16, name="proj_pg")
    pw = pa.shape[1]
    d_b = q.shape[1]
    tp = t if t % RWKV_CHUNK == 0 else _round_up(t, SUBLANE)
    c = RWKV_CHUNK if t % RWKV_CHUNK == 0 else tp
    pa3 = pa.reshape(b, t, pw)
    pa_p = pa3 if tp == t else jnp.pad(pa3, ((0, 0), (0, tp - t), (0, 0)))
    prev = jnp.pad(shift0, ((0, 0), (0, pw - shift0.shape[1])))[:, None]
    r, lgw, k2, v, kk, a, gg = _rwkv_prep(pa_p, prev, lw["mu"], lw["wl"], lw["w0"], lw["a0"], lw["k_k"], lw["k_a"],
                                          None if tp == t else t)
    o, s_fin = _rwkv_scan(r, lgw, k2, v, kk, a, s0, c)
    flat = lambda u: u.reshape(b * tp, d_a)
    ya = _rwkv_post(flat(o), flat(r), flat(k2), flat(v), flat(gg), lw["lnx_w"], lw["lnx_b"], lw["r_k"], lw["w_oa"])
    if tp != t:
        ya = ya.reshape(b, tp, d)[:, :t].reshape(m, d)
    sh = pa3[:, t - 1, :lw["rw"]]
    tqa = t if t % LANE == 0 else _round_up(t, SUBLANE)
    q3 = q.reshape(b, t, d_b)
    gl_p = gl
    if tqa != t:
        q3 = jnp.pad(q3, ((0, 0), (0, tqa - t), (0, 0)))
        gl_p = jnp.pad(gl.reshape(b, t, LANE), ((0, 0), (0, tqa - t), (0, 0))).reshape(b * tqa, LANE)
    rows3 = rows.reshape(b, t, rows.shape[1])
    win3 = win.reshape(b, t, win.shape[1])
    yb = _nsa(q3, rows3, win3, gl_p, tab, lw, t_real=t, paged=paged, lidx=lidx, cwin=cwin)
    if tqa != t:
        yb = yb.reshape(b, tqa, d)[:, :t].reshape(m, d)
    x1, xn1 = _merge(ya, yb, pg, x2, lw["w_o"], g[1:3])
    h = _matmul(xn1, lw["w_up"], BF16, relu2=True, name="ffn_up")
    x_out, xn_out = _ffn_down(h, lw["w_down"], x1, jnp.concatenate([g[3:4], g_next], axis=0))
    wctx = win3 if cwin is None else jnp.concatenate([cwin, win3], axis=1)
    n_keep = min(WINDOW, wctx.shape[1])
    new_rows = rows3 if rows_buf is None else rows_buf
    return x_out.reshape(b, t, d), xn_out, new_rows, wctx[:, wctx.shape[1] - n_keep:], sh, s_fin


def _pair_states(s):
    b, h, n, _ = s.shape
    s = s.reshape(b, h // 2, 2, n, n)
    z = jnp.zeros((b, h // 2, n, n), s.dtype)
    top = jnp.concatenate([s[:, :, 0], z], axis=3)
    bot = jnp.concatenate([z, s[:, :, 1]], axis=3)
    return jnp.concatenate([top, bot], axis=2)


def _unpair_states(s):
    n = HEAD_A
    b, hp = s.shape[:2]
    return jnp.stack([s[:, :, :n, :n], s[:, :, n:, n:]], axis=2).reshape(b, 2 * hp, n, n)


def _trunk(x, shift0, wkv0, paged, cache_win, tab, layers):
    rows, wins, shifts, wkvs = [], [], [], []
    b, t = x.shape[:2]
    ns = 4 * N_KV
    xn = _rmsnorm_cast(x.reshape(-1, x.shape[2]), layers[0]["g"][0:1])
    rows_buf = jnp.zeros((len(layers), b * t * ns, HEAD_B), F32) if (b * t) % LANE == 0 else None
    for l, lw in enumerate(layers):
        g_next = layers[l + 1]["g"][0:1] if l + 1 < len(layers) else lw["g"][3:4]
        x, xn, nr, nw, sh, st = _layer(x, xn, lw, g_next, tab, shift0[l], _pair_states(wkv0[l]),
                                       paged, l, None if cache_win is None else cache_win[l], rows_buf)
        if rows_buf is None:
            rows.append(nr.reshape(b, t, 4, N_KV, HEAD_B))
        else:
            rows_buf = nr
        wins.append(nw.reshape(b, nw.shape[1], 2, N_KV, HEAD_B))
        shifts.append(sh)
        wkvs.append(_unpair_states(st))
    rows = jnp.stack(rows) if rows_buf is None else rows_buf.reshape(len(layers), b, t, 4, N_KV, HEAD_B)
    return x, rows, jnp.stack(wins), jnp.stack(shifts), jnp.stack(wkvs)


def kernel(x_prompt, x_sample, cache_kv, cache_win, state_shift, state_wkv, page_table, w_in, mu, w0, w_up, a0, a_up, g_up, k_k, k_a, r_k, lnx_w, lnx_b, w_oa, cmp_pe, cmp_w1, cmp_w2, w_ob, w_o, w_ff_up, w_ff_down, norm_g, rel_bias):
    depth = w_in.shape[0]
    layers = [_layer_weights(l, w_in, mu, w0, w_up, a0, a_up, g_up, k_k, k_a, r_k, lnx_w, lnx_b, w_oa,
                             cmp_pe, cmp_w1, cmp_w2, w_ob, w_o, w_ff_up, w_ff_down, norm_g) for l in range(depth)]
    tab = rel_bias[_t5_bucket_table()].T
    bp = x_prompt.shape[0]
    rw = state_shift.shape[2]
    zeros_shift = jnp.zeros((depth, bp, rw), x_prompt.dtype)
    zeros_wkv = jnp.zeros((depth, bp) + state_wkv.shape[2:], state_wkv.dtype)
    y_p, kv_p, win_p, sh_p, wkv_p = _trunk(x_prompt, zeros_shift, zeros_wkv, None, None, tab, layers)
    nl, n_pool, page = cache_kv.shape[:3]
    paged = (cache_kv.reshape(nl, n_pool, page, 2, 2 * N_KV, HEAD_B), page_table,
             cache_kv.reshape(nl, n_pool, page * 4 * N_KV, HEAD_B))
    cwin = cache_win.reshape(cache_win.shape[:3] + (-1,))
    y_s, kv_s, win_s, sh_s, wkv_s = _trunk(x_sample, state_shift, state_wkv, paged, cwin, tab, layers)
    return (y_p, y_s, kv_p, kv_s, win_p, win_s, sh_p, sh_s, wkv_p, wkv_s)
```

```python
import functools
import math

import numpy as np
import jax
import jax.numpy as jnp
from jax import lax
from jax.experimental import pallas as pl
from jax.experimental.pallas import tpu as pltpu

F32 = jnp.float32
BF16 = jnp.bfloat16

HEAD_A = 64
DECAY_LORA = 64
A_LORA = 64
GATE_LORA = 160
LNX_EPS = 64e-5
N_KV = 2
HEAD_B = 128
L_CMP = 32
S_CMP = 16
L_SEL = 64
TOP_N = 16
WINDOW = 512
FORCE_BONUS = 1e4
NEG = -1e30
NUM_BUCKETS = 32
MAX_DIST = 128
EPS = 1e-6

LANE = 128
SUBLANE = 8
VMEM_LIMIT = 56 * 1024 * 1024
RWKV_CHUNK = 64
LORA_W = 384
FAR = LANE - 1


def _cp(n_axes):
    return pltpu.CompilerParams(dimension_semantics=("arbitrary",) * n_axes,
                                vmem_limit_bytes=VMEM_LIMIT)


def _round_up(x, m):
    return -(-x // m) * m


def _tile(n, pref, mult=SUBLANE):
    if n <= pref:
        return n
    for t in range(pref - pref % mult, 0, -mult):
        if n % t == 0:
            return t
    return n


def _dot(a, b):
    return jnp.dot(a, b, preferred_element_type=F32)


def _dot_nt(a, b):
    return lax.dot_general(a, b, (((1,), (1,)), ((), ())), preferred_element_type=F32)


def _dot_tn(a, b):
    return lax.dot_general(a, b, (((0,), (0,)), ((), ())), preferred_element_type=F32)


def _split(x):
    hi = x.astype(BF16)
    lo = (x - hi.astype(F32)).astype(BF16)
    return hi, lo


def _dot_x(a, b):
    hi, lo = _split(a)
    return _dot(hi, b) + _dot(lo, b)


def _dot3(a, b):
    ah, al = _split(a)
    bh, bl = _split(b)
    return _dot(ah, bh) + _dot(ah, bl) + _dot(al, bh)


def _sigmoid(x):
    return 1.0 / (1.0 + jnp.exp(-x))


def _rms(y, g):
    return y * lax.rsqrt(jnp.mean(y * y, axis=-1, keepdims=True) + EPS) * g


def _rmsnorm_kernel(x_ref, g_ref, o_ref):
    o_ref[...] = _rms(x_ref[...], g_ref[...]).astype(o_ref.dtype)


def _rmsnorm_cast(x, g):
    m, d = x.shape
    tm = _tile(m, 256)
    return pl.pallas_call(
        _rmsnorm_kernel, grid=(m // tm,),
        in_specs=[pl.BlockSpec((tm, d), lambda i: (i, 0)), pl.BlockSpec((1, d), lambda i: (0, 0))],
        out_specs=pl.BlockSpec((tm, d), lambda i: (i, 0)),
        out_shape=jax.ShapeDtypeStruct((m, d), BF16), compiler_params=_cp(1), name="rmsnorm_cast",
    )(x, g)


def _mm_kernel(a_ref, w_ref, o_ref, *, relu2):
    y = _dot(a_ref[...], w_ref[...])
    if relu2:
        y = jnp.square(jnp.maximum(y, 0.0))
    o_ref[...] = y.astype(o_ref.dtype)


def _matmul(a, w, out_dtype, relu2=False, name="matmul"):
    m, k = a.shape
    n = w.shape[1]
    tm = _tile(m, 1024)
    tn = _tile(n, 1280, LANE)
    return pl.pallas_call(
        functools.partial(_mm_kernel, relu2=relu2), grid=(m // tm, n // tn),
        in_specs=[pl.BlockSpec((tm, k), lambda i, j: (i, 0)), pl.BlockSpec((k, tn), lambda i, j: (0, j))],
        out_specs=pl.BlockSpec((tm, tn), lambda i, j: (i, j)),
        out_shape=jax.ShapeDtypeStruct((m, n), out_dtype), compiler_params=_cp(2), name=name,
    )(a, w)


def _mm_rows_kernel(a_ref, w_ref, buf_ref, o_ref, il_ref):
    del buf_ref
    y = _dot(a_ref[...], w_ref[...])
    o_ref[...] = y
    ns = y.shape[1] // LANE
    for s in range(ns):
        il_ref[pl.ds(s, y.shape[0], stride=ns), :] = y[:, s * LANE:(s + 1) * LANE]


def _matmul_rows(a, w, buf, lidx):
    m, k = a.shape
    n = w.shape[1]
    ns = n // LANE
    tm = _tile(m, 1024)
    return pl.pallas_call(
        _mm_rows_kernel, grid=(m // tm,),
        in_specs=[pl.BlockSpec((tm, k), lambda i: (i, 0)), pl.BlockSpec((k, n), lambda i: (0, 0)),
                  pl.BlockSpec(memory_space=pl.ANY)],
        out_specs=[pl.BlockSpec((tm, n), lambda i: (i, 0)), pl.BlockSpec((None, tm * ns, LANE), lambda i: (lidx, i, 0))],
        out_shape=[jax.ShapeDtypeStruct((m, n), F32), jax.ShapeDtypeStruct(buf.shape, buf.dtype)],
        input_output_aliases={2: 1}, compiler_params=_cp(1), name="proj_rows",
    )(a, w, buf)


def _ffn_down_kernel(h_ref, w_ref, x_ref, g_ref, o_ref, n_ref, acc_ref):
    k = pl.program_id(1)

    @pl.when(k == 0)
    def _():
        acc_ref[...] = jnp.zeros_like(acc_ref)

    acc_ref[...] += _dot(h_ref[...], w_ref[...])

    @pl.when(k == pl.num_programs(1) - 1)
    def _():
        o = x_ref[...] + _rms(acc_ref[...], g_ref[0:1])
        o_ref[...] = o
        n_ref[...] = _rms(o, g_ref[1:2]).astype(n_ref.dtype)


def _ffn_down(h, w, x, g):
    m, kdim = h.shape
    n = w.shape[1]
    tm = _tile(m, 512)
    tk = _tile(kdim, 2048, LANE)
    row = pl.BlockSpec((tm, n), lambda i, k: (i, 0))
    return pl.pallas_call(
        _ffn_down_kernel, grid=(m // tm, kdim // tk),
        in_specs=[pl.BlockSpec((tm, tk), lambda i, k: (i, k)), pl.BlockSpec((tk, n), lambda i, k: (k, 0)),
                  row, pl.BlockSpec((2, n), lambda i, k: (0, 0))],
        out_specs=[row, row],
        out_shape=[jax.ShapeDtypeStruct((m, n), F32), jax.ShapeDtypeStruct((m, n), BF16)],
        scratch_shapes=[pltpu.VMEM((tm, n), F32)], compiler_params=_cp(2), name="ffn_down",
    )(h, w, x, g)


def _merge_kernel(ya_ref, yb_ref, pg_ref, x_ref, w_ref, g_ref, o_ref, n_ref):
    d = ya_ref.shape[1]
    pg = pg_ref[...].astype(F32)
    mix = _sigmoid(pg[:, :d]) * ya_ref[...].astype(F32) + _sigmoid(pg[:, d:]) * yb_ref[...].astype(F32)
    y = _dot(mix.astype(BF16), w_ref[...])
    o = x_ref[...] + _rms(y, g_ref[0:1])
    o_ref[...] = o
    n_ref[...] = _rms(o, g_ref[1:2]).astype(n_ref.dtype)


def _merge(ya, yb, pg, x, w_o, g):
    m, d = x.shape
    tm = _tile(m, 256)
    row = lambda c: pl.BlockSpec((tm, c), lambda i: (i, 0))
    return pl.pallas_call(
        _merge_kernel, grid=(m // tm,),
        in_specs=[row(d), row(d), row(2 * d), row(d),
                  pl.BlockSpec((d, d), lambda i: (0, 0)), pl.BlockSpec((2, d), lambda i: (0, 0))],
        out_specs=[row(d), row(d)],
        out_shape=[jax.ShapeDtypeStruct((m, d), F32), jax.ShapeDtypeStruct((m, d), BF16)],
        compiler_params=_cp(1), name="merge",
    )(ya, yb, pg, x, w_o, g)


def _head_indicator(d_a):
    h = np.zeros((d_a, LANE), np.float32)
    h[np.arange(d_a), np.arange(d_a) // HEAD_A] = 1.0
    return jnp.asarray(h, BF16), jnp.asarray(h.T, BF16)


def _rwkv_prep_kernel(pa_ref, prev_ref, mu_ref, wl_ref, w0_ref, a0_ref, kk_ref, ka_ref, hd_ref, hdt_ref,
                      r_ref, lw_ref, k2_ref, v_ref, kkn_ref, a_ref, g_ref, carry_ref, *, d_a, t_real):
    t = pl.program_id(1)
    tt = pa_ref.shape[1]

    @pl.when(t == 0)
    def _():
        carry_ref[...] = prev_ref[0]

    x = pa_ref[0]
    row = lax.broadcasted_iota(jnp.int32, x.shape, 0)
    xprev = jnp.where(row == 0, carry_ref[...], pltpu.roll(x, 1, 0))
    carry_ref[...] = pa_ref[0, pl.ds(tt - 1, 1), :]
    xs = x + (xprev - x) * mu_ref[...]
    r = xs[:, :d_a]
    k = xs[:, d_a:2 * d_a]
    v = xs[:, 2 * d_a:3 * d_a]
    lo = xs[:, 3 * d_a:]
    lane = lax.broadcasted_iota(jnp.int32, lo.shape, 1)
    act = jnp.where(lane < DECAY_LORA, jnp.tanh(lo),
                    jnp.where(lane < DECAY_LORA + A_LORA, lo, _sigmoid(lo)))
    lin = _dot(act.astype(BF16), wl_ref[...])
    z = -(w0_ref[...] + lin[:, :d_a])
    w = -(jnp.maximum(z, 0.0) + jnp.log(1.0 + jnp.exp(-jnp.abs(z)))) - 0.5
    logw = -jnp.exp(w)
    a = _sigmoid(a0_ref[...] + lin[:, d_a:2 * d_a])
    g = lin[:, 2 * d_a:]
    kkr = k * kk_ref[...]
    ss = _dot_x(kkr * kkr, hd_ref[...])
    inv = 1.0 / jnp.maximum(jnp.sqrt(ss), 1e-12)
    kkn = kkr * _dot_x(inv, hdt_ref[...])
    k2 = k * (1.0 + (a - 1.0) * ka_ref[...])
    if t_real is not None:
        live = (t * tt + lax.broadcasted_iota(jnp.int32, r.shape, 0)) < t_real
        zero = jnp.zeros_like(r)
        r, logw, k2, v, kkn, a = (jnp.where(live, u, zero) for u in (r, logw, k2, v, kkn, a))
    r_ref[0] = r.astype(r_ref.dtype)
    lw_ref[0] = logw
    k2_ref[0] = k2.astype(k2_ref.dtype)
    v_ref[0] = v.astype(v_ref.dtype)
    kkn_ref[0] = kkn.astype(kkn_ref.dtype)
    a_ref[0] = a.astype(a_ref.dtype)
    g_ref[0] = g.astype(g_ref.dtype)


def _rwkv_prep(pa, prev, mu, wl, w0, a0, k_k, k_a, t_real):
    b, t, p = pa.shape
    d_a = w0.shape[1]
    tt = _tile(t, 256)
    hd, hdt = _head_indicator(d_a)
    full = lambda a: pl.BlockSpec(a.shape, lambda i, j: (0,) * a.ndim)
    out = [jax.ShapeDtypeStruct((b, t, d_a), F32 if i == 1 else BF16) for i in range(7)]
    ospec = pl.BlockSpec((1, tt, d_a), lambda i, j: (i, j, 0))
    return pl.pallas_call(
        functools.partial(_rwkv_prep_kernel, d_a=d_a, t_real=t_real), grid=(b, t // tt),
        in_specs=[pl.BlockSpec((1, tt, p), lambda i, j: (i, j, 0)), pl.BlockSpec((1, 1, p), lambda i, j: (i, 0, 0)),
                  full(mu), full(wl), full(w0), full(a0), full(k_k), full(k_a), full(hd), full(hdt)],
        out_specs=[ospec] * 7, out_shape=out,
        scratch_shapes=[pltpu.VMEM((1, p), F32)], compiler_params=_cp(2), name="rwkv_prep",
    )(pa, prev, mu, wl, w0, a0, k_k, k_a, hd, hdt)


def _mmb(a, b, form="nn"):
    dot = {"nn": _dot, "nt": _dot_nt, "tn": _dot_tn}[form]
    return dot(a.astype(BF16), b.astype(BF16))


def _rwkv_scan_kernel(r_ref, lw_ref, k2_ref, v_ref, kk_ref, a_ref, s0_ref, o_ref, sT_ref, s_ref, *, c):
    @pl.when(pl.program_id(1) == 0)
    def _():
        s_ref[...] = s0_ref[0]

    hk = HEAD_A
    pairs = range(s_ref.shape[0])
    sls = [slice(p * LANE, (p + 1) * LANE) for p in pairs]
    lane = lax.broadcasted_iota(jnp.int32, (c, LANE), 1)
    m0 = lane < hk
    rowi = lax.broadcasted_iota(jnp.int32, (c, c), 0)
    coli = lax.broadcasted_iota(jnp.int32, (c, c), 1)
    ltri = (coli <= rowi).astype(BF16)
    lw = [lw_ref[0, :, sl] for sl in sls]
    cum = [_dot_x_rhs(ltri, x) for x in lw]
    e_wi = [jnp.exp(-x) for x in cum]
    kk = [kk_ref[0, :, sl].astype(F32) for sl in sls]
    at = [-k * jnp.exp(x - y) for k, x, y in zip(kk, cum, lw)]
    bt = [k * a_ref[0, :, sl].astype(F32) * e for k, sl, e in zip(kk, sls, e_wi)]
    kt = [k2_ref[0, :, sl].astype(F32) * e for sl, e in zip(sls, e_wi)]
    rt = [r_ref[0, :, sl].astype(F32) * jnp.exp(x) for sl, x in zip(sls, cum)]
    v = [v_ref[0, :, sl] for sl in sls]
    w_c = [jnp.exp(x[c - 1:c, :]) for x in cum]

    def bd(z):
        zero = jnp.zeros_like(z)
        return jnp.concatenate([jnp.where(m0, z, zero), jnp.where(m0, zero, z)], axis=0)

    prow = lax.broadcasted_iota(jnp.int32, (c, 2 * c), 0)
    pcol = lax.broadcasted_iota(jnp.int32, (c, 2 * c), 1)
    pcol = jnp.where(pcol >= c, pcol - c, pcol)
    strict = pcol < prow
    incl = pcol <= prow
    zero_cc = jnp.zeros((c, 2 * c), F32)
    x2 = [jnp.concatenate([x, y], axis=0) for x, y in zip(at, rt)]
    xb = [_mmb(x, bd(y), "nt") for x, y in zip(x2, bt)]
    xk = [_mmb(x, bd(y), "nt") for x, y in zip(x2, kt)]
    a_ab = [jnp.where(strict, x[:c], zero_cc) for x in xb]
    a_rb = [jnp.where(incl, x[c:], zero_cc) for x in xb]
    a_ak = [jnp.where(strict, x[:c], zero_cc) for x in xk]
    a_rk = [jnp.where(incl, x[c:], zero_cc) for x in xk]

    if 2 * c == LANE:
        bdc = bd
    else:
        mc = lax.broadcasted_iota(jnp.int32, (c, 2 * c), 1) < c

        def bdc(z):
            zero = jnp.zeros_like(z)
            return jnp.concatenate([jnp.where(mc, z, zero), jnp.where(mc, zero, z)], axis=0)

    eye = (pcol == prow).astype(F32)
    tmat = [eye + x for x in a_ab]
    npow = a_ab
    steps = 1
    while 2 * steps < c:
        npow = [_dot3(x, bdc(x)) for x in npow]
        tmat = [x + _dot3(x, bdc(y)) for x, y in zip(tmat, npow)]
        steps *= 2
    ta = [_mmb(x, bd(y)) for x, y in zip(tmat, at)]
    xv = [_mmb(x, bd(y)) for x, y in zip(a_ak, v)]
    tx = [_mmb(x, bd(y)) for x, y in zip(tmat, xv)]
    p_c = [x + _mmb(y, bd(z)) for x, y, z in zip(rt, a_rb, ta)]
    q_c = [_mmb(x, bd(y)) + _mmb(z, bd(u)) for x, y, z, u in zip(a_rb, tx, a_rk, v)]
    lr = lax.broadcasted_iota(jnp.int32, (LANE, LANE), 0)
    lc = lax.broadcasted_iota(jnp.int32, (LANE, LANE), 1)
    same = (lr < hk) == (lc < hk)
    eye_l = (lr == lc).astype(F32)
    zero_l = jnp.zeros((LANE, LANE), F32)
    m_c = [(eye_l + jnp.where(same, _mmb(x, y, "tn"), zero_l)) * w for x, y, w in zip(ta, bt, w_c)]
    n_c = [jnp.where(same, _mmb(x, y, "tn") + _mmb(z, u, "tn"), zero_l) * w
           for x, y, z, u, w in zip(tx, bt, v, kt, w_c)]
    s = [s_ref[p] for p in pairs]
    o = [_mmb(x, y, "nt") + z for x, y, z in zip(p_c, s, q_c)]
    s_new = [_mmb(x, y) + z for x, y, z in zip(s, m_c, n_c)]
    for p in pairs:
        o_ref[0, :, sls[p]] = o[p]
        s_ref[p] = s_new[p]
        sT_ref[0, p] = s_new[p]


def _dot_x_rhs(a, b):
    hi, lo = _split(b)
    return _dot(a, hi) + _dot(a, lo)


def _rwkv_scan(r, lw, k2, v, kk, a, s0, c):
    b, t, d_a = r.shape
    npair = d_a // LANE
    seq = pl.BlockSpec((1, c, d_a), lambda i, j: (i, j, 0))
    st = pl.BlockSpec((1, npair, LANE, LANE), lambda i, j: (i, 0, 0, 0))
    return pl.pallas_call(
        functools.partial(_rwkv_scan_kernel, c=c), grid=(b, t // c),
        in_specs=[seq] * 6 + [st], out_specs=[seq, st],
        out_shape=[jax.ShapeDtypeStruct((b, t, d_a), F32), jax.ShapeDtypeStruct((b, npair, LANE, LANE), F32)],
        scratch_shapes=[pltpu.VMEM((npair, LANE, LANE), F32)], compiler_params=_cp(2), name="rwkv_scan",
    )(r, lw, k2, v, kk, a, s0)


def _rwkv_post_kernel(o_ref, r_ref, k2_ref, v_ref, g_ref, lw_ref, lb_ref, rk_ref, hd_ref, hdt_ref, w_ref, y_ref):
    o = o_ref[...]
    hd = hd_ref[...]
    hdt = hdt_ref[...]
    inv_n = 1.0 / HEAD_A
    mean = _dot_x(_dot_x(o, hd) * inv_n, hdt)
    d = o - mean
    var = _dot_x(d * d, hd) * inv_n
    xo = d * _dot_x(lax.rsqrt(var + LNX_EPS), hdt) * lw_ref[...] + lb_ref[...]
    rk2 = r_ref[...].astype(F32) * k2_ref[...].astype(F32) * rk_ref[...]
    bonus = _dot_x(_dot_x(rk2, hd), hdt) * v_ref[...].astype(F32)
    y_ref[...] = _dot(((xo + bonus) * g_ref[...].astype(F32)).astype(BF16), w_ref[...]).astype(y_ref.dtype)


def _rwkv_post(o, r, k2, v, g, lnx_w, lnx_b, r_k, w_oa):
    m, d_a = o.shape
    n = w_oa.shape[1]
    tm = _tile(m, 256)
    hd, hdt = _head_indicator(d_a)
    row = pl.BlockSpec((tm, d_a), lambda i: (i, 0))
    full = lambda a: pl.BlockSpec(a.shape, lambda i: (0,) * a.ndim)
    return pl.pallas_call(
        _rwkv_post_kernel, grid=(m // tm,),
        in_specs=[row] * 5 + [full(lnx_w), full(lnx_b), full(r_k), full(hd), full(hdt), full(w_oa)],
        out_specs=pl.BlockSpec((tm, n), lambda i: (i, 0)), out_shape=jax.ShapeDtypeStruct((m, n), BF16),
        compiler_params=_cp(1), name="rwkv_post",
    )(o, r, k2, v, g, lnx_w, lnx_b, r_k, hd, hdt, w_oa)


def _t5_bucket_table():
    d = np.arange(LANE)
    max_exact = NUM_BUCKETS // 2
    df = np.maximum(d, 1).astype(np.float32)
    large = max_exact + (np.log(df / np.float32(max_exact)) / np.float32(math.log(MAX_DIST / max_exact))
                         * np.float32(NUM_BUCKETS - max_exact)).astype(np.int32)
    large = np.minimum(large, NUM_BUCKETS - 1)
    tab = np.where(d < max_exact, d, large)
    assert tab[FAR] == NUM_BUCKETS - 1
    return tab


PAGES_PER_STEP = 16


def _page_specs(cache, lidx, npp, half=None):
    if half is None:
        blk = (None, None) + cache.shape[2:]
        return [pl.BlockSpec(blk, lambda i, p, pt, k=k: (lidx, pt[i, p * npp + k], 0, 0)) for k in range(npp)]
    blk = (None, None, cache.shape[2], None) + cache.shape[4:]
    return [pl.BlockSpec(blk, lambda i, p, pt, k=k: (lidx, pt[i, p * npp + k], 0, half, 0, 0)) for k in range(npp)]


def _gelu_tanh(x):
    return 0.5 * x * (1.0 + jnp.tanh(math.sqrt(2.0 / math.pi) * (x + 0.044715 * (x * x * x))))


def _compress_finish(y, ex_ref, w1, w2_ref, o_ref):
    rr = y.shape[0]
    e = _dot(ex_ref[0].astype(BF16), w1)
    nxt = pltpu.roll(y[:, LANE:], rr - 1, 0)
    row = lax.broadcasted_iota(jnp.int32, nxt.shape, 0)
    nxt = jnp.where(row == rr - 1, e[0:1, LANE:], nxt)
    pre = y[:, :LANE] + nxt + (e[1:2, :LANE] + e[2:3, LANE:])
    o_ref[0] = _dot(_gelu_tanh(pre).astype(BF16), w2_ref[0])


def _compress_kernel(x_ref, ex_ref, w1_ref, w2_ref, o_ref):
    rr = x_ref.shape[0] // S_CMP
    w1 = w1_ref[0]
    y = jnp.zeros((rr, 2 * LANE), F32)
    for j in range(S_CMP):
        xj = x_ref[pl.ds(j, rr, stride=S_CMP), :].astype(BF16)
        y = y + _dot(xj, w1[j * LANE:(j + 1) * LANE, :])
    _compress_finish(y, ex_ref, w1, w2_ref, o_ref)


def _compress_finish_kernel(y_ref, ex_ref, w1_ref, w2_ref, o_ref):
    _compress_finish(y_ref[0], ex_ref, w1_ref[0], w2_ref, o_ref)


def _compress_paged_kernel(pt_ref, *refs):
    x_refs, (w1_ref, y_ref) = refs[:-2], refs[-2:]
    page, ncmp, _ = x_refs[0].shape
    cpp = page // S_CMP
    for s in range(ncmp):
        y = jnp.zeros((len(x_refs) * cpp, 2 * LANE), F32)
        for j in range(S_CMP):
            xj = jnp.concatenate([x[pl.ds(j, cpp, stride=S_CMP), s, :] for x in x_refs], axis=0)
            y = y + _dot(xj.astype(BF16), w1_ref[s // N_KV, j * LANE:(j + 1) * LANE, :])
        y_ref[0, s] = y


def _compress(x, x_spec, n_pos, ex, w1cat, w2):
    nb, _, cw = ex.shape
    rr = n_pos // S_CMP
    c_of = lambda i: (i // N_KV) % 2
    return pl.pallas_call(
        _compress_kernel, grid=(nb,),
        in_specs=[x_spec, pl.BlockSpec((1, SUBLANE, cw), lambda i: (i, 0, 0)),
                  pl.BlockSpec((1, cw, 2 * LANE), lambda i: (c_of(i), 0, 0)),
                  pl.BlockSpec((1, LANE, LANE), lambda i: (c_of(i), 0, 0))],
        out_specs=pl.BlockSpec((1, rr, LANE), lambda i: (i, 0, 0)),
        out_shape=jax.ShapeDtypeStruct((nb, rr, LANE), F32), compiler_params=_cp(1), name="nsa_compress",
    )(x, ex, w1cat, w2)


def _compress_paged(cache, lidx, page_table, ex, w1cat, w2):
    b, n_pages = page_table.shape
    ncmp = cache.shape[4]
    cw = ex.shape[2]
    page = cache.shape[2]
    npp = next(k for k in (PAGES_PER_STEP, 4, 2, 1) if n_pages % k == 0)
    rr = n_pages * page // S_CMP
    rt = npp * page // S_CMP
    grid_spec = pltpu.PrefetchScalarGridSpec(
        num_scalar_prefetch=1, grid=(b, n_pages // npp),
        in_specs=_page_specs(cache, lidx, npp, 0) + [pl.BlockSpec(w1cat.shape, lambda i, p, pt: (0, 0, 0))],
        out_specs=pl.BlockSpec((1, ncmp, rt, 2 * LANE), lambda i, p, pt: (i, 0, p, 0)))
    y = pl.pallas_call(
        _compress_paged_kernel, grid_spec=grid_spec,
        out_shape=jax.ShapeDtypeStruct((b, ncmp, rr, 2 * LANE), F32),
        compiler_params=_cp(2), name="nsa_compress_paged",
    )(page_table, *([cache] * npp), w1cat)
    c_of = lambda i: (i // N_KV) % 2
    return pl.pallas_call(
        _compress_finish_kernel, grid=(b * ncmp,),
        in_specs=[pl.BlockSpec((1, rr, 2 * LANE), lambda i: (i, 0, 0)), pl.BlockSpec((1, SUBLANE, cw), lambda i: (i, 0, 0)),
                  pl.BlockSpec((1, cw, 2 * LANE), lambda i: (c_of(i), 0, 0)),
                  pl.BlockSpec((1, LANE, LANE), lambda i: (c_of(i), 0, 0))],
        out_specs=pl.BlockSpec((1, rr, LANE), lambda i: (i, 0, 0)),
        out_shape=jax.ShapeDtypeStruct((b * ncmp, rr, LANE), F32), compiler_params=_cp(1), name="nsa_compress_finish",
    )(y.reshape(b * ncmp, rr, 2 * LANE), ex, w1cat, w2)


def _bias_gather(tab_row, dist):
    idx = jnp.clip(dist, 0, FAR)
    return jnp.take_along_axis(jnp.broadcast_to(tab_row, idx.shape), idx, axis=1)


def _stack_heads(q_ref, col0=0, hpg=None):
    hpg = q_ref.shape[1] // HEAD_B if hpg is None else hpg
    return jnp.concatenate([q_ref[:, col0 + h * HEAD_B:col0 + (h + 1) * HEAD_B] for h in range(hpg)], axis=0)


def _rel_bias(dist, g, hpg, tab_ref):
    rel = []
    for h in range(hpg):
        tab_row = tab_ref[pl.ds(g * hpg + h, 1), :]
        far = tab_row[:, FAR:FAR + 1]
        rel.append(jnp.concatenate([_bias_gather(tab_row, dist[:, c * LANE:(c + 1) * LANE]) - far
                                    for c in range(dist.shape[1] // LANE)], axis=1))
    return jnp.stack(rel)


def _online_softmax_update(qs, tiles):
    hpg, tq, _ = tiles[0][4].shape
    scores = [_dot_nt(qs, kb) for kb, *_ in tiles]
    probs, alphas = [], []
    for s, (kb, vb, mask, rel, m_ref, l_ref, acc_ref) in zip(scores, tiles):
        width = kb.shape[0]
        s = s.reshape(hpg, tq, width) * (HEAD_B ** -0.5)
        if rel is None:
            s = s + jnp.where(mask, 0.0, 2 * NEG)[None]
        else:
            s = s + jnp.where(mask[None], rel, 2 * NEG)
        m_prev = m_ref[...][:, :, :1]
        m_new = jnp.maximum(m_prev, jnp.max(s, axis=2, keepdims=True))
        alpha = jnp.exp(m_prev - m_new)
        p = jnp.exp(s - m_new)
        m_ref[...] = jnp.broadcast_to(m_new, m_ref.shape)
        probs.append(p.reshape(hpg * tq, width).astype(BF16))
        alphas.append(alpha)
    pvs = [_dot(p, jnp.concatenate([t[1], jnp.ones_like(t[1])], axis=1)) for p, t in zip(probs, tiles)]
    for pv, alpha, t in zip(pvs, alphas, tiles):
        pv = pv.reshape(hpg, tq, 2 * HEAD_B)
        t[5][...] = alpha * t[5][...] + pv[:, :, HEAD_B:]
        t[6][...] = alpha * t[6][...] + pv[:, :, :HEAD_B]


def _cmp_attn_kernel(q_ref, kc_ref, vc_ref, ovt_ref, tab_ref, o_ref, imp_ref, *, tq, q0, lanes):
    g = pl.program_id(1)
    qt = pl.program_id(2)
    hpg = q_ref.shape[2] // HEAD_B
    rr = kc_ref.shape[1]
    kc = kc_ref[0].astype(BF16)
    vc = vc_ref[0].astype(BF16)
    qbase = q0 + qt * tq
    qrow = qbase + lax.broadcasted_iota(jnp.int32, (tq, LANE), 0)
    dists = []
    for cix in range(rr // LANE):
        c_end = S_CMP * (cix * LANE + lax.broadcasted_iota(jnp.int32, (tq, LANE), 1)) + (L_CMP - 1)
        dists.append(qrow - c_end)
    mask = (jnp.concatenate(dists, axis=1) >= 0)[None]
    bias = jnp.stack([jnp.concatenate([_bias_gather(tab_ref[pl.ds(g * hpg + h, 1), :], d) for d in dists], axis=1)
                      for h in range(hpg)])
    s = _dot_nt(_stack_heads(q_ref.at[0]), kc).reshape(hpg, tq, rr) * (HEAD_B ** -0.5) + bias
    s = jnp.where(mask, s, NEG)
    e = jnp.where(mask, jnp.exp(s - jnp.max(s, axis=2, keepdims=True)), 0.0)
    p = e / jnp.maximum(jnp.sum(e, axis=2, keepdims=True), 1e-30)
    o = _dot(p.reshape(hpg * tq, rr).astype(BF16), vc).reshape(hpg, tq, HEAD_B)
    o_ref[0] = jnp.concatenate([o[h] for h in range(hpg)], axis=1)
    psum = jnp.sum(p, axis=0)
    if tq < lanes:
        psum = jnp.concatenate([psum, jnp.zeros((lanes - tq, rr), F32)], axis=0)
    ph, plo = _split(psum)
    ovt = ovt_ref[...]
    imp_ref[0, 0] = _dot_nt(ovt, ph) + _dot_nt(ovt, plo)


def _select_kernel(imp_ref, qpos_ref, selt_ref, *, n_top):
    nsr, lanes = imp_ref.shape[2:]
    j = lax.broadcasted_iota(jnp.int32, (nsr, lanes), 0)
    cur = qpos_ref[...] // L_SEL
    valid = j <= cur
    forced = (j == 0) | (j == cur) | (j == cur - 1)
    score = jnp.where(valid, imp_ref[0, 0] + jnp.where(forced, FORCE_BONUS, 0.0), NEG)

    def body(_, carry):
        score, sel = carry
        top = jnp.max(score, axis=0, keepdims=True)
        first = jnp.min(jnp.where(score == top, j, nsr), axis=0, keepdims=True)
        pick = j == first
        sel = jnp.where(pick & (top > NEG / 2), 1.0, sel)
        return jnp.where(pick, 2 * NEG, score), sel

    _, sel = lax.fori_loop(0, n_top, body, (score, jnp.zeros((nsr, lanes), F32)))
    selt_ref[0, 0] = sel


def _select(imp, qpos, ns):
    x, gg, nsr, l = imp.shape
    lt = next(c for c in (256, 128) if l % c == 0)
    blk = pl.BlockSpec((1, 1, nsr, lt), lambda i, g, t: (i, g, 0, t))
    return pl.pallas_call(
        functools.partial(_select_kernel, n_top=min(TOP_N, ns)), grid=(x, gg, l // lt),
        in_specs=[blk, pl.BlockSpec((1, lt), lambda i, g, t: (0, t))], out_specs=blk,
        out_shape=jax.ShapeDtypeStruct(imp.shape, F32), compiler_params=_cp(3), name="nsa_select",
    )(imp, qpos)


def _cmp_attn(q, kcv, tab, *, t_keys, q0, tq):
    b, tqa, d_b = q.shape
    rr = kcv.shape[1]
    gw = d_b // N_KV
    ns = -(-t_keys // L_SEL)
    nsr = _round_up(ns, SUBLANE)
    lanes = max(tq, LANE)
    nqt = tqa // tq
    ci = np.arange(rr)
    sj = np.arange(nsr)
    ov = ((S_CMP * ci) // L_SEL)[None, :] == sj[:, None]
    ov |= ((S_CMP * ci + L_CMP - 1) // L_SEL)[None, :] == sj[:, None]
    ovt = jnp.asarray(ov.astype(np.float32), BF16)
    kern = functools.partial(_cmp_attn_kernel, tq=tq, q0=q0, lanes=lanes)
    return pl.pallas_call(
        kern, grid=(b, N_KV, nqt),
        in_specs=[pl.BlockSpec((1, tq, gw), lambda i, g, t: (i, t, g)),
                  pl.BlockSpec((1, rr, LANE), lambda i, g, t: (i * 4 + g, 0, 0)),
                  pl.BlockSpec((1, rr, LANE), lambda i, g, t: (i * 4 + N_KV + g, 0, 0)),
                  pl.BlockSpec(ovt.shape, lambda i, g, t: (0, 0)),
                  pl.BlockSpec(tab.shape, lambda i, g, t: (0, 0))],
        out_specs=[pl.BlockSpec((1, tq, gw), lambda i, g, t: (i, t, g)),
                   pl.BlockSpec((1, 1, nsr, lanes), lambda i, g, t: (i, g, 0, t))],
        out_shape=[jax.ShapeDtypeStruct((b, tqa, d_b), F32),
                   jax.ShapeDtypeStruct((b, N_KV, nsr, nqt * lanes), F32)],
        compiler_params=_cp(3), name="nsa_cmp_attn",
    )(q, kcv, kcv, ovt, tab)


def _win_cached_kernel(q_ref, k_ref, v_ref, kt_ref, vt_ref, tab_ref, o_ref, m_ref, l_ref, acc_ref,
                       *, q0, kbase, n_tail):
    g = pl.program_id(1)
    tq = q_ref.shape[0]
    hpg = q_ref.shape[1] // HEAD_B
    qs = _stack_heads(q_ref)
    m_ref[...] = jnp.full_like(m_ref, NEG)
    l_ref[...] = jnp.zeros_like(l_ref)
    acc_ref[...] = jnp.zeros_like(acc_ref)

    def fold(k_r, v_r, kpos0, n_valid):
        w = k_r.shape[0]
        col = lax.broadcasted_iota(jnp.int32, (tq, w), 1)
        dist = q0 + lax.broadcasted_iota(jnp.int32, (tq, w), 0) - (kpos0 + col)
        mask = (dist >= 0) & (dist < WINDOW)
        if n_valid is not None:
            mask &= col < n_valid
        _online_softmax_update(qs, [(k_r[...].astype(BF16), v_r[...].astype(BF16), mask,
                                     _rel_bias(dist, g, hpg, tab_ref), m_ref, l_ref, acc_ref)])

    fold(k_ref, v_ref, kbase, None)
    fold(kt_ref, vt_ref, q0, n_tail)
    out = acc_ref[...] / jnp.maximum(l_ref[...][:, :, :1], 1e-30)
    o_ref[...] = jnp.concatenate([out[h] for h in range(hpg)], axis=1)


def _sel_paged_kernel(pt_ref, *refs, ns, n_tail, p_len):
    npp = len(refs) - 8
    q_ref, x_refs = refs[0], refs[1:1 + npp]
    tail_ref, sel_ref, tab_ref, o_ref, m_ref, l_ref, acc_ref = refs[1 + npp:]
    j = pl.program_id(1)
    nj = pl.num_programs(1)
    tq = q_ref.shape[0]
    gw = q_ref.shape[1] // N_KV
    hpg = gw // HEAD_B
    page = x_refs[0].shape[0] // ns
    width = npp * page

    @pl.when(j == 0)
    def _():
        m_ref[...] = jnp.full_like(m_ref, NEG)
        l_ref[...] = jnp.zeros_like(l_ref)
        acc_ref[...] = jnp.zeros_like(acc_ref)

    def fold(g, kb, vb, kpos0, n_valid, near):
        w = kb.shape[0]
        col = lax.broadcasted_iota(jnp.int32, (tq, w), 1)
        dist = p_len + lax.broadcasted_iota(jnp.int32, (tq, w), 0) - (kpos0 + col)
        mask = dist >= 0
        if n_valid is not None:
            mask &= col < n_valid
        nsp = sel_ref.shape[2]
        blk = (kpos0 + lax.broadcasted_iota(jnp.int32, (nsp, w), 1)) // L_SEL
        expand = (blk == lax.broadcasted_iota(jnp.int32, (nsp, w), 0)).astype(BF16)
        mask &= _dot(sel_ref[g].astype(BF16), expand) > 0.5
        rel = _rel_bias(dist, g, hpg, tab_ref) if near else None
        _online_softmax_update(_stack_heads(q_ref, g * gw, hpg),
                               [(kb, vb, mask, rel, m_ref.at[g], l_ref.at[g], acc_ref.at[g])])

    def pages(stream):
        return jnp.concatenate([x[pl.ds(stream, page, stride=ns), :] for x in x_refs], axis=0).astype(BF16)

    def main(near):
        for g in range(N_KV):
            fold(g, pages(2 * N_KV + g), pages(3 * N_KV + g), j * width, None, near)

    @pl.when(j < nj - 1)
    def _():
        main(False)

    @pl.when(j == nj - 1)
    def _():
        main(True)
        for g in range(N_KV):
            kt = tail_ref[:, (2 * N_KV + g) * LANE:(2 * N_KV + g + 1) * LANE].astype(BF16)
            vt = tail_ref[:, (3 * N_KV + g) * LANE:(3 * N_KV + g + 1) * LANE].astype(BF16)
            fold(g, kt, vt, p_len, n_tail, True)
        out = acc_ref[...] / jnp.maximum(l_ref[...][:, :, :, :1], 1e-30)
        o_ref[...] = jnp.concatenate([out[g, h] for g in range(N_KV) for h in range(hpg)], axis=1)


def _sel_paged(q, cache, lidx, page_table, rows_t, n_tail, sel, tab):
    b, tq, d_b = q.shape
    n_pages = page_table.shape[1]
    ns = 4 * N_KV
    page = cache.shape[2] // ns
    npp = next(k for k in (PAGES_PER_STEP, 4, 2, 1) if n_pages % k == 0)
    assert npp * page > FAR
    hpg = d_b // N_KV // HEAD_B
    whole = lambda a: pl.BlockSpec((None,) + a.shape[1:], lambda i, p, pt: (i,) + (0,) * (a.ndim - 1))
    grid_spec = pltpu.PrefetchScalarGridSpec(
        num_scalar_prefetch=1, grid=(b, n_pages // npp),
        in_specs=[whole(q)] + _page_specs(cache, lidx, npp) + [whole(rows_t), whole(sel),
                  pl.BlockSpec(tab.shape, lambda i, p, pt: (0, 0))],
        out_specs=pl.BlockSpec((None, tq, d_b), lambda i, p, pt: (i, 0, 0)),
        scratch_shapes=[pltpu.VMEM((N_KV, hpg, tq, LANE), F32), pltpu.VMEM((N_KV, hpg, tq, LANE), F32),
                        pltpu.VMEM((N_KV, hpg, tq, HEAD_B), F32)])
    return pl.pallas_call(
        functools.partial(_sel_paged_kernel, ns=ns, n_tail=n_tail, p_len=n_pages * page), grid_spec=grid_spec,
        out_shape=jax.ShapeDtypeStruct((b, tq, d_b), F32), compiler_params=_cp(2), name="nsa_sel_attn_paged",
    )(page_table, q, *([cache] * npp), rows_t, sel, tab)


def _win_cached(q, cwin, win_t, n_tail, tab, p_len):
    b, tq, d_b = q.shape
    gw = d_b // N_KV
    hpg = gw // HEAD_B
    wb = cwin.shape[1]
    col = lambda rows, c0: pl.BlockSpec((None, rows, LANE), lambda i, g: (i, 0, c0 + g))
    qspec = pl.BlockSpec((None, tq, gw), lambda i, g: (i, 0, g))
    return pl.pallas_call(
        functools.partial(_win_cached_kernel, q0=p_len, kbase=p_len - wb, n_tail=n_tail), grid=(b, N_KV),
        in_specs=[qspec, col(wb, 0), col(wb, N_KV), col(LANE, 0), col(LANE, N_KV),
                  pl.BlockSpec(tab.shape, lambda i, g: (0, 0))],
        out_specs=qspec, out_shape=jax.ShapeDtypeStruct((b, tq, d_b), F32),
        scratch_shapes=[pltpu.VMEM((hpg, tq, LANE), F32), pltpu.VMEM((hpg, tq, LANE), F32),
                        pltpu.VMEM((hpg, tq, HEAD_B), F32)],
        compiler_params=_cp(2), name="nsa_win_attn_cached",
    )(q, cwin, cwin, win_t, win_t, tab)


def _attn_res_kernel(*refs, mode, tq, ch, far_w):
    if mode == "sel":
        q_ref, k_ref, v_ref, sel_ref, tab_ref, o_ref, m_ref, l_ref, acc_ref = refs
    else:
        q_ref, k_ref, v_ref, tab_ref, o_ref, m_ref, l_ref, acc_ref = refs
    g = pl.program_id(1)
    qt = pl.program_id(2)
    hpg = q_ref.shape[1] // HEAD_B
    qbase = qt * tq
    near0 = qbase - LANE
    qs = _stack_heads(q_ref)
    m_ref[...] = jnp.full_like(m_ref, NEG)
    l_ref[...] = jnp.zeros_like(l_ref)
    acc_ref[...] = jnp.zeros_like(acc_ref)

    def tile(stream, kpos0, width, near):
        qrow = qbase + lax.broadcasted_iota(jnp.int32, (tq, width), 0)
        kcol = kpos0 + lax.broadcasted_iota(jnp.int32, (tq, width), 1)
        dist = qrow - kcol
        mask = (dist >= 0) if near else (kcol < near0)
        if mode == "win":
            mask &= dist < WINDOW
        else:
            nsp = sel_ref.shape[1]
            blk = (kpos0 + lax.broadcasted_iota(jnp.int32, (nsp, width), 1)) // L_SEL
            expand = (blk == lax.broadcasted_iota(jnp.int32, (nsp, width), 0)).astype(BF16)
            mask &= _dot(sel_ref[...].astype(BF16), expand) > 0.5
        kb = k_ref[pl.ds(kpos0, width), :].astype(BF16)
        vb = v_ref[pl.ds(kpos0, width), :].astype(BF16)
        rel = _rel_bias(dist, g, hpg, tab_ref) if near else None
        return (kb, vb, mask, rel, m_ref.at[stream], l_ref.at[stream], acc_ref.at[stream])

    near = lambda: tile(1, pl.multiple_of(jnp.maximum(near0, 0), LANE), 2 * LANE, True)
    if mode == "sel":
        def body(c, carry):
            _online_softmax_update(qs, [tile(0, pl.multiple_of(c * ch, ch), ch, False)])
            return carry

        lax.fori_loop(0, (jnp.maximum(near0, 0) + ch - 1) // ch, body, 0)
        _online_softmax_update(qs, [near()])
    elif far_w:
        far = tile(0, pl.multiple_of(jnp.maximum(qbase - WINDOW, 0), LANE), far_w, False)
        _online_softmax_update(qs, [far, near()])
    else:
        _online_softmax_update(qs, [near()])
    m0, m1 = m_ref[0][:, :, :1], m_ref[1][:, :, :1]
    m = jnp.maximum(m0, m1)
    w0, w1 = jnp.exp(m0 - m), jnp.exp(m1 - m)
    l = w0 * l_ref[0][:, :, :1] + w1 * l_ref[1][:, :, :1]
    out = (w0 * acc_ref[0] + w1 * acc_ref[1]) / jnp.maximum(l, 1e-30)
    o_ref[...] = jnp.concatenate([out[h] for h in range(hpg)], axis=1)


def _attn_res(q, kv, kcol, vcol, tab, *, mode, sel=None, name):
    b, t, d_b = q.shape
    gw = d_b // N_KV
    hpg = gw // HEAD_B
    tq = LANE
    assert t % tq == 0 and t >= 2 * LANE
    ch = next(c for c in (1024, 512, 256, 128) if t % c == 0)
    far_w = min(WINDOW - LANE, t - 2 * LANE)
    args = [q, kv, kv]
    in_specs = [pl.BlockSpec((None, tq, gw), lambda i, g, qt: (i, qt, g)),
                pl.BlockSpec((None, t, LANE), lambda i, g, qt: (i, 0, kcol + g)),
                pl.BlockSpec((None, t, LANE), lambda i, g, qt: (i, 0, vcol + g))]
    if mode == "sel":
        args.append(sel)
        in_specs.append(pl.BlockSpec((None, None, tq, sel.shape[3]), lambda i, g, qt: (i, g, qt, 0)))
    args.append(tab)
    in_specs.append(pl.BlockSpec(tab.shape, lambda i, g, qt: (0, 0)))
    return pl.pallas_call(
        functools.partial(_attn_res_kernel, mode=mode, tq=tq, ch=ch, far_w=far_w),
        grid=(b, N_KV, t // tq), in_specs=in_specs,
        out_specs=pl.BlockSpec((None, tq, gw), lambda i, g, qt: (i, qt, g)),
        out_shape=jax.ShapeDtypeStruct((b, t, d_b), F32),
        scratch_shapes=[pltpu.VMEM((2, hpg, tq, LANE), F32), pltpu.VMEM((2, hpg, tq, LANE), F32),
                        pltpu.VMEM((2, hpg, tq, HEAD_B), F32)],
        compiler_params=_cp(3), name=name,
    )(*args)


def _nsa_out_kernel(oc_ref, os_ref, ow_ref, gl_ref, e_ref, w_ref, y_ref):
    gate = _sigmoid(gl_ref[...])
    o = (_dot_x(gate, e_ref[0]) * oc_ref[...] + _dot_x(gate, e_ref[1]) * os_ref[...]
         + _dot_x(gate, e_ref[2]) * ow_ref[...])
    y_ref[...] = _dot(o.astype(BF16), w_ref[...]).astype(y_ref.dtype)


def _nsa_out(o_c, o_s, o_w, gl, w_ob):
    m, d_b = o_c.shape
    n = w_ob.shape[1]
    h_b = d_b // HEAD_B
    tm = _tile(m, 256)
    e = np.zeros((3, LANE, d_b), np.float32)
    for br in range(3):
        for hh in range(h_b):
            e[br, br * h_b + hh, hh * HEAD_B:(hh + 1) * HEAD_B] = 1.0
    e = jnp.asarray(e, BF16)
    row = pl.BlockSpec((tm, d_b), lambda i: (i, 0))
    return pl.pallas_call(
        _nsa_out_kernel, grid=(m // tm,),
        in_specs=[row, row, row, pl.BlockSpec((tm, LANE), lambda i: (i, 0)),
                  pl.BlockSpec(e.shape, lambda i: (0, 0, 0)), pl.BlockSpec(w_ob.shape, lambda i: (0, 0))],
        out_specs=pl.BlockSpec((tm, n), lambda i: (i, 0)), out_shape=jax.ShapeDtypeStruct((m, n), BF16),
        compiler_params=_cp(1), name="nsa_out",
    )(o_c, o_s, o_w, gl, e, w_ob)


def _nsa(q, rows, win, gl, tab, lw, *, t_real, paged, lidx, cwin):
    b, tqa, d_b = q.shape
    t = t_real
    ncmp = 2 * N_KV
    cw = S_CMP * LANE
    p_len = 0 if paged is None else paged[1].shape[1] * paged[0].shape[2]
    l_tot = p_len + t
    assert p_len % (S_CMP * SUBLANE) == 0 and (p_len == 0 or t <= S_CMP) and (p_len > 0 or t % LANE == 0)
    pe = jnp.repeat(lw["cmp_pe"].reshape(2, 1, 2, cw), N_KV, axis=1)
    pe = jnp.broadcast_to(pe[None], (b, 2, N_KV, 2, cw)).reshape(b * ncmp, 2, cw)
    pad_rows = jnp.zeros((b * ncmp, SUBLANE - 3, cw), F32)
    if paged is None:
        ex = jnp.concatenate([jnp.zeros((b * ncmp, 1, cw), F32), pe, pad_rows], axis=1)
        x_spec = pl.BlockSpec((None, t, LANE), lambda i: (i // ncmp, 0, i % ncmp))
        kcv = _compress(rows, x_spec, t, ex, lw["w1cat"], lw["w2"])
    else:
        new_cmp = rows[:, :, :ncmp * LANE].reshape(b, t, ncmp, LANE).transpose(0, 2, 1, 3)
        ex0 = jnp.pad(new_cmp.reshape(b * ncmp, 1, t * LANE), ((0, 0), (0, 0), (0, cw - t * LANE)))
        ex = jnp.concatenate([ex0, pe, pad_rows], axis=1)
        kcv = _compress_paged(paged[0], lidx, paged[1], ex, lw["w1cat"], lw["w2"])
    if kcv.shape[1] % LANE:
        kcv = jnp.pad(kcv, ((0, 0), (0, _round_up(kcv.shape[1], LANE) - kcv.shape[1]), (0, 0)))
    tq = tqa if p_len else next(c for c in (4 * LANE, 2 * LANE, LANE) if tqa % c == 0)
    o_c, imp = _cmp_attn(q, kcv, tab, t_keys=l_tot, q0=p_len, tq=tq)
    ns = -(-l_tot // L_SEL)
    nsp = _round_up(ns, LANE)
    if p_len == 0:
        sel = jnp.swapaxes(_select(imp, jnp.arange(tqa, dtype=jnp.int32)[None], ns), 2, 3)
    else:
        nq = b * tqa
        nql = _round_up(nq, LANE)
        nsr = imp.shape[2]
        impl = jnp.pad(imp[..., :tqa].transpose(1, 2, 0, 3).reshape(1, N_KV, nsr, nq),
                       ((0, 0), (0, 0), (0, 0), (0, nql - nq)))
        qpos = (p_len + jnp.arange(nql, dtype=jnp.int32) % tqa)[None]
        sel = _select(impl, qpos, ns)[0, :, :, :nq].reshape(N_KV, nsr, b, tqa).transpose(2, 0, 3, 1)
    sel = jnp.pad(sel, ((0, 0), (0, 0), (0, 0), (0, nsp - sel.shape[3])))
    if paged is None:
        o_s = _attn_res(q, rows, 2 * N_KV, 3 * N_KV, tab, mode="sel", sel=sel, name="nsa_sel_attn")
        o_w = _attn_res(q, win, 0, N_KV, tab, mode="win", name="nsa_win_attn")
    else:
        rows_t = jnp.pad(rows, ((0, 0), (0, LANE - t), (0, 0)))
        win_t = jnp.pad(win, ((0, 0), (0, LANE - t), (0, 0)))
        o_s = _sel_paged(q, paged[2], lidx, paged[1], rows_t, t, sel, tab)
        o_w = _win_cached(q, cwin, win_t, t, tab, p_len)
    m = b * tqa
    return _nsa_out(o_c.reshape(m, d_b), o_s.reshape(m, d_b), o_w.reshape(m, d_b), gl, lw["w_ob"])


def _layer_weights(l, w_in, mu, w0, w_up, a0, a_up, g_up, k_k, k_a, r_k, lnx_w, lnx_b, w_oa,
                   cmp_pe, cmp_w1, cmp_w2, w_ob, w_o, w_ff_up, w_ff_down, norm_g):
    d_a = w_oa.shape[1]
    d_b = w_ob.shape[1]
    d = w_o.shape[1]
    rw = 3 * d_a + DECAY_LORA + A_LORA + GATE_LORA
    pw = 3 * d_a + LORA_W
    nrow = 4 * N_KV * HEAD_B
    nwin = 2 * N_KV * HEAD_B
    ngl = 3 * (d_b // HEAD_B)
    wi = w_in[l]
    o_q = rw
    o_rows = o_q + d_b
    o_win = o_rows + nrow
    o_gl = o_win + nwin
    o_pg = o_gl + ngl
    wl = jnp.zeros((LORA_W, 3 * d_a), F32)
    wl = wl.at[:DECAY_LORA, :d_a].set(w_up[l])
    wl = wl.at[DECAY_LORA:DECAY_LORA + A_LORA, d_a:2 * d_a].set(a_up[l])
    wl = wl.at[DECAY_LORA + A_LORA:DECAY_LORA + A_LORA + GATE_LORA, 2 * d_a:].set(g_up[l])
    half = S_CMP * HEAD_B
    return dict(
        w_pa=jnp.pad(wi[:, :rw], ((0, 0), (0, pw - rw))).astype(BF16),
        w_q=wi[:, o_q:o_rows].astype(BF16),
        w_rows=wi[:, o_rows:o_win].astype(BF16),
        w_win=wi[:, o_win:o_gl].astype(BF16),
        w_gl=jnp.pad(wi[:, o_gl:o_pg], ((0, 0), (0, LANE - ngl))).astype(BF16),
        w_pg=wi[:, o_pg:o_pg + 2 * d].astype(BF16),
        mu=jnp.pad(mu[l], (0, pw - rw))[None], wl=wl.astype(BF16),
        w0=w0[l][None], a0=a0[l][None], k_k=k_k[l][None], k_a=k_a[l][None],
        r_k=r_k[l].reshape(1, d_a), lnx_w=lnx_w[l][None], lnx_b=lnx_b[l][None],
        w_oa=w_oa[l].astype(BF16), w_ob=w_ob[l].astype(BF16), w_o=w_o[l].astype(BF16),
        w_up=w_ff_up[l].astype(BF16), w_down=w_ff_down[l].astype(BF16),
        cmp_pe=cmp_pe[l],
        w1cat=jnp.concatenate([cmp_w1[l][:, :half], cmp_w1[l][:, half:]], axis=2).astype(BF16),
        w2=cmp_w2[l].astype(BF16), g=norm_g[l], rw=rw,
    )


def _layer(x, xn, lw, g_next, tab, shift0, s0, paged, lidx, cwin, rows_buf):
    b, t, d = x.shape
    m = b * t
    d_a = lw["w_oa"].shape[0]
    x2 = x.reshape(m, d)
    g = lw["g"]
    pa = _matmul(xn, lw["w_pa"], F32, name="proj_rwkv")
    q = _matmul(xn, lw["w_q"], BF16, name="proj_q")
    if rows_buf is None:
        rows = _matmul(xn, lw["w_rows"], F32, name="proj_rows")
    else:
        rows, rows_buf = _matmul_rows(xn, lw["w_rows"], rows_buf, lidx)
    win = _matmul(xn, lw["w_win"], F32, name="proj_win")
    gl = _matmul(xn, lw["w_gl"], F32, name="proj_gl")
    pg = _matmul(xn, lw["w_pg"], BF16, name="proj_pg")
    pw = pa.shape[1]
    d_b = q.shape[1]
    tp = t if t % RWKV_CHUNK == 0 else _round_up(t, SUBLANE)
    c = RWKV_CHUNK if t % RWKV_CHUNK == 0 else tp
    pa3 = pa.reshape(b, t, pw)
    pa_p = pa3 if tp == t else jnp.pad(pa3, ((0, 0), (0, tp - t), (0, 0)))
    prev = jnp.pad(shift0, ((0, 0), (0, pw - shift0.shape[1])))[:, None]
    r, lgw, k2, v, kk, a, gg = _rwkv_prep(pa_p, prev, lw["mu"], lw["wl"], lw["w0"], lw["a0"], lw["k_k"], lw["k_a"],
                                          None if tp == t else t)
    o, s_fin = _rwkv_scan(r, lgw, k2, v, kk, a, s0, c)
    flat = lambda u: u.reshape(b * tp, d_a)
    ya = _rwkv_post(flat(o), flat(r), flat(k2), flat(v), flat(gg), lw["lnx_w"], lw["lnx_b"], lw["r_k"], lw["w_oa"])
    if tp != t:
        ya = ya.reshape(b, tp, d)[:, :t].reshape(m, d)
    sh = pa3[:, t - 1, :lw["rw"]]
    tqa = t if t % LANE == 0 else _round_up(t, SUBLANE)
    q3 = q.reshape(b, t, d_b)
    gl_p = gl
    if tqa != t:
        q3 = jnp.pad(q3, ((0, 0), (0, tqa - t), (0, 0)))
        gl_p = jnp.pad(gl.reshape(b, t, LANE), ((0, 0), (0, tqa - t), (0, 0))).reshape(b * tqa, LANE)
    rows3 = rows.reshape(b, t, rows.shape[1])
    win3 = win.reshape(b, t, win.shape[1])
    yb = _nsa(q3, rows3, win3, gl_p, tab, lw, t_real=t, paged=paged, lidx=lidx, cwin=cwin)
    if tqa != t:
        yb = yb.reshape(b, tqa, d)[:, :t].reshape(m, d)
    x1, xn1 = _merge(ya, yb, pg, x2, lw["w_o"], g[1:3])
    h = _matmul(xn1, lw["w_up"], BF16, relu2=True, name="ffn_up")
    x_out, xn_out = _ffn_down(h, lw["w_down"], x1, jnp.concatenate([g[3:4], g_next], axis=0))
    wctx = win3 if cwin is None else jnp.concatenate([cwin, win3], axis=1)
    n_keep = min(WINDOW, wctx.shape[1])
    new_rows = rows3 if rows_buf is None else rows_buf
    return x_out.reshape(b, t, d), xn_out, new_rows, wctx[:, wctx.shape[1] - n_keep:], sh, s_fin


def _pair_states(s):
    b, h, n, _ = s.shape
    s = s.reshape(b, h // 2, 2, n, n)
    z = jnp.zeros((b, h // 2, n, n), s.dtype)
    top = jnp.concatenate([s[:, :, 0], z], axis=3)
    bot = jnp.concatenate([z, s[:, :, 1]], axis=3)
    return jnp.concatenate([top, bot], axis=2)


def _unpair_states(s):
    n = HEAD_A
    b, hp = s.shape[:2]
    return jnp.stack([s[:, :, :n, :n], s[:, :, n:, n:]], axis=2).reshape(b, 2 * hp, n, n)


def _trunk(x, shift0, wkv0, paged, cache_win, tab, layers):
    rows, wins, shifts, wkvs = [], [], [], []
    b, t = x.shape[:2]
    ns = 4 * N_KV
    xn = _rmsnorm_cast(x.reshape(-1, x.shape[2]), layers[0]["g"][0:1])
    rows_buf = jnp.zeros((len(layers), b * t * ns, HEAD_B), F32) if (b * t) % LANE == 0 else None
    for l, lw in enumerate(layers):
        g_next = layers[l + 1]["g"][0:1] if l + 1 < len(layers) else lw["g"][3:4]
        x, xn, nr, nw, sh, st = _layer(x, xn, lw, g_next, tab, shift0[l], _pair_states(wkv0[l]),
                                       paged, l, None if cache_win is None else cache_win[l], rows_buf)
        if rows_buf is None:
            rows.append(nr.reshape(b, t, 4, N_KV, HEAD_B))
        else:
            rows_buf = nr
        wins.append(nw.reshape(b, nw.shape[1], 2, N_KV, HEAD_B))
        shifts.append(sh)
        wkvs.append(_unpair_states(st))
    rows = jnp.stack(rows) if rows_buf is None else rows_buf.reshape(len(layers), b, t, 4, N_KV, HEAD_B)
    return x, rows, jnp.stack(wins), jnp.stack(shifts), jnp.stack(wkvs)


def kernel(x_prompt, x_sample, cache_kv, cache_win, state_shift, state_wkv, page_table, w_in, mu, w0, w_up, a0, a_up, g_up, k_k, k_a, r_k, lnx_w, lnx_b, w_oa, cmp_pe, cmp_w1, cmp_w2, w_ob, w_o, w_ff_up, w_ff_down, norm_g, rel_bias):
    depth = w_in.shape[0]
    layers = [_layer_weights(l, w_in, mu, w0, w_up, a0, a_up, g_up, k_k, k_a, r_k, lnx_w, lnx_b, w_oa,
                             cmp_pe, cmp_w1, cmp_w2, w_ob, w_o, w_ff_up, w_ff_down, norm_g) for l in range(depth)]
    tab = rel_bias[_t5_bucket_table()].T
    bp = x_prompt.shape[0]
    rw = state_shift.shape[2]
    zeros_shift = jnp.zeros((depth, bp, rw), x_prompt.dtype)
    zeros_wkv = jnp.zeros((depth, bp) + state_wkv.shape[2:], state_wkv.dtype)
    y_p, kv_p, win_p, sh_p, wkv_p = _trunk(x_prompt, zeros_shift, zeros_wkv, None, None, tab, layers)
    nl, n_pool, page = cache_kv.shape[:3]
    paged = (cache_kv.reshape(nl, n_pool, page, 2, 2 * N_KV, HEAD_B), page_table,
             cache_kv.reshape(nl, n_pool, page * 4 * N_KV, HEAD_B))
    cwin = cache_win.reshape(cache_win.shape[:3] + (-1,))
    y_s, kv_s, win_s, sh_s, wkv_s = _trunk(x_sample, state_shift, state_wkv, paged, cwin, tab, layers)
    return (y_p, y_s, kv_p, kv_s, win_p, win_s, sh_p, sh_s, wkv_p, wkv_s)
```
